```python
import math
import jax, jax.numpy as jnp
from jax import lax
import numpy as np

D_MODEL = 2048
BATCH = 2
SEQ = 4096
DEPTH = 4
DEC_BATCH = 8
DEC_SEQ = 8
PAST_LEN = 16384
PAGE_SIZE = 128

HEAD_DIM = 128
N_HEADS = D_MODEL // HEAD_DIM
H_A = N_HEADS // 2
H_B = N_HEADS // 4
H_C = N_HEADS - H_A - H_B
W_A = H_A * HEAD_DIM
W_B = H_B * HEAD_DIM
W_C = H_C * HEAD_DIM
MOBA_BLOCK = 256
MOBA_TOPK = 3
Q_CHUNK = 64
NUM_BUCKETS = 32
MAX_DISTANCE = 2048
MLSTM_CHUNK = 64
HGRN_CHUNK = 64
CONV_W = 4
D_FF = 4 * D_MODEL
ALPHA = (2 * DEPTH) ** 0.25
BETA = (8 * DEPTH) ** -0.25
EPS = 1e-5
GATE_MASK = -1e30
SPLIT_SIZES = (W_A, W_A, W_A, W_B, W_B, W_B, H_B, H_B, W_B, W_C, W_C, W_C, W_C)
SPLIT_IDX = tuple(sum(SPLIT_SIZES[:i + 1]) for i in range(len(SPLIT_SIZES) - 1))
D_IN = sum(SPLIT_SIZES)

kernel_name = 'hymba_moba_mlstm_hgrn2_step'


def layer_norm(x, g, b):
    xf = x.astype(jnp.float32)
    mu = xf.mean(-1, keepdims=True)
    var = jnp.mean(jnp.square(xf - mu), -1, keepdims=True)
    return ((xf - mu) * lax.rsqrt(var + EPS) * g + b).astype(x.dtype)


def head_norm(x, gain, center):
    xf = x.astype(jnp.float32)
    if center:
        xf = xf - xf.mean(-1, keepdims=True)
    y = xf * lax.rsqrt(jnp.mean(xf * xf, -1, keepdims=True) + EPS)
    return y.reshape(x.shape[0], x.shape[1], -1) * gain


def t5_bucket(dist):
    max_exact = NUM_BUCKETS // 2
    n = jnp.maximum(dist, 0)
    nf = jnp.maximum(n, 1).astype(jnp.float32)
    large = max_exact + (jnp.log(nf / max_exact) / math.log(MAX_DISTANCE / max_exact)
                         * (NUM_BUCKETS - max_exact)).astype(jnp.int32)
    large = jnp.minimum(large, NUM_BUCKETS - 1)
    return jnp.where(n < max_exact, n, large)


def moba_attention(q, k, v, q_pos, rel_bias):
    B, T, H, Dh = q.shape
    L = k.shape[1]
    nb = -(-L // MOBA_BLOCK)
    pad = ((0, 0), (0, nb * MOBA_BLOCK - L), (0, 0), (0, 0))
    kb = jnp.pad(k, pad).reshape(B, nb, MOBA_BLOCK, H, Dh).transpose(0, 3, 1, 2, 4)
    vb = jnp.pad(v, pad).reshape(B, nb, MOBA_BLOCK, H, Dh).transpose(0, 3, 1, 2, 4)
    kmean = jnp.mean(kb.astype(jnp.float32), axis=3)
    k_eff = min(MOBA_TOPK, nb)
    qc = math.gcd(T, Q_CHUNK)
    qs = q.reshape(B, T // qc, qc, H, Dh).transpose(1, 0, 3, 2, 4)
    ps = q_pos.reshape(T // qc, qc)
    bi = jnp.arange(B)[:, None, None, None]
    hi = jnp.arange(H)[None, :, None, None]
    offs = jnp.arange(MOBA_BLOCK)
    scale = Dh ** -0.5

    def one_chunk(args):
        qi, pi = args
        own = pi // MOBA_BLOCK
        gate = jnp.einsum('bhqd,bhnd->bhqn', qi.astype(jnp.float32), kmean)
        gate = jnp.where(jnp.arange(nb)[None, :] < own[:, None], gate, GATE_MASK)
        _, sel = lax.top_k(gate, k_eff)
        slot_ok = jnp.concatenate([jnp.arange(k_eff)[None, :] < jnp.minimum(own, k_eff)[:, None],
                                   jnp.ones((qc, 1), bool)], axis=1)
        blocks = jnp.concatenate([sel, jnp.broadcast_to(own[:, None], (B, H, qc, 1))], axis=-1)
        blocks = jnp.where(slot_ok, blocks, 0)
        kg = kb[bi, hi, blocks]
        vg = vb[bi, hi, blocks]
        dist = pi[:, None, None] - (blocks[..., None] * MOBA_BLOCK + offs)
        logits = (jnp.einsum('bhqd,bhqskd->bhqsk', qi, kg).astype(jnp.float32) * scale
                  + rel_bias[t5_bucket(dist), hi[..., None]].astype(jnp.float32))
        logits = jnp.where(slot_ok[:, :, None] & (dist >= 0), logits, -jnp.inf)
        p = jax.nn.softmax(logits.reshape(B, H, qc, -1), axis=-1).reshape(logits.shape)
        return jnp.einsum('bhqsk,bhqskd->bhqd', p.astype(v.dtype), vg)

    out = lax.map(one_chunk, (qs, ps))
    return out.transpose(1, 0, 3, 2, 4).reshape(B, T, H * Dh)


def causal_conv(u, buf, w, b):
    full = jnp.concatenate([buf.astype(u.dtype), u], axis=1)
    T = u.shape[1]
    out = sum(full[:, j:j + T] * w[j] for j in range(CONV_W)) + b
    return jax.nn.silu(out), full[:, -(CONV_W - 1):]


def mlstm_scan(q, k, v, i_pre, f_pre, C0, n0, m0):
    B, T, H, D = q.shape
    L = math.gcd(T, MLSTM_CHUNK)
    nc = T // L
    f32 = jnp.float32

    def chunked(a):
        return a.astype(f32).reshape(B, nc, L, *a.shape[2:]).swapaxes(0, 1)

    xs = (chunked(q), chunked(k) * D ** -0.5, chunked(v), chunked(i_pre), chunked(f_pre))
    causal = jnp.tril(jnp.ones((L, L), bool))

    def step(carry, xs):
        C, n, m = carry
        qc, kc, vc, ic, fc = xs
        F = jnp.cumsum(jax.nn.log_sigmoid(fc).transpose(0, 2, 1), axis=-1)
        ih = ic.transpose(0, 2, 1)
        Dm = jnp.where(causal, F[..., :, None] - F[..., None, :] + ih[..., None, :], -jnp.inf)
        g = F + m[..., None]
        mt = jnp.maximum(g, Dm.max(-1))
        W = jnp.exp(Dm - mt[..., None])
        wg = jnp.exp(g - mt)
        s = jnp.einsum('bthd,bshd->bhts', qc, kc) * W
        num = (jnp.einsum('bhts,bshd->bthd', s, vc)
               + wg.transpose(0, 2, 1)[..., None] * jnp.einsum('bthd,bhde->bthe', qc, C))
        den = s.sum(-1) + wg * jnp.einsum('bthd,bhd->bht', qc, n)
        h = num / jnp.maximum(jnp.abs(den), jnp.exp(-mt)).transpose(0, 2, 1)[..., None]
        mL = mt[..., -1]
        wl = jnp.exp(Dm[..., -1, :] - mL[..., None])
        gl = jnp.exp(g[..., -1] - mL)
        C = gl[..., None, None] * C + jnp.einsum('bhs,bshd,bshe->bhde', wl, kc, vc)
        n = gl[..., None] * n + jnp.einsum('bhs,bshd->bhd', wl, kc)
        return (C, n, mL), h

    (C, n, m), h = lax.scan(step, (C0.astype(f32), n0.astype(f32), m0.astype(f32)), xs)
    return h.swapaxes(0, 1).reshape(B, T, H, D), C, n, m


def hgrn2_scan(q, k, v, logf, S0):
    B, T, H, K = q.shape
    L = math.gcd(T, HGRN_CHUNK)
    nc = T // L

    def chunked(a):
        return a.reshape(B, nc, L, *a.shape[2:]).swapaxes(0, 1)

    causal = jnp.tril(jnp.ones((L, L), bool))

    def step(S, xs):
        qc, kc, vc, lf = xs
        b = jnp.cumsum(lf, axis=1)
        dec = jnp.where(causal[None, :, :, None, None], b[:, :, None] - b[:, None], -jnp.inf)
        A = jnp.einsum('bthk,bshk,btshk->bhts', qc, kc, jnp.exp(dec))
        o = (jnp.einsum('bhts,bshv->bthv', A, vc)
             + jnp.einsum('bthk,bhkv->bthv', qc * jnp.exp(b), S))
        bl = b[:, -1]
        S = jnp.exp(bl)[..., None] * S + jnp.einsum('bshk,bshv->bhkv', kc * jnp.exp(bl[:, None] - b), vc)
        return S, o

    S, o = lax.scan(step, S0.astype(jnp.float32), (chunked(q), chunked(k), chunked(v), chunked(logf)))
    return o.swapaxes(0, 1).reshape(B, T, H, -1), S


def mixer(h, k_past, v_past, q_pos, C0, n0, m0, conv_buf, S0, lb,
          w_in, b_gate, conv_w, conv_b, gn_b, gn_c, rel_bias, w_out):
    B, T, _ = h.shape
    dt = h.dtype
    f32 = jnp.float32
    proj = h @ w_in
    qa, ka, va, qb, kb, vb, ib, fb, ob, qc, fc, ic, gc = jnp.split(proj, SPLIT_IDX, axis=-1)

    def heads(a, n):
        return a.reshape(B, T, n, HEAD_DIM)

    ka, va = heads(ka, H_A), heads(va, H_A)
    if k_past is None:
        k_all, v_all = ka, va
    else:
        k_all = jnp.concatenate([k_past.astype(dt), ka], axis=1)
        v_all = jnp.concatenate([v_past.astype(dt), va], axis=1)
    ya = moba_attention(heads(qa, H_A), k_all, v_all, q_pos, rel_bias)
    qk, conv_new = causal_conv(jnp.concatenate([qb, kb], axis=-1), conv_buf, conv_w, conv_b)
    qb, kb = jnp.split(qk, 2, axis=-1)
    hb, C, n, m = mlstm_scan(heads(qb, H_B), heads(kb, H_B), heads(vb, H_B),
                             ib + b_gate[:H_B], fb + b_gate[H_B:], C0, n0, m0)
    yb = jax.nn.sigmoid(ob.astype(f32)) * head_norm(hb, gn_b, True)
    fcf = fc.astype(f32)
    f_gate = lb + (1.0 - lb) * jax.nn.sigmoid(fcf)
    logf = jnp.log(f_gate)
    kc = (1.0 - lb) * jax.nn.sigmoid(-fcf)
    oc, S = hgrn2_scan(heads(jax.nn.silu(qc.astype(f32)), H_C), heads(kc, H_C),
                       heads(ic.astype(f32), H_C), heads(logf, H_C), S0)
    yc = head_norm(oc, gn_c, False) * jax.nn.silu(gc.astype(f32))
    y = jnp.concatenate([ya, yb.astype(dt), yc.astype(dt)], axis=-1) @ w_out
    return y, ka, va, C, n, m, conv_new, S


def post_block(x, mix, ln1_g, ln1_b, w_up, w_down, ln2_g, ln2_b):
    h = layer_norm(ALPHA * x + mix, ln1_g, ln1_b)
    ff = jnp.square(jax.nn.relu(h @ w_up)) @ w_down
    return layer_norm(ALPHA * h + ff, ln2_g, ln2_b)


def setup_inputs(seed: int = 0) -> dict:
    key = jax.random.key(seed)
    ks = jax.random.split(key, 32)
    n_pages = PAST_LEN // PAGE_SIZE
    n_used = DEC_BATCH * n_pages
    n_pool = n_used + n_used // 4

    def nrm(k, shape, s=1.0):
        return jax.random.normal(k, shape, jnp.float32) * s

    page_table = jax.random.permutation(ks[0], n_pool)[:n_used].reshape(DEC_BATCH, n_pages).astype(jnp.int32)
    b_gate = jnp.concatenate([nrm(ks[1], (DEPTH, H_B), 0.1), 3.0 + nrm(ks[2], (DEPTH, H_B), 0.5)], axis=-1)
    return {
        'x_prompt': nrm(ks[3], (BATCH, SEQ, D_MODEL)),
        'x_sample': nrm(ks[4], (DEC_BATCH, DEC_SEQ, D_MODEL)),
        'cache_k': nrm(ks[5], (DEPTH, n_pool, PAGE_SIZE, H_A, HEAD_DIM)),
        'cache_v': nrm(ks[6], (DEPTH, n_pool, PAGE_SIZE, H_A, HEAD_DIM)),
        'page_table': page_table,
        'state_b_C': nrm(ks[7], (DEPTH, DEC_BATCH, H_B, HEAD_DIM, HEAD_DIM), 0.1),
        'state_b_n': nrm(ks[8], (DEPTH, DEC_BATCH, H_B, HEAD_DIM), 0.1),
        'state_b_m': nrm(ks[9], (DEPTH, DEC_BATCH, H_B)),
        'state_b_conv': nrm(ks[10], (DEPTH, DEC_BATCH, CONV_W - 1, 2 * W_B)),
        'state_c_S': nrm(ks[11], (DEPTH, DEC_BATCH, H_C, HEAD_DIM, HEAD_DIM), 0.3),
        'w_in': nrm(ks[12], (DEPTH, D_MODEL, D_IN), D_MODEL ** -0.5),
        'b_gate': b_gate,
        'conv_w': nrm(ks[13], (DEPTH, CONV_W, 2 * W_B), CONV_W ** -0.5),
        'conv_b': nrm(ks[14], (DEPTH, 2 * W_B), 0.01),
        'gn_b': 1.0 + nrm(ks[15], (DEPTH, W_B), 0.01),
        'gn_c': 1.0 + nrm(ks[16], (DEPTH, W_C), 0.01),
        'lower_bounds': nrm(ks[17], (DEPTH, W_C), 0.1),
        'rel_bias': nrm(ks[18], (NUM_BUCKETS, H_A), 0.5),
        'w_out': nrm(ks[19], (DEPTH, D_MODEL, D_MODEL), BETA * D_MODEL ** -0.5),
        'ln1_g': 1.0 + nrm(ks[20], (DEPTH, D_MODEL), 0.01),
        'ln1_b': nrm(ks[21], (DEPTH, D_MODEL), 0.01),
        'w_up': nrm(ks[22], (DEPTH, D_MODEL, D_FF), BETA * D_MODEL ** -0.5),
        'w_down': nrm(ks[23], (DEPTH, D_FF, D_MODEL), BETA * D_FF ** -0.5),
        'ln2_g': 1.0 + nrm(ks[24], (DEPTH, D_MODEL), 0.01),
        'ln2_b': nrm(ks[25], (DEPTH, D_MODEL), 0.01),
    }


def reference(x_prompt, x_sample, cache_k, cache_v, page_table, state_b_C, state_b_n, state_b_m,
              state_b_conv, state_c_S, w_in, b_gate, conv_w, conv_b, gn_b, gn_c, lower_bounds,
              rel_bias, w_out, ln1_g, ln1_b, w_up, w_down, ln2_g, ln2_b):
    f32 = jnp.float32
    Bp, Tp, _ = x_prompt.shape
    Bs, Ts, _ = x_sample.shape
    past = page_table.shape[1] * PAGE_SIZE
    pos_p = jnp.arange(Tp, dtype=jnp.int32)
    pos_s = past + jnp.arange(Ts, dtype=jnp.int32)
    sm = jax.nn.softmax(lower_bounds.astype(f32), axis=0)
    lb_all = jnp.cumsum(sm, axis=0) - sm[0]
    zC = jnp.zeros((Bp, H_B, HEAD_DIM, HEAD_DIM), f32)
    zn = jnp.zeros((Bp, H_B, HEAD_DIM), f32)
    zm = jnp.zeros((Bp, H_B), f32)
    zconv = jnp.zeros((Bp, CONV_W - 1, 2 * W_B), x_prompt.dtype)
    zS = jnp.zeros((Bp, H_C, HEAD_DIM, HEAD_DIM), f32)
    xp, xs = x_prompt, x_sample
    kp_l, vp_l, ks_l, vs_l = [], [], [], []
    Cp_l, np_l, mp_l, cp_l, Cs_l, ns_l, ms_l, cs_l, Sp_l, Ss_l = [], [], [], [], [], [], [], [], [], []
    for l in range(DEPTH):
        shared = (lb_all[l], w_in[l], b_gate[l], conv_w[l], conv_b[l], gn_b[l], gn_c[l], rel_bias, w_out[l])
        mix_p, kp, vp, Cp, nP, mP, cP, SP = mixer(xp, None, None, pos_p, zC, zn, zm, zconv, zS, *shared)
        xp = post_block(xp, mix_p, ln1_g[l], ln1_b[l], w_up[l], w_down[l], ln2_g[l], ln2_b[l])
        k_past = cache_k[l][page_table].reshape(Bs, past, H_A, HEAD_DIM)
        v_past = cache_v[l][page_table].reshape(Bs, past, H_A, HEAD_DIM)
        mix_s, kS, vS, Cs, nS, mS, cS, SS = mixer(xs, k_past, v_past, pos_s, state_b_C[l], state_b_n[l],
                                                  state_b_m[l], state_b_conv[l], state_c_S[l], *shared)
        xs = post_block(xs, mix_s, ln1_g[l], ln1_b[l], w_up[l], w_down[l], ln2_g[l], ln2_b[l])
        kp_l.append(kp); vp_l.append(vp); ks_l.append(kS); vs_l.append(vS)
        Cp_l.append(Cp); np_l.append(nP); mp_l.append(mP); cp_l.append(cP)
        Cs_l.append(Cs); ns_l.append(nS); ms_l.append(mS); cs_l.append(cS)
        Sp_l.append(SP); Ss_l.append(SS)
    return (xp, xs,
            jnp.stack(kp_l), jnp.stack(vp_l), jnp.stack(ks_l), jnp.stack(vs_l),
            jnp.stack(Cp_l), jnp.stack(np_l), jnp.stack(mp_l), jnp.stack(cp_l),
            jnp.stack(Cs_l), jnp.stack(ns_l), jnp.stack(ms_l), jnp.stack(cs_l),
            jnp.stack(Sp_l), jnp.stack(Ss_l))
```

```python
import functools
import math

import numpy as np
import jax
import jax.numpy as jnp
from jax import lax
from jax.experimental import pallas as pl
from jax.experimental.pallas import tpu as pltpu

F32 = jnp.float32
BF16 = jnp.bfloat16

HEAD_DIM = 128
H_A, H_B, H_C = 8, 4, 4
W_A, W_B, W_C = H_A * HEAD_DIM, H_B * HEAD_DIM, H_C * HEAD_DIM
MOBA_BLOCK = 256
MOBA_TOPK = 3
NUM_BUCKETS = 32
MAX_DISTANCE = 2048
CONV_W = 4
EPS = 1e-5
GATE_MASK = -1e30
NEG_BIG = -1e30
PAGE_SIZE = 128
LANES = 128
SUBLANES = 8
HGRN_SUB = 32
VMEM_LIMIT = 56 * 1024 * 1024

N_MAIN = 3 * W_A + 4 * W_B + 4 * W_C
GATE_COL0 = 3 * W_A + 3 * W_B


def _t5_thresholds():
    max_exact = NUM_BUCKETS // 2
    n = np.arange(1, 4 * MAX_DISTANCE, dtype=np.float32)
    large = max_exact + (np.log(n / np.float32(max_exact)) / np.float32(math.log(MAX_DISTANCE / max_exact))
                         * np.float32(NUM_BUCKETS - max_exact)).astype(np.int32)
    large = np.minimum(large, NUM_BUCKETS - 1)
    thr = []
    for b in range(max_exact + 1, NUM_BUCKETS):
        thr.append(int(np.argmax(large >= b)) + 1)
    return tuple(thr)


T5_THRESHOLDS = _t5_thresholds()


def _cparams(sem):
    return pltpu.CompilerParams(dimension_semantics=sem, vmem_limit_bytes=VMEM_LIMIT)


def _dot(a, b):
    return jnp.dot(a, b, preferred_element_type=F32)


def _dot_nt(a, b):
    return lax.dot_general(a, b, (((1,), (1,)), ((), ())), preferred_element_type=F32)


def _dot_tn(a, b):
    return lax.dot_general(a, b, (((0,), (0,)), ((), ())), preferred_element_type=F32)


def _dot_hi(a, b):
    return jnp.dot(a, b, precision=lax.Precision.HIGHEST, preferred_element_type=F32)


def _dot_nt_hi(a, b):
    return lax.dot_general(a, b, (((1,), (1,)), ((), ())), precision=lax.Precision.HIGHEST,
                           preferred_element_type=F32)


def _sigmoid(x):
    return 1.0 / (1.0 + jnp.exp(-x))


def _layer_norm(z, g, b):
    mu = jnp.mean(z, axis=-1, keepdims=True)
    zc = z - mu
    var = jnp.mean(zc * zc, axis=-1, keepdims=True)
    return zc * lax.rsqrt(var + EPS) * g + b


def _matmul_kernel(x_ref, w_ref, o_ref):
    o_ref[...] = _dot(x_ref[...], w_ref[...])


def _matmul(x, w, tm, tn):
    M, K = x.shape
    N = w.shape[1]
    return pl.pallas_call(
        _matmul_kernel,
        grid=(M // tm, N // tn),
        in_specs=[pl.BlockSpec((tm, K), lambda i, j: (i, 0)),
                  pl.BlockSpec((K, tn), lambda i, j: (0, j))],
        out_specs=pl.BlockSpec((tm, tn), lambda i, j: (i, j)),
        out_shape=jax.ShapeDtypeStruct((M, N), F32),
        compiler_params=_cparams(("arbitrary", "arbitrary")),
        name="in_proj",
    )(x, w)


def _t5_bias_from_dist(dist, rb_ref, h):
    n = jnp.maximum(dist, 0)
    large = jnp.full(n.shape, NUM_BUCKETS // 2, jnp.int32)
    for thr in T5_THRESHOLDS:
        large = large + (n >= thr).astype(jnp.int32)
    bucket = jnp.where(n < NUM_BUCKETS // 2, n, large)
    val = jnp.zeros(n.shape, F32)
    for b in range(NUM_BUCKETS):
        val = jnp.where(bucket == b, rb_ref[b, h], val)
    return val


def _bias_table_kernel(rb_ref, o_ref):
    d = pl.program_id(0)
    h = pl.program_id(1)
    row = lax.broadcasted_iota(jnp.int32, (MOBA_BLOCK, MOBA_BLOCK), 0)
    col = lax.broadcasted_iota(jnp.int32, (MOBA_BLOCK, MOBA_BLOCK), 1)
    dist = d * MOBA_BLOCK + row - col
    o_ref[...] = _t5_bias_from_dist(dist, rb_ref, h)


def _bias_table(rel_bias, nd):
    return pl.pallas_call(
        _bias_table_kernel,
        grid=(nd, H_A),
        in_specs=[pl.BlockSpec(memory_space=pltpu.SMEM)],
        out_specs=pl.BlockSpec((None, None, MOBA_BLOCK, MOBA_BLOCK), lambda d, h: (d, h, 0, 0)),
        out_shape=jax.ShapeDtypeStruct((nd, H_A, MOBA_BLOCK, MOBA_BLOCK), F32),
        compiler_params=_cparams(("arbitrary", "arbitrary")),
        name="t5_bias_table",
    )(rel_bias)


def _topk_select(gate, n_valid, n_cand):
    lane = lax.broadcasted_iota(jnp.int32, gate.shape, 1)
    gm = jnp.where(lane < n_valid, gate, GATE_MASK)
    rank = jnp.zeros(gate.shape, jnp.int32)
    for c in range(n_cand):
        gc = gm[:, c:c + 1]
        ahead = (gc > gm) | ((gc == gm) & (c < lane))
        rank = rank + ahead.astype(jnp.int32)
    return rank, lane


def _moba_prompt_kernel(q_ref, k_ref, v_ref, bias_ref, o_ref, kb_s, vb_s, kmean_s, *, nb):
    i = pl.program_id(2)
    blk = MOBA_BLOCK

    @pl.when(i == 0)
    def _():
        kmean_s[...] = jnp.zeros(kmean_s.shape, F32)
        for n in range(nb):
            kf = k_ref[n * blk:(n + 1) * blk, :]
            kb_s[n * blk:(n + 1) * blk, :] = kf.astype(BF16)
            vb_s[n * blk:(n + 1) * blk, :] = v_ref[n * blk:(n + 1) * blk, :].astype(BF16)
            kmean_s[n:n + 1, :] = jnp.mean(kf, axis=0, keepdims=True)

    q = q_ref[...]
    gate = _dot_nt_hi(q, kmean_s[...])
    rank, lane = _topk_select(gate, i, nb)
    sel = ((lane < i) & (rank < MOBA_TOPK)).astype(F32)
    qb = q.astype(BF16)
    scale = HEAD_DIM ** -0.5
    row = lax.broadcasted_iota(jnp.int32, (blk, blk), 0)
    col = lax.broadcasted_iota(jnp.int32, (blk, blk), 1)

    r0 = pl.multiple_of(i * blk, blk)
    s = _dot_nt(qb, kb_s[pl.ds(r0, blk), :]) * scale + bias_ref[0]
    s = jnp.where(row >= col, s, NEG_BIG)
    m = jnp.max(s, axis=1, keepdims=True)
    p = jnp.exp(s - m)
    l = jnp.sum(p, axis=1, keepdims=True)
    acc = _dot(p.astype(BF16), vb_s[pl.ds(r0, blk), :])

    def body(j, carry):
        m, l, acc = carry
        flag = jnp.sum(jnp.where(lane == j, sel, 0.0), axis=1, keepdims=True) > 0.0
        rj = pl.multiple_of(j * blk, blk)
        s = _dot_nt(qb, kb_s[pl.ds(rj, blk), :]) * scale + bias_ref[i - j]
        s = jnp.where(flag, s, NEG_BIG)
        m_new = jnp.maximum(m, jnp.max(s, axis=1, keepdims=True))
        alpha = jnp.exp(m - m_new)
        p = jnp.exp(s - m_new)
        l = alpha * l + jnp.sum(p, axis=1, keepdims=True)
        acc = alpha * acc + _dot(p.astype(BF16), vb_s[pl.ds(rj, blk), :])
        return m_new, l, acc

    m, l, acc = lax.fori_loop(0, i, body, (m, l, acc))
    o_ref[...] = (acc / l).astype(o_ref.dtype)


def _moba_prompt(proj, bias_tab, B, T):
    nb = T // MOBA_BLOCK
    return pl.pallas_call(
        functools.partial(_moba_prompt_kernel, nb=nb),
        grid=(B, H_A, nb),
        in_specs=[pl.BlockSpec((MOBA_BLOCK, HEAD_DIM), lambda b, h, i: (b * nb + i, h)),
                  pl.BlockSpec((T, HEAD_DIM), lambda b, h, i: (b, H_A + h)),
                  pl.BlockSpec((T, HEAD_DIM), lambda b, h, i: (b, 2 * H_A + h)),
                  pl.BlockSpec((nb, None, MOBA_BLOCK, MOBA_BLOCK), lambda b, h, i: (0, h, 0, 0))],
        out_specs=pl.BlockSpec((MOBA_BLOCK, HEAD_DIM), lambda b, h, i: (b * nb + i, h)),
        out_shape=jax.ShapeDtypeStruct((B * T, W_A), BF16),
        scratch_shapes=[pltpu.VMEM((T, HEAD_DIM), BF16), pltpu.VMEM((T, HEAD_DIM), BF16),
                        pltpu.VMEM((LANES, HEAD_DIM), F32)],
        compiler_params=_cparams(("arbitrary", "arbitrary", "arbitrary")),
        name="moba_prompt",
    )(proj, proj, proj, bias_tab)


def _mlstm_kernel(q_ref, k_ref, qp_ref, kp_ref, v_ref, og_ref, g_ref, conv0_ref, cw_ref, cb_ref, bg_ref,
                  gn_ref, c0_ref, n0_ref, m0_ref,
                  y_ref, cout_ref, nout_ref, mout_ref,
                  c_s, n_s, m_s, ext_s, *, L):
    c = pl.program_id(1)
    last = pl.num_programs(1) - 1

    @pl.when(c == 0)
    def _():
        c_s[...] = c0_ref[...]
        n_s[...] = n0_ref[...]
        m_s[...] = m0_ref[...]

    def conv_silu(u_ref, up_ref, col0):
        u = u_ref[...]
        tail = jnp.where(c == 0, conv0_ref[:, col0:col0 + W_B], up_ref[L - SUBLANES:L, :])
        ext_s[0:SUBLANES, :] = tail
        ext_s[SUBLANES:SUBLANES + L, :] = u
        acc = u * cw_ref[CONV_W - 1:CONV_W, col0:col0 + W_B] + cb_ref[:, col0:col0 + W_B]
        for j in range(1, CONV_W):
            xj = ext_s[SUBLANES - j:SUBLANES - j + L, :]
            acc = acc + xj * cw_ref[CONV_W - 1 - j:CONV_W - j, col0:col0 + W_B]
        return acc * _sigmoid(acc)

    qc = conv_silu(q_ref, qp_ref, 0)
    kc = conv_silu(k_ref, kp_ref, W_B) * (HEAD_DIM ** -0.5)
    v = v_ref[...]
    og = og_ref[...]

    g = g_ref[...] + bg_ref[...]
    lf = jnp.minimum(g, 0.0) - jnp.log(1.0 + jnp.exp(-jnp.abs(g)))
    row = lax.broadcasted_iota(jnp.int32, (L, L), 0)
    col = lax.broadcasted_iota(jnp.int32, (L, L), 1)
    causal = row >= col
    fcum = _dot_hi(causal.astype(F32), lf)

    for h in range(H_B):
        hs = slice(h * HEAD_DIM, (h + 1) * HEAD_DIM)
        fcol = fcum[:, H_B + h:H_B + h + 1]
        rcol = g[:, h:h + 1] - fcol
        rrow = jnp.sum(jnp.where(row == col, rcol, 0.0), axis=0, keepdims=True)
        dm = jnp.where(causal, fcol + rrow, -jnp.inf)
        mprev = m_s[h:h + 1, 0:1]
        gcol = fcol + mprev
        mt = jnp.maximum(gcol, jnp.max(dm, axis=1, keepdims=True))
        w = jnp.exp(dm - mt)
        wg = jnp.exp(gcol - mt)
        qh = qc[:, hs]
        kh = kc[:, hs]
        vh = v[:, hs]
        qhb = qh.astype(BF16)
        s = _dot_nt(qhb, kh.astype(BF16)) * w
        num = _dot(s.astype(BF16), vh.astype(BF16)) + wg * _dot(qhb, c_s[h].astype(BF16))
        den = jnp.sum(s, axis=1, keepdims=True) + wg * jnp.sum(qh * n_s[h:h + 1, :], axis=1, keepdims=True)
        hh = num / jnp.maximum(jnp.abs(den), jnp.exp(-mt))
        ml = mt[L - 1:L, :]
        wl = jnp.exp(fcol[L - 1:L, :] + rcol - ml)
        gl = jnp.exp(gcol[L - 1:L, :] - ml)
        kw = kh * wl
        c_s[h] = gl * c_s[h] + _dot_tn(kw.astype(BF16), vh.astype(BF16))
        n_s[h:h + 1, :] = gl * n_s[h:h + 1, :] + jnp.sum(kw, axis=0, keepdims=True)
        m_s[h:h + 1, :] = jnp.broadcast_to(ml, (1, LANES))
        hc = hh - jnp.mean(hh, axis=1, keepdims=True)
        yn = hc * lax.rsqrt(jnp.mean(hc * hc, axis=1, keepdims=True) + EPS) * gn_ref[:, hs]
        y_ref[:, hs] = (_sigmoid(og[:, hs]) * yn).astype(y_ref.dtype)

    @pl.when(c == last)
    def _():
        cout_ref[...] = c_s[...]
        nout_ref[...] = n_s[...]
        mout_ref[...] = m_s[...]


def _mlstm(proj, gates, conv0, conv_w, conv_b, bgate, gn_b, c0, n0, m0, B, T, L, out_dtype):
    nc = T // L
    cb = W_B // W_B
    q_blk, k_blk, v_blk, o_blk = 3 * W_A // W_B, 3 * W_A // W_B + 1, 3 * W_A // W_B + 2, 3 * W_A // W_B + 3
    del cb

    def cur(colblk):
        return pl.BlockSpec((L, W_B), lambda b, c: (b * nc + c, colblk))

    def prev(colblk):
        return pl.BlockSpec((L, W_B), lambda b, c: (b * nc + jnp.maximum(c - 1, 0), colblk))

    full2 = lambda shape: pl.BlockSpec(shape, lambda b, c: (0, 0))
    per_b3 = lambda shape: pl.BlockSpec((None,) + shape, lambda b, c: (b, 0, 0))
    return pl.pallas_call(
        functools.partial(_mlstm_kernel, L=L),
        grid=(B, nc),
        in_specs=[cur(q_blk), cur(k_blk), prev(q_blk), prev(k_blk), cur(v_blk), cur(o_blk),
                  pl.BlockSpec((L, LANES), lambda b, c: (b * nc + c, 0)),
                  per_b3((SUBLANES, 2 * W_B)),
                  full2((CONV_W, 2 * W_B)), full2((1, 2 * W_B)), full2((1, LANES)), full2((1, W_B)),
                  pl.BlockSpec((None, H_B, HEAD_DIM, HEAD_DIM), lambda b, c: (b, 0, 0, 0)),
                  per_b3((SUBLANES, HEAD_DIM)), per_b3((SUBLANES, LANES))],
        out_specs=[pl.BlockSpec((L, W_B), lambda b, c: (b * nc + c, 0)),
                   pl.BlockSpec((None, H_B, HEAD_DIM, HEAD_DIM), lambda b, c: (b, 0, 0, 0)),
                   per_b3((SUBLANES, HEAD_DIM)), per_b3((SUBLANES, LANES))],
        out_shape=[jax.ShapeDtypeStruct((B * T, W_B), out_dtype),
                   jax.ShapeDtypeStruct((B, H_B, HEAD_DIM, HEAD_DIM), F32),
                   jax.ShapeDtypeStruct((B, SUBLANES, HEAD_DIM), F32),
                   jax.ShapeDtypeStruct((B, SUBLANES, LANES), F32)],
        scratch_shapes=[pltpu.VMEM((H_B, HEAD_DIM, HEAD_DIM), F32), pltpu.VMEM((SUBLANES, HEAD_DIM), F32),
                        pltpu.VMEM((SUBLANES, LANES), F32), pltpu.VMEM((L + SUBLANES, W_B), F32)],
        compiler_params=_cparams(("arbitrary", "arbitrary")),
        name="mlstm",
    )(proj, proj, proj, proj, proj, proj, gates, conv0, conv_w, conv_b, bgate, gn_b, c0, n0, m0)


def _hgrn_kernel(q_ref, f_ref, i_ref, g_ref, lb_ref, gn_ref, s0_ref, y_ref, sout_ref,
                 st_s, k_s, b_s, v_s, *, LC, LS):
    c = pl.program_id(1)
    last = pl.num_programs(1) - 1

    @pl.when(c == 0)
    def _():
        for h in range(H_C):
            st_s[h] = s0_ref[h].T

    lb = lb_ref[...]
    one_m_lb = 1.0 - lb
    row = lax.broadcasted_iota(jnp.int32, (LS, LS), 0)
    col = lax.broadcasted_iota(jnp.int32, (LS, LS), 1)
    tril = (row >= col).astype(F32)
    row8 = lax.broadcasted_iota(jnp.int32, (SUBLANES, W_C), 0)

    def sub(sc, carry):
        r = pl.multiple_of(sc * LS, LS)
        fc = f_ref[pl.ds(r, LS), :]
        qc = q_ref[pl.ds(r, LS), :]
        logf = jnp.log(lb + one_m_lb * _sigmoid(fc))
        kk = one_m_lb * _sigmoid(-fc)
        qq = qc * _sigmoid(qc)
        vv = i_ref[pl.ds(r, LS), :]
        b = _dot_hi(tril, logf)
        k_s[...] = kk
        b_s[...] = b
        v_s[...] = vv

        o_rows = []
        for rb in range(LS // SUBLANES):
            t0 = rb * SUBLANES
            qb_ = qq[t0:t0 + SUBLANES, :]
            bb_ = b[t0:t0 + SUBLANES, :]
            o_h = [jnp.zeros((SUBLANES, HEAD_DIM), F32) for _ in range(H_C)]
            for s in range(t0 + SUBLANES):
                d = bb_ - b_s[s:s + 1, :]
                if s >= t0:
                    d = jnp.where(row8 >= (s - t0), d, -jnp.inf)
                tmp = qb_ * k_s[s:s + 1, :] * jnp.exp(d)
                vs = v_s[s:s + 1, :]
                for h in range(H_C):
                    hs = slice(h * HEAD_DIM, (h + 1) * HEAD_DIM)
                    a = jnp.sum(tmp[:, hs], axis=1, keepdims=True)
                    o_h[h] = o_h[h] + a * vs[:, hs]
            o_rows.append(o_h)

        qe = qq * jnp.exp(b)
        bl = b[LS - 1:LS, :]
        ke = kk * jnp.exp(bl - b)
        dec = jnp.exp(bl)
        gg = g_ref[pl.ds(r, LS), :]
        for h in range(H_C):
            hs = slice(h * HEAD_DIM, (h + 1) * HEAD_DIM)
            st = st_s[h]
            o_inter = _dot_nt(qe[:, hs].astype(BF16), st.astype(BF16))
            o_intra = o_rows[0][h] if LS == SUBLANES else jnp.concatenate([o_rows[rb][h] for rb in range(LS // SUBLANES)], axis=0)
            o = o_inter + o_intra
            st_s[h] = dec[:, hs] * st + _dot_tn(vv[:, hs].astype(BF16), ke[:, hs].astype(BF16))
            yn = o * lax.rsqrt(jnp.mean(o * o, axis=1, keepdims=True) + EPS) * gn_ref[:, hs]
            gh = gg[:, hs]
            y_ref[pl.ds(r, LS), hs] = (yn * (gh * _sigmoid(gh))).astype(y_ref.dtype)
        return carry

    lax.fori_loop(0, LC // LS, sub, 0)

    @pl.when(c == last)
    def _():
        for h in range(H_C):
            sout_ref[h] = st_s[h].T


def _hgrn(proj, lb, gn_c, s0, B, T, LC, LS, out_dtype):
    nc = T // LC
    base = (3 * W_A + 4 * W_B) // W_C
    blk = lambda k: pl.BlockSpec((LC, W_C), lambda b, c: (b * nc + c, base + k))
    full2 = lambda shape: pl.BlockSpec(shape, lambda b, c: (0, 0))
    st_spec = pl.BlockSpec((None, H_C, HEAD_DIM, HEAD_DIM), lambda b, c: (b, 0, 0, 0))
    return pl.pallas_call(
        functools.partial(_hgrn_kernel, LC=LC, LS=LS),
        grid=(B, nc),
        in_specs=[blk(0), blk(1), blk(2), blk(3), full2((1, W_C)), full2((1, W_C)), st_spec],
        out_specs=[pl.BlockSpec((LC, W_C), lambda b, c: (b * nc + c, 0)), st_spec],
        out_shape=[jax.ShapeDtypeStruct((B * T, W_C), out_dtype),
                   jax.ShapeDtypeStruct((B, H_C, HEAD_DIM, HEAD_DIM), F32)],
        scratch_shapes=[pltpu.VMEM((H_C, HEAD_DIM, HEAD_DIM), F32)] + [pltpu.VMEM((LS, W_C), F32)] * 3,
        compiler_params=_cparams(("arbitrary", "arbitrary")),
        name="hgrn2",
    )(proj, proj, proj, proj, lb, gn_c, s0)


def _outproj_kernel(ya_ref, yb_ref, yc_ref, x_ref, w_ref, g_ref, b_ref, hf_ref, hb_ref, *, alpha):
    mix = (_dot(ya_ref[...].astype(BF16), w_ref[0:W_A, :])
           + _dot(yb_ref[...].astype(BF16), w_ref[W_A:W_A + W_B, :])
           + _dot(yc_ref[...].astype(BF16), w_ref[W_A + W_B:W_A + W_B + W_C, :]))
    h = _layer_norm(alpha * x_ref[...] + mix, g_ref[...], b_ref[...])
    hf_ref[...] = h
    hb_ref[...] = h.astype(BF16)


def _outproj(ya, yb, yc, x, w, g, b, alpha, tm):
    M, D = x.shape
    rows = lambda n: pl.BlockSpec((tm, n), lambda i: (i, 0))
    full = lambda shape: pl.BlockSpec(shape, lambda i: (0, 0))
    return pl.pallas_call(
        functools.partial(_outproj_kernel, alpha=alpha),
        grid=(M // tm,),
        in_specs=[rows(W_A), rows(W_B), rows(W_C), rows(D), full(w.shape), full((1, D)), full((1, D))],
        out_specs=[rows(D), rows(D)],
        out_shape=[jax.ShapeDtypeStruct((M, D), F32), jax.ShapeDtypeStruct((M, D), BF16)],
        compiler_params=_cparams(("arbitrary",)),
        name="out_proj_ln1",
    )(ya, yb, yc, x, w, g, b)


def _mlp_kernel(hb_ref, hf_ref, wu_ref, wd_ref, g_ref, b_ref, of_ref, ob_ref, acc_s, *, alpha):
    f = pl.program_id(1)
    u = jnp.maximum(_dot(hb_ref[...], wu_ref[...]), 0.0)
    part = _dot((u * u).astype(BF16), wd_ref[...])

    @pl.when(f == 0)
    def _():
        acc_s[...] = part

    @pl.when(f > 0)
    def _():
        acc_s[...] += part

    @pl.when(f == pl.num_programs(1) - 1)
    def _():
        o = _layer_norm(alpha * hf_ref[...] + acc_s[...], g_ref[...], b_ref[...])
        of_ref[...] = o
        ob_ref[...] = o.astype(BF16)


def _mlp(hb, hf, wu, wd, g, b, alpha, tm, tf):
    M, D = hf.shape
    FF = wu.shape[1]
    rows = lambda: pl.BlockSpec((tm, D), lambda i, f: (i, 0))
    vec = lambda: pl.BlockSpec((1, D), lambda i, f: (0, 0))
    return pl.pallas_call(
        functools.partial(_mlp_kernel, alpha=alpha),
        grid=(M // tm, FF // tf),
        in_specs=[rows(), rows(), pl.BlockSpec((D, tf), lambda i, f: (0, f)),
                  pl.BlockSpec((tf, D), lambda i, f: (f, 0)), vec(), vec()],
        out_specs=[rows(), rows()],
        out_shape=[jax.ShapeDtypeStruct((M, D), F32), jax.ShapeDtypeStruct((M, D), BF16)],
        scratch_shapes=[pltpu.VMEM((tm, D), F32)],
        compiler_params=_cparams(("arbitrary", "arbitrary")),
        name="mlp_ln2",
    )(hb, hf, wu, wd, g, b)


PAGES_PER_STEP = 16


def _kmean_kernel(pt_ref, *refs):
    page_refs, o_ref = refs[:PAGES_PER_STEP], refs[PAGES_PER_STEP]
    per_blk = MOBA_BLOCK // PAGE_SIZE
    for u in range(PAGES_PER_STEP // per_blk):
        tot = page_refs[per_blk * u][...].sum(axis=0)
        for e in range(1, per_blk):
            tot = tot + page_refs[per_blk * u + e][...].sum(axis=0)
        o_ref[u] = tot * (1.0 / MOBA_BLOCK)


def _cache_kmean(cache_k, page_table):
    depth = cache_k.shape[0]
    B, n_pages = page_table.shape
    nb = n_pages * PAGE_SIZE // MOBA_BLOCK
    steps = n_pages // PAGES_PER_STEP
    blocks_per_step = PAGES_PER_STEP * PAGE_SIZE // MOBA_BLOCK

    def page_spec(u):
        return pl.BlockSpec((None, None, PAGE_SIZE, H_A, HEAD_DIM),
                            lambda l, b, g, pt: (l, pt[b, g * PAGES_PER_STEP + u], 0, 0, 0))

    return pl.pallas_call(
        _kmean_kernel,
        grid_spec=pltpu.PrefetchScalarGridSpec(
            num_scalar_prefetch=1,
            grid=(depth, B, steps),
            in_specs=[page_spec(u) for u in range(PAGES_PER_STEP)],
            out_specs=pl.BlockSpec((None, None, blocks_per_step, H_A, HEAD_DIM),
                                   lambda l, b, g, pt: (l, b, g, 0, 0)),
        ),
        out_shape=jax.ShapeDtypeStruct((depth, B, nb, H_A, HEAD_DIM), F32),
        compiler_params=_cparams(("arbitrary", "arbitrary", "arbitrary")),
        name="cache_kmean",
    )(page_table, *([cache_k] * PAGES_PER_STEP))


def _sample_select_kernel(q_ref, km_ref, o_ref, *, nb):
    out = jnp.zeros(o_ref.shape, jnp.int32)
    lane_o = lax.broadcasted_iota(jnp.int32, o_ref.shape, 1)
    for h in range(H_A):
        hs = slice(h * HEAD_DIM, (h + 1) * HEAD_DIM)
        gate = _dot_nt_hi(q_ref[:, hs], km_ref[:, hs])
        rank, lane = _topk_select(gate, nb, nb)
        for slot in range(MOBA_TOPK):
            pick = (lane < nb) & (rank == slot)
            idx = jnp.sum(jnp.where(pick, lane, 0), axis=1, keepdims=True)
            out = jnp.where(lane_o == h * 4 + slot, idx, out)
    o_ref[...] = out


def _sample_select(proj_s, kmean_pad, B, T, nb):
    return pl.pallas_call(
        functools.partial(_sample_select_kernel, nb=nb),
        grid=(B,),
        in_specs=[pl.BlockSpec((T, W_A), lambda b: (b, 0)),
                  pl.BlockSpec((None, LANES, W_A), lambda b: (b, 0, 0))],
        out_specs=pl.BlockSpec((T, LANES), lambda b: (b, 0)),
        out_shape=jax.ShapeDtypeStruct((B * T, LANES), jnp.int32),
        compiler_params=_cparams(("arbitrary",)),
        name="moba_sample_select",
    )(proj_s, kmean_pad)


def _moba_sample_kernel(sel_ref, pt_ref, rb_ref, q_ref, kn_ref, vn_ref, ck_ref, cv_ref, o_ref,
                        kbuf, vbuf, sem, *, T, past, layer):
    b = pl.program_id(0)
    h = pl.program_id(1)
    per_blk = MOBA_BLOCK // PAGE_SIZE
    n_slots = T * MOBA_TOPK

    def copies(qi, slot, e):
        blk = sel_ref[b * T + qi, h * 4 + slot]
        page = pt_ref[b, blk * per_blk + e]
        idx = qi * MOBA_TOPK + slot
        dst = pl.ds(e * PAGE_SIZE, PAGE_SIZE)
        return (pltpu.make_async_copy(ck_ref.at[layer, page, :, h, :], kbuf.at[idx, dst, :], sem.at[0]),
                pltpu.make_async_copy(cv_ref.at[layer, page, :, h, :], vbuf.at[idx, dst, :], sem.at[1]))

    for qi in range(T):
        for slot in range(MOBA_TOPK):
            for e in range(per_blk):
                ck, cv = copies(qi, slot, e)
                ck.start()
                cv.start()

    q = q_ref[...]
    qb = q.astype(BF16)
    scale = HEAD_DIM ** -0.5
    rowT = lax.broadcasted_iota(jnp.int32, (T, T), 0)
    colT = lax.broadcasted_iota(jnp.int32, (T, T), 1)
    s = _dot_nt(qb, kn_ref[...].astype(BF16)) * scale + _t5_bias_from_dist(rowT - colT, rb_ref, h)
    s = jnp.where(rowT >= colT, s, NEG_BIG)
    m = jnp.max(s, axis=1, keepdims=True)
    p = jnp.exp(s - m)
    l = jnp.sum(p, axis=1, keepdims=True)
    acc = _dot(p.astype(BF16), vn_ref[...].astype(BF16))

    for qi in range(T):
        for slot in range(MOBA_TOPK):
            for e in range(per_blk):
                ck, cv = copies(qi, slot, e)
                ck.wait()
                cv.wait()

    rowB = lax.broadcasted_iota(jnp.int32, (T, MOBA_BLOCK), 0)
    colB = lax.broadcasted_iota(jnp.int32, (T, MOBA_BLOCK), 1)
    for qi in range(T):
        for slot in range(MOBA_TOPK):
            idx = qi * MOBA_TOPK + slot
            blk = sel_ref[b * T + qi, h * 4 + slot]
            dist = past + rowB - (blk * MOBA_BLOCK + colB)
            s = _dot_nt(qb, kbuf[idx].astype(BF16)) * scale + _t5_bias_from_dist(dist, rb_ref, h)
            s = jnp.where(rowB == qi, s, NEG_BIG)
            m_new = jnp.maximum(m, jnp.max(s, axis=1, keepdims=True))
            alpha = jnp.exp(m - m_new)
            p = jnp.exp(s - m_new)
            l = alpha * l + jnp.sum(p, axis=1, keepdims=True)
            acc = alpha * acc + _dot(p.astype(BF16), vbuf[idx].astype(BF16))
            m = m_new
    del n_slots
    o_ref[...] = acc / l


def _moba_sample(sel, page_table, rel_bias, proj_s, cache_k, cache_v, B, T, past, layer):
    n_slots = T * MOBA_TOPK
    return pl.pallas_call(
        functools.partial(_moba_sample_kernel, T=T, past=past, layer=layer),
        grid_spec=pltpu.PrefetchScalarGridSpec(
            num_scalar_prefetch=3,
            grid=(B, H_A),
            in_specs=[pl.BlockSpec((T, HEAD_DIM), lambda b, h, *_: (b, h)),
                      pl.BlockSpec((T, HEAD_DIM), lambda b, h, *_: (b, H_A + h)),
                      pl.BlockSpec((T, HEAD_DIM), lambda b, h, *_: (b, 2 * H_A + h)),
                      pl.BlockSpec(memory_space=pl.ANY),
                      pl.BlockSpec(memory_space=pl.ANY)],
            out_specs=pl.BlockSpec((T, HEAD_DIM), lambda b, h, *_: (b, h)),
            scratch_shapes=[pltpu.VMEM((n_slots, MOBA_BLOCK, HEAD_DIM), F32),
                            pltpu.VMEM((n_slots, MOBA_BLOCK, HEAD_DIM), F32),
                            pltpu.SemaphoreType.DMA((2,))],
        ),
        out_shape=jax.ShapeDtypeStruct((B * T, W_A), F32),
        compiler_params=_cparams(("arbitrary", "arbitrary")),
        name="moba_sample",
    )(sel, page_table, rel_bias, proj_s, proj_s, proj_s, cache_k, cache_v)


def _pad_rows(a, rows):
    return jnp.pad(a, ((0, 0), (0, rows - a.shape[1])) + ((0, 0),) * (a.ndim - 2))


def _mixer_states_in(c0, n0, m0, conv0):
    B = c0.shape[0]
    n0p = _pad_rows(n0, SUBLANES)
    m0p = _pad_rows(jnp.broadcast_to(m0[:, :, None], (B, H_B, LANES)), SUBLANES)
    conv0p = jnp.pad(conv0, ((0, 0), (SUBLANES - (CONV_W - 1), 0), (0, 0)))
    return c0, n0p, m0p, conv0p


def _tile(m, pref):
    return pref if m % pref == 0 else m


def _layer(x_f32, x_bf16, B, T, wts, states, attn_fn, mlstm_chunk, hgrn_chunk):
    (w_main, w_gate, bgate, conv_w, conv_b, gn_b, gn_c, lb, w_out, ln1_g, ln1_b, w_up, w_down,
     ln2_g, ln2_b, alpha) = wts
    c0, n0, m0, conv0, s0 = states
    M = B * T
    tm = _tile(M, 1024)
    proj = _matmul(x_bf16, w_main, tm, 512)
    gates = _matmul(x_bf16, w_gate, tm, LANES)
    ya = attn_fn(proj)
    c0, n0p, m0p, conv0p = _mixer_states_in(c0, n0, m0, conv0)
    y_dtype = BF16 if T % 16 == 0 else F32
    yb, c_new, n_new, m_new = _mlstm(proj, gates, conv0p, conv_w, conv_b, bgate, gn_b, c0, n0p, m0p, B, T,
                                     mlstm_chunk, y_dtype)
    yc, s_new = _hgrn(proj, lb, gn_c, s0, B, T, hgrn_chunk, min(HGRN_SUB, hgrn_chunk), y_dtype)
    tm2 = _tile(M, 512)
    hf, hb = _outproj(ya, yb, yc, x_f32, w_out, ln1_g, ln1_b, alpha, tm2)
    of, ob = _mlp(hb, hf, w_up, w_down, ln2_g, ln2_b, alpha, tm2, 512)
    return of, ob, proj, (c_new, n_new[:, :H_B, :], m_new[:, :H_B, 0], s_new)


def kernel(x_prompt, x_sample, cache_k, cache_v, page_table, state_b_C, state_b_n, state_b_m, state_b_conv,
           state_c_S, w_in, b_gate, conv_w, conv_b, gn_b, gn_c, lower_bounds, rel_bias, w_out, ln1_g, ln1_b,
           w_up, w_down, ln2_g, ln2_b):
    depth = w_in.shape[0]
    Bp, Tp, D = x_prompt.shape
    Bs, Ts, _ = x_sample.shape
    n_pages = page_table.shape[1]
    past = n_pages * PAGE_SIZE
    alpha = (2 * depth) ** 0.25

    sm = jax.nn.softmax(lower_bounds.astype(F32), axis=0)
    lb_all = jnp.cumsum(sm, axis=0) - sm[0]

    w_main = jnp.concatenate([w_in[:, :, :GATE_COL0], w_in[:, :, GATE_COL0 + 2 * H_B:]], axis=-1).astype(BF16)
    w_gate = jnp.pad(w_in[:, :, GATE_COL0:GATE_COL0 + 2 * H_B], ((0, 0), (0, 0), (0, LANES - 2 * H_B))).astype(BF16)
    bgate = jnp.pad(b_gate, ((0, 0), (0, LANES - 2 * H_B)))[:, None, :]
    w_out_b = w_out.astype(BF16)
    w_up_b = w_up.astype(BF16)
    w_down_b = w_down.astype(BF16)

    bias_tab = _bias_table(rel_bias, Tp // MOBA_BLOCK)
    nb_past = past // MOBA_BLOCK
    kmean_all = _cache_kmean(cache_k, page_table).reshape(depth, Bs, nb_past, W_A)
    kmean_pad = jnp.pad(kmean_all, ((0, 0), (0, 0), (0, LANES - nb_past), (0, 0)))

    zeros_p = (jnp.zeros((Bp, H_B, HEAD_DIM, HEAD_DIM), F32), jnp.zeros((Bp, H_B, HEAD_DIM), F32),
               jnp.zeros((Bp, H_B), F32), jnp.zeros((Bp, CONV_W - 1, 2 * W_B), F32),
               jnp.zeros((Bp, H_C, HEAD_DIM, HEAD_DIM), F32))

    xp_f = x_prompt.reshape(Bp * Tp, D)
    xs_f = x_sample.reshape(Bs * Ts, D)
    xp_b = xp_f.astype(BF16)
    xs_b = xs_f.astype(BF16)
    outs = [[] for _ in range(14)]
    mlstm_chunk_p = math.gcd(Tp, 256)
    for l in range(depth):
        wts = (w_main[l], w_gate[l], bgate[l], conv_w[l], conv_b[l][None, :], gn_b[l][None, :], gn_c[l][None, :],
               lb_all[l][None, :], w_out_b[l], ln1_g[l][None, :], ln1_b[l][None, :], w_up_b[l], w_down_b[l],
               ln2_g[l][None, :], ln2_b[l][None, :], alpha)

        attn_p = lambda proj: _moba_prompt(proj, bias_tab, Bp, Tp)
        xp_f, xp_b, proj_p, (Cp, nP, mP, SP) = _layer(xp_f, xp_b, Bp, Tp, wts, zeros_p, attn_p,
                                                      mlstm_chunk_p, mlstm_chunk_p)

        def attn_s(proj, l=l):
            sel = _sample_select(proj, kmean_pad[l], Bs, Ts, nb_past)
            return _moba_sample(sel, page_table, rel_bias, proj, cache_k, cache_v, Bs, Ts, past, l)

        st_s = (state_b_C[l], state_b_n[l], state_b_m[l], state_b_conv[l], state_c_S[l])
        xs_f, xs_b, proj_s, (Cs, nS, mS, SS) = _layer(xs_f, xs_b, Bs, Ts, wts, st_s, attn_s, Ts, Ts)

        kv_p = proj_p[:, W_A:3 * W_A].reshape(Bp, Tp, 2, H_A, HEAD_DIM)
        kv_s = proj_s[:, W_A:3 * W_A].reshape(Bs, Ts, 2, H_A, HEAD_DIM)
        conv_p = proj_p[:, 3 * W_A:3 * W_A + 2 * W_B].reshape(Bp, Tp, 2 * W_B)[:, Tp - (CONV_W - 1):]
        conv_s = proj_s[:, 3 * W_A:3 * W_A + 2 * W_B].reshape(Bs, Ts, 2 * W_B)[:, Ts - (CONV_W - 1):]
        for lst, val in zip(outs, (kv_p[:, :, 0], kv_p[:, :, 1], kv_s[:, :, 0], kv_s[:, :, 1],
                                   Cp, nP, mP, conv_p, Cs, nS, mS, conv_s, SP, SS)):
            lst.append(val)

    return (xp_f.reshape(Bp, Tp, D), xs_f.reshape(Bs, Ts, D)) + tuple(jnp.stack(o) for o in outs)
```

```python
import functools
import math

import numpy as np
import jax
import jax.numpy as jnp
from jax import lax
from jax.experimental import pallas as pl
from jax.experimental.pallas import tpu as pltpu

F32 = jnp.float32
BF16 = jnp.bfloat16

HEAD_DIM = 128
H_A, H_B, H_C = 8, 4, 4
W_A, W_B, W_C = H_A * HEAD_DIM, H_B * HEAD_DIM, H_C * HEAD_DIM
MOBA_BLOCK = 256
MOBA_TOPK = 3
NUM_BUCKETS = 32
MAX_DISTANCE = 2048
CONV_W = 4
EPS = 1e-5
GATE_MASK = -1e30
NEG_BIG = -1e30
PAGE_SIZE = 128
LANES = 128
SUBLANES = 8
HGRN_SUB = 32
MOBA_GROUP = 4
VT_EXTRA = 16
VMEM_LIMIT = 56 * 1024 * 1024

N_MAIN = 3 * W_A + 4 * W_B + 4 * W_C
GATE_COL0 = 3 * W_A + 3 * W_B


def _t5_thresholds():
    max_exact = NUM_BUCKETS // 2
    n = np.arange(1, 4 * MAX_DISTANCE, dtype=np.float32)
    large = max_exact + (np.log(n / np.float32(max_exact)) / np.float32(math.log(MAX_DISTANCE / max_exact))
                         * np.float32(NUM_BUCKETS - max_exact)).astype(np.int32)
    large = np.minimum(large, NUM_BUCKETS - 1)
    thr = []
    for b in range(max_exact + 1, NUM_BUCKETS):
        thr.append(int(np.argmax(large >= b)) + 1)
    return tuple(thr)


T5_THRESHOLDS = _t5_thresholds()


def _cparams(sem):
    return pltpu.CompilerParams(dimension_semantics=sem, vmem_limit_bytes=VMEM_LIMIT)


def _dot(a, b):
    return jnp.dot(a, b, preferred_element_type=F32)


def _dot_nt(a, b):
    return lax.dot_general(a, b, (((1,), (1,)), ((), ())), preferred_element_type=F32)


def _dot_tn(a, b):
    return lax.dot_general(a, b, (((0,), (0,)), ((), ())), preferred_element_type=F32)


def _dot_hi(a, b):
    return jnp.dot(a, b, precision=lax.Precision.HIGHEST, preferred_element_type=F32)


def _dot_nt_hi(a, b):
    return lax.dot_general(a, b, (((1,), (1,)), ((), ())), precision=lax.Precision.HIGHEST,
                           preferred_element_type=F32)


def _sigmoid(x):
    return 1.0 / (1.0 + jnp.exp(-x))


def _layer_norm(z, g, b):
    mu = jnp.mean(z, axis=-1, keepdims=True)
    zc = z - mu
    var = jnp.mean(zc * zc, axis=-1, keepdims=True)
    return zc * lax.rsqrt(var + EPS) * g + b


def _matmul_kernel(x_ref, w_ref, o_ref):
    o_ref[...] = _dot(x_ref[...], w_ref[...])


def _matmul(x, w, tm, tn):
    M, K = x.shape
    N = w.shape[1]
    return pl.pallas_call(
        _matmul_kernel,
        grid=(M // tm, N // tn),
        in_specs=[pl.BlockSpec((tm, K), lambda i, j: (i, 0)),
                  pl.BlockSpec((K, tn), lambda i, j: (0, j))],
        out_specs=pl.BlockSpec((tm, tn), lambda i, j: (i, j)),
        out_shape=jax.ShapeDtypeStruct((M, N), F32),
        compiler_params=_cparams(("arbitrary", "arbitrary")),
        name="in_proj",
    )(x, w)


def _t5_bias_from_dist(dist, rb_ref, h):
    n = jnp.maximum(dist, 0)
    large = jnp.full(n.shape, NUM_BUCKETS // 2, jnp.int32)
    for thr in T5_THRESHOLDS:
        large = large + (n >= thr).astype(jnp.int32)
    bucket = jnp.where(n < NUM_BUCKETS // 2, n, large)
    val = jnp.zeros(n.shape, F32)
    for b in range(NUM_BUCKETS):
        val = jnp.where(bucket == b, rb_ref[b, h], val)
    return val


def _bias_table_kernel(rb_ref, o_ref):
    d = pl.program_id(0)
    h = pl.program_id(1)
    row = lax.broadcasted_iota(jnp.int32, (MOBA_BLOCK, MOBA_BLOCK), 0)
    col = lax.broadcasted_iota(jnp.int32, (MOBA_BLOCK, MOBA_BLOCK), 1)
    dist = d * MOBA_BLOCK + col - row
    o_ref[...] = _t5_bias_from_dist(dist, rb_ref, h)


def _bias_table(rel_bias, nd):
    return pl.pallas_call(
        _bias_table_kernel,
        grid=(nd, H_A),
        in_specs=[pl.BlockSpec(memory_space=pltpu.SMEM)],
        out_specs=pl.BlockSpec((None, None, MOBA_BLOCK, MOBA_BLOCK), lambda d, h: (d, h, 0, 0)),
        out_shape=jax.ShapeDtypeStruct((nd, H_A, MOBA_BLOCK, MOBA_BLOCK), F32),
        compiler_params=_cparams(("arbitrary", "arbitrary")),
        name="t5_bias_table",
    )(rel_bias)


def _topk_select(gate, n_valid, n_cand):
    lane = lax.broadcasted_iota(jnp.int32, gate.shape, 1)
    gm = jnp.where(lane < n_valid, gate, GATE_MASK)
    rank = jnp.zeros(gate.shape, jnp.int32)
    for c in range(n_cand):
        gc = gm[:, c:c + 1]
        ahead = (gc > gm) | ((gc == gm) & (c < lane))
        rank = rank + ahead.astype(jnp.int32)
    return rank, lane


def _moba_prompt_kernel(q_ref, k_ref, v_ref, bias_ref, o_ref, ka_s, vt_s, qa_s, kmean_s, *, nb):
    i = pl.program_id(2)
    blk = MOBA_BLOCK
    nbp = -(-nb // SUBLANES) * SUBLANES

    @pl.when(i == 0)
    def _():
        kmean_s[...] = jnp.zeros(kmean_s.shape, F32)
        lane = lax.broadcasted_iota(jnp.int32, (blk, HEAD_DIM), 1)
        for n in range(nb):
            kf = k_ref[n * blk:(n + 1) * blk, :]
            ka_s[n, :, 0:HEAD_DIM] = kf.astype(BF16)
            ka_s[n, :, HEAD_DIM:2 * HEAD_DIM] = jnp.where(lane == n, NEG_BIG, 0.0).astype(BF16)
            ones_row = jnp.where(lax.broadcasted_iota(jnp.int32, (VT_EXTRA, blk), 0) == 0, 1.0, 0.0)
            vt_s[n] = jnp.concatenate([v_ref[n * blk:(n + 1) * blk, :].T, ones_row], axis=0).astype(BF16)
            kmean_s[n:n + 1, :] = jnp.mean(kf, axis=0, keepdims=True)
        km = kmean_s[...]
        sub = lax.broadcasted_iota(jnp.int32, (nbp, blk), 0)
        pad = jnp.zeros((HEAD_DIM - nbp, blk), F32)
        for t in range(nb):
            q = q_ref[t * blk:(t + 1) * blk, :]
            if t > MOBA_TOPK:
                gm = jnp.where(sub < t, _dot_nt_hi(km, q), GATE_MASK)
                rank = jnp.zeros((nbp, blk), jnp.int32)
                for c in range(t):
                    gc = gm[c:c + 1, :]
                    rank = rank + ((gc > gm) | ((gc == gm) & (c < sub))).astype(jnp.int32)
                notsel = jnp.where((sub < t) & (rank < MOBA_TOPK), 0.0, 1.0)
            else:
                notsel = jnp.where(sub < t, 0.0, 1.0)
            qt = (q * (HEAD_DIM ** -0.5)).T
            qa_s[t] = jnp.concatenate([qt, notsel, pad], axis=0).astype(BF16)

    qaug = qa_s[i]
    rowk = lax.broadcasted_iota(jnp.int32, (blk, blk), 0)
    colq = lax.broadcasted_iota(jnp.int32, (blk, blk), 1)

    def tile(n_blocks):
        sd = _dot(ka_s[i, :, 0:HEAD_DIM], qaug[0:HEAD_DIM, :]) + bias_ref[0]
        sd = jnp.where(colq >= rowk, sd, NEG_BIG)
        m = jnp.max(sd, axis=0, keepdims=True)
        acc = _dot(vt_s[i], jnp.exp(sd - m).astype(BF16))
        for pp in range(n_blocks // 2):
            ja, jb = 2 * pp, 2 * pp + 1
            sa = _dot(ka_s[ja], qaug) + bias_ref[jnp.maximum(i - ja, 0)]
            sb = _dot(ka_s[jb], qaug) + bias_ref[jnp.maximum(i - jb, 0)]
            m_new = jnp.maximum(m, jnp.maximum(jnp.max(sa, axis=0, keepdims=True),
                                               jnp.max(sb, axis=0, keepdims=True)))
            pa = jnp.exp(sa - m_new).astype(BF16)
            pb = jnp.exp(sb - m_new).astype(BF16)
            acc = jnp.exp(m - m_new) * acc + _dot(vt_s[ja], pa) + _dot(vt_s[jb], pb)
            m = m_new
        o_ref[...] = (acc[0:HEAD_DIM] / acc[HEAD_DIM:HEAD_DIM + 1]).T.astype(o_ref.dtype)

    n_groups = -(-nb // MOBA_GROUP)
    for grp in range(n_groups):
        pl.when(i // MOBA_GROUP == grp)(functools.partial(tile, min((grp + 1) * MOBA_GROUP, nb)))


def _moba_prompt(proj, bias_tab, B, T):
    nb = T // MOBA_BLOCK
    assert T % (2 * MOBA_BLOCK) == 0, "key blocks are processed in pairs"
    return pl.pallas_call(
        functools.partial(_moba_prompt_kernel, nb=nb),
        grid=(B, H_A, nb),
        in_specs=[pl.BlockSpec((T, HEAD_DIM), lambda b, h, i: (b, h)),
                  pl.BlockSpec((T, HEAD_DIM), lambda b, h, i: (b, H_A + h)),
                  pl.BlockSpec((T, HEAD_DIM), lambda b, h, i: (b, 2 * H_A + h)),
                  pl.BlockSpec((nb, None, MOBA_BLOCK, MOBA_BLOCK), lambda b, h, i: (0, h, 0, 0))],
        out_specs=pl.BlockSpec((MOBA_BLOCK, HEAD_DIM), lambda b, h, i: (b * nb + i, h)),
        out_shape=jax.ShapeDtypeStruct((B * T, W_A), BF16),
        scratch_shapes=[pltpu.VMEM((nb, MOBA_BLOCK, 2 * HEAD_DIM), BF16),
                        pltpu.VMEM((nb, HEAD_DIM + VT_EXTRA, MOBA_BLOCK), BF16),
                        pltpu.VMEM((nb, 2 * HEAD_DIM, MOBA_BLOCK), BF16),
                        pltpu.VMEM((-(-nb // SUBLANES) * SUBLANES, HEAD_DIM), F32)],
        compiler_params=_cparams(("arbitrary", "arbitrary", "arbitrary")),
        name="moba_prompt",
    )(proj, proj, proj, bias_tab)


def _mlstm_kernel(q_ref, k_ref, qp_ref, kp_ref, v_ref, og_ref, g_ref, conv0_ref, cw_ref, cb_ref, bg_ref,
                  gn_ref, c0_ref, n0_ref, m0_ref,
                  y_ref, cout_ref, nout_ref, mout_ref,
                  c_s, n_s, m_s, ext_s, *, L):
    c = pl.program_id(1)
    last = pl.num_programs(1) - 1

    @pl.when(c == 0)
    def _():
        c_s[...] = c0_ref[...]
        n_s[...] = n0_ref[...]
        m_s[...] = m0_ref[...]

    def conv_silu(u_ref, up_ref, col0):
        u = u_ref[...]
        tail = jnp.where(c == 0, conv0_ref[:, col0:col0 + W_B], up_ref[L - SUBLANES:L, :])
        ext_s[0:SUBLANES, :] = tail
        ext_s[SUBLANES:SUBLANES + L, :] = u
        acc = u * cw_ref[CONV_W - 1:CONV_W, col0:col0 + W_B] + cb_ref[:, col0:col0 + W_B]
        for j in range(1, CONV_W):
            xj = ext_s[SUBLANES - j:SUBLANES - j + L, :]
            acc = acc + xj * cw_ref[CONV_W - 1 - j:CONV_W - j, col0:col0 + W_B]
        return acc * _sigmoid(acc)

    qc = conv_silu(q_ref, qp_ref, 0)
    kc = conv_silu(k_ref, kp_ref, W_B) * (HEAD_DIM ** -0.5)
    v = v_ref[...]
    og = og_ref[...]

    g = g_ref[...] + bg_ref[...]
    lf = jnp.minimum(g, 0.0) - jnp.log(1.0 + jnp.exp(-jnp.abs(g)))
    row = lax.broadcasted_iota(jnp.int32, (L, L), 0)
    col = lax.broadcasted_iota(jnp.int32, (L, L), 1)
    causal = row >= col
    fcum = _dot_hi(causal.astype(F32), lf)

    for h in range(H_B):
        hs = slice(h * HEAD_DIM, (h + 1) * HEAD_DIM)
        fcol = fcum[:, H_B + h:H_B + h + 1]
        rcol = g[:, h:h + 1] - fcol
        rrow = jnp.sum(jnp.where(row == col, rcol, 0.0), axis=0, keepdims=True)
        dm = jnp.where(causal, fcol + rrow, -jnp.inf)
        mprev = m_s[h:h + 1, 0:1]
        gcol = fcol + mprev
        mt = jnp.maximum(gcol, jnp.max(dm, axis=1, keepdims=True))
        w = jnp.exp(dm - mt)
        wg = jnp.exp(gcol - mt)
        qh = qc[:, hs]
        kh = kc[:, hs]
        vh = v[:, hs]
        qhb = qh.astype(BF16)
        s = _dot_nt(qhb, kh.astype(BF16)) * w
        num = _dot(s.astype(BF16), vh.astype(BF16)) + wg * _dot(qhb, c_s[h].astype(BF16))
        den = jnp.sum(s, axis=1, keepdims=True) + wg * jnp.sum(qh * n_s[h:h + 1, :], axis=1, keepdims=True)
        hh = num / jnp.maximum(jnp.abs(den), jnp.exp(-mt))
        ml = mt[L - 1:L, :]
        wl = jnp.exp(fcol[L - 1:L, :] + rcol - ml)
        gl = jnp.exp(gcol[L - 1:L, :] - ml)
        kw = kh * wl
        c_s[h] = gl * c_s[h] + _dot_tn(kw.astype(BF16), vh.astype(BF16))
        n_s[h:h + 1, :] = gl * n_s[h:h + 1, :] + jnp.sum(kw, axis=0, keepdims=True)
        m_s[h:h + 1, :] = jnp.broadcast_to(ml, (1, LANES))
        hc = hh - jnp.mean(hh, axis=1, keepdims=True)
        yn = hc * lax.rsqrt(jnp.mean(hc * hc, axis=1, keepdims=True) + EPS) * gn_ref[:, hs]
        y_ref[:, hs] = (_sigmoid(og[:, hs]) * yn).astype(y_ref.dtype)

    @pl.when(c == last)
    def _():
        cout_ref[...] = c_s[...]
        nout_ref[...] = n_s[...]
        mout_ref[...] = m_s[...]


def _mlstm(proj, gates, conv0, conv_w, conv_b, bgate, gn_b, c0, n0, m0, B, T, L, out_dtype):
    nc = T // L
    cb = W_B // W_B
    q_blk, k_blk, v_blk, o_blk = 3 * W_A // W_B, 3 * W_A // W_B + 1, 3 * W_A // W_B + 2, 3 * W_A // W_B + 3
    del cb

    def cur(colblk):
        return pl.BlockSpec((L, W_B), lambda b, c: (b * nc + c, colblk))

    def prev(colblk):
        return pl.BlockSpec((L, W_B), lambda b, c: (b * nc + jnp.maximum(c - 1, 0), colblk))

    full2 = lambda shape: pl.BlockSpec(shape, lambda b, c: (0, 0))
    per_b3 = lambda shape: pl.BlockSpec((None,) + shape, lambda b, c: (b, 0, 0))
    return pl.pallas_call(
        functools.partial(_mlstm_kernel, L=L),
        grid=(B, nc),
        in_specs=[cur(q_blk), cur(k_blk), prev(q_blk), prev(k_blk), cur(v_blk), cur(o_blk),
                  pl.BlockSpec((L, LANES), lambda b, c: (b * nc + c, 0)),
                  per_b3((SUBLANES, 2 * W_B)),
                  full2((CONV_W, 2 * W_B)), full2((1, 2 * W_B)), full2((1, LANES)), full2((1, W_B)),
                  pl.BlockSpec((None, H_B, HEAD_DIM, HEAD_DIM), lambda b, c: (b, 0, 0, 0)),
                  per_b3((SUBLANES, HEAD_DIM)), per_b3((SUBLANES, LANES))],
        out_specs=[pl.BlockSpec((L, W_B), lambda b, c: (b * nc + c, 0)),
                   pl.BlockSpec((None, H_B, HEAD_DIM, HEAD_DIM), lambda b, c: (b, 0, 0, 0)),
                   per_b3((SUBLANES, HEAD_DIM)), per_b3((SUBLANES, LANES))],
        out_shape=[jax.ShapeDtypeStruct((B * T, W_B), out_dtype),
                   jax.ShapeDtypeStruct((B, H_B, HEAD_DIM, HEAD_DIM), F32),
                   jax.ShapeDtypeStruct((B, SUBLANES, HEAD_DIM), F32),
                   jax.ShapeDtypeStruct((B, SUBLANES, LANES), F32)],
        scratch_shapes=[pltpu.VMEM((H_B, HEAD_DIM, HEAD_DIM), F32), pltpu.VMEM((SUBLANES, HEAD_DIM), F32),
                        pltpu.VMEM((SUBLANES, LANES), F32), pltpu.VMEM((L + SUBLANES, W_B), F32)],
        compiler_params=_cparams(("arbitrary", "arbitrary")),
        name="mlstm",
    )(proj, proj, proj, proj, proj, proj, gates, conv0, conv_w, conv_b, bgate, gn_b, c0, n0, m0)


def _hgrn_kernel(q_ref, f_ref, i_ref, g_ref, lb_ref, gn_ref, s0_ref, y_ref, sout_ref,
                 st_s, k_s, b_s, v_s, *, LC, LS):
    c = pl.program_id(1)
    last = pl.num_programs(1) - 1

    @pl.when(c == 0)
    def _():
        for h in range(H_C):
            st_s[h] = s0_ref[h].T

    lb = lb_ref[...]
    one_m_lb = 1.0 - lb
    row = lax.broadcasted_iota(jnp.int32, (LS, LS), 0)
    col = lax.broadcasted_iota(jnp.int32, (LS, LS), 1)
    tril = (row >= col).astype(F32)
    row8 = lax.broadcasted_iota(jnp.int32, (SUBLANES, W_C), 0)

    def sub(sc, carry):
        r = pl.multiple_of(sc * LS, LS)
        fc = f_ref[pl.ds(r, LS), :]
        qc = q_ref[pl.ds(r, LS), :]
        logf = jnp.log(lb + one_m_lb * _sigmoid(fc))
        kk = one_m_lb * _sigmoid(-fc)
        qq = qc * _sigmoid(qc)
        vv = i_ref[pl.ds(r, LS), :]
        b = _dot_hi(tril, logf)
        k_s[...] = kk
        b_s[...] = b
        v_s[...] = vv

        o_rows = []
        for rb in range(LS // SUBLANES):
            t0 = rb * SUBLANES
            qb_ = qq[t0:t0 + SUBLANES, :]
            bb_ = b[t0:t0 + SUBLANES, :]
            o_h = [jnp.zeros((SUBLANES, HEAD_DIM), F32) for _ in range(H_C)]
            for s in range(t0 + SUBLANES):
                d = bb_ - b_s[s:s + 1, :]
                if s >= t0:
                    d = jnp.where(row8 >= (s - t0), d, -jnp.inf)
                tmp = qb_ * k_s[s:s + 1, :] * jnp.exp(d)
                vs = v_s[s:s + 1, :]
                for h in range(H_C):
                    hs = slice(h * HEAD_DIM, (h + 1) * HEAD_DIM)
                    a = jnp.sum(tmp[:, hs], axis=1, keepdims=True)
                    o_h[h] = o_h[h] + a * vs[:, hs]
            o_rows.append(o_h)

        qe = qq * jnp.exp(b)
        bl = b[LS - 1:LS, :]
        ke = kk * jnp.exp(bl - b)
        dec = jnp.exp(bl)
        gg = g_ref[pl.ds(r, LS), :]
        for h in range(H_C):
            hs = slice(h * HEAD_DIM, (h + 1) * HEAD_DIM)
            st = st_s[h]
            o_inter = _dot_nt(qe[:, hs].astype(BF16), st.astype(BF16))
            o_intra = o_rows[0][h] if LS == SUBLANES else jnp.concatenate([o_rows[rb][h] for rb in range(LS // SUBLANES)], axis=0)
            o = o_inter + o_intra
            st_s[h] = dec[:, hs] * st + _dot_tn(vv[:, hs].astype(BF16), ke[:, hs].astype(BF16))
            yn = o * lax.rsqrt(jnp.mean(o * o, axis=1, keepdims=True) + EPS) * gn_ref[:, hs]
            gh = gg[:, hs]
            y_ref[pl.ds(r, LS), hs] = (yn * (gh * _sigmoid(gh))).astype(y_ref.dtype)
        return carry

    lax.fori_loop(0, LC // LS, sub, 0)

    @pl.when(c == last)
    def _():
        for h in range(H_C):
            sout_ref[h] = st_s[h].T


def _hgrn(proj, lb, gn_c, s0, B, T, LC, LS, out_dtype):
    nc = T // LC
    base = (3 * W_A + 4 * W_B) // W_C
    blk = lambda k: pl.BlockSpec((LC, W_C), lambda b, c: (b * nc + c, base + k))
    full2 = lambda shape: pl.BlockSpec(shape, lambda b, c: (0, 0))
    st_spec = pl.BlockSpec((None, H_C, HEAD_DIM, HEAD_DIM), lambda b, c: (b, 0, 0, 0))
    return pl.pallas_call(
        functools.partial(_hgrn_kernel, LC=LC, LS=LS),
        grid=(B, nc),
        in_specs=[blk(0), blk(1), blk(2), blk(3), full2((1, W_C)), full2((1, W_C)), st_spec],
        out_specs=[pl.BlockSpec((LC, W_C), lambda b, c: (b * nc + c, 0)), st_spec],
        out_shape=[jax.ShapeDtypeStruct((B * T, W_C), out_dtype),
                   jax.ShapeDtypeStruct((B, H_C, HEAD_DIM, HEAD_DIM), F32)],
        scratch_shapes=[pltpu.VMEM((H_C, HEAD_DIM, HEAD_DIM), F32)] + [pltpu.VMEM((LS, W_C), F32)] * 3,
        compiler_params=_cparams(("arbitrary", "arbitrary")),
        name="hgrn2",
    )(proj, proj, proj, proj, lb, gn_c, s0)


def _outproj_kernel(ya_ref, yb_ref, yc_ref, x_ref, w_ref, g_ref, b_ref, hf_ref, hb_ref, *, alpha):
    mix = (_dot(ya_ref[...].astype(BF16), w_ref[0:W_A, :])
           + _dot(yb_ref[...].astype(BF16), w_ref[W_A:W_A + W_B, :])
           + _dot(yc_ref[...].astype(BF16), w_ref[W_A + W_B:W_A + W_B + W_C, :]))
    h = _layer_norm(alpha * x_ref[...] + mix, g_ref[...], b_ref[...])
    hf_ref[...] = h
    hb_ref[...] = h.astype(BF16)


def _outproj(ya, yb, yc, x, w, g, b, alpha, tm):
    M, D = x.shape
    rows = lambda n: pl.BlockSpec((tm, n), lambda i: (i, 0))
    full = lambda shape: pl.BlockSpec(shape, lambda i: (0, 0))
    return pl.pallas_call(
        functools.partial(_outproj_kernel, alpha=alpha),
        grid=(M // tm,),
        in_specs=[rows(W_A), rows(W_B), rows(W_C), rows(D), full(w.shape), full((1, D)), full((1, D))],
        out_specs=[rows(D), rows(D)],
        out_shape=[jax.ShapeDtypeStruct((M, D), F32), jax.ShapeDtypeStruct((M, D), BF16)],
        compiler_params=_cparams(("arbitrary",)),
        name="out_proj_ln1",
    )(ya, yb, yc, x, w, g, b)


def _mlp_kernel(hb_ref, hf_ref, wu_ref, wd_ref, g_ref, b_ref, of_ref, ob_ref, acc_s, *, alpha):
    f = pl.program_id(1)
    @pl.when(f == 0)
    def _():
        acc_s[...] = jnp.zeros(acc_s.shape, F32)

    u = jnp.maximum(_dot(hb_ref[...], wu_ref[...]), 0.0)
    acc_s[...] += _dot((u * u).astype(BF16), wd_ref[...])

    @pl.when(f == pl.num_programs(1) - 1)
    def _():
        o = _layer_norm(alpha * hf_ref[...] + acc_s[...], g_ref[...], b_ref[...])
        of_ref[...] = o
        ob_ref[...] = o.astype(BF16)


def _mlp(hb, hf, wu, wd, g, b, alpha, tm, tf):
    M, D = hf.shape
    FF = wu.shape[1]
    rows = lambda: pl.BlockSpec((tm, D), lambda i, f: (i, 0))
    vec = lambda: pl.BlockSpec((1, D), lambda i, f: (0, 0))
    return pl.pallas_call(
        functools.partial(_mlp_kernel, alpha=alpha),
        grid=(M // tm, FF // tf),
        in_specs=[rows(), rows(), pl.BlockSpec((D, tf), lambda i, f: (0, f)),
                  pl.BlockSpec((tf, D), lambda i, f: (f, 0)), vec(), vec()],
        out_specs=[rows(), rows()],
        out_shape=[jax.ShapeDtypeStruct((M, D), F32), jax.ShapeDtypeStruct((M, D), BF16)],
        scratch_shapes=[pltpu.VMEM((tm, D), F32)],
        compiler_params=_cparams(("arbitrary", "arbitrary")),
        name="mlp_ln2",
    )(hb, hf, wu, wd, g, b)


PAGES_PER_STEP = 16


def _kmean_kernel(pt_ref, *refs):
    page_refs, o_ref = refs[:PAGES_PER_STEP], refs[PAGES_PER_STEP]
    per_blk = MOBA_BLOCK // PAGE_SIZE
    for u in range(PAGES_PER_STEP // per_blk):
        tot = page_refs[per_blk * u][...].sum(axis=0)
        for e in range(1, per_blk):
            tot = tot + page_refs[per_blk * u + e][...].sum(axis=0)
        o_ref[u] = tot * (1.0 / MOBA_BLOCK)


def _cache_kmean(cache_k, page_table):
    depth = cache_k.shape[0]
    B, n_pages = page_table.shape
    nb = n_pages * PAGE_SIZE // MOBA_BLOCK
    steps = n_pages // PAGES_PER_STEP
    blocks_per_step = PAGES_PER_STEP * PAGE_SIZE // MOBA_BLOCK

    def page_spec(u):
        return pl.BlockSpec((None, None, PAGE_SIZE, H_A, HEAD_DIM),
                            lambda l, b, g, pt: (l, pt[b, g * PAGES_PER_STEP + u], 0, 0, 0))

    return pl.pallas_call(
        _kmean_kernel,
        grid_spec=pltpu.PrefetchScalarGridSpec(
            num_scalar_prefetch=1,
            grid=(depth, B, steps),
            in_specs=[page_spec(u) for u in range(PAGES_PER_STEP)],
            out_specs=pl.BlockSpec((None, None, blocks_per_step, H_A, HEAD_DIM),
                                   lambda l, b, g, pt: (l, b, g, 0, 0)),
        ),
        out_shape=jax.ShapeDtypeStruct((depth, B, nb, H_A, HEAD_DIM), F32),
        compiler_params=_cparams(("arbitrary", "arbitrary", "arbitrary")),
        name="cache_kmean",
    )(page_table, *([cache_k] * PAGES_PER_STEP))


def _sample_select_kernel(q_ref, km_ref, o_ref, *, nb):
    out = jnp.zeros(o_ref.shape, jnp.int32)
    lane_o = lax.broadcasted_iota(jnp.int32, o_ref.shape, 1)
    for h in range(H_A):
        hs = slice(h * HEAD_DIM, (h + 1) * HEAD_DIM)
        gate = _dot_nt_hi(q_ref[:, hs], km_ref[:, hs])
        rank, lane = _topk_select(gate, nb, nb)
        for slot in range(MOBA_TOPK):
            pick = (lane < nb) & (rank == slot)
            idx = jnp.sum(jnp.where(pick, lane, 0), axis=1, keepdims=True)
            out = jnp.where(lane_o == h * 4 + slot, idx, out)
    o_ref[...] = out


def _sample_select(proj_s, kmean_pad, B, T, nb):
    return pl.pallas_call(
        functools.partial(_sample_select_kernel, nb=nb),
        grid=(B,),
        in_specs=[pl.BlockSpec((T, W_A), lambda b: (b, 0)),
                  pl.BlockSpec((None, LANES, W_A), lambda b: (b, 0, 0))],
        out_specs=pl.BlockSpec((T, LANES), lambda b: (b, 0)),
        out_shape=jax.ShapeDtypeStruct((B * T, LANES), jnp.int32),
        compiler_params=_cparams(("arbitrary",)),
        name="moba_sample_select",
    )(proj_s, kmean_pad)


def _moba_sample_kernel(sel_ref, pt_ref, rb_ref, q_ref, kn_ref, vn_ref, ck_ref, cv_ref, o_ref,
                        kbuf, vbuf, sem, *, T, past, layer):
    b = pl.program_id(0)
    h = pl.program_id(1)
    nh = pl.num_programs(1)
    step = b * nh + h
    n_steps = pl.num_programs(0) * nh
    per_blk = MOBA_BLOCK // PAGE_SIZE

    def copies(bb, hh, buf, qi, slot, e):
        blk = sel_ref[bb * T + qi, hh * 4 + slot]
        page = pt_ref[bb, blk * per_blk + e]
        idx = qi * MOBA_TOPK + slot
        dst = pl.ds(e * PAGE_SIZE, PAGE_SIZE)
        return (pltpu.make_async_copy(ck_ref.at[layer, page, :, hh, :], kbuf.at[buf, idx, dst, :], sem.at[buf, 0]),
                pltpu.make_async_copy(cv_ref.at[layer, page, :, hh, :], vbuf.at[buf, idx, dst, :], sem.at[buf, 1]))

    def for_all_copies(bb, hh, buf, fn):
        for qi in range(T):
            for slot in range(MOBA_TOPK):
                for e in range(per_blk):
                    for c in copies(bb, hh, buf, qi, slot, e):
                        fn(c)

    cur = step % 2

    @pl.when(step == 0)
    def _():
        for_all_copies(b, h, 0, lambda c: c.start())

    @pl.when(step + 1 < n_steps)
    def _():
        nxt = step + 1
        for_all_copies(nxt // nh, nxt % nh, 1 - cur, lambda c: c.start())

    q = q_ref[...]
    qb = q.astype(BF16)
    scale = HEAD_DIM ** -0.5
    rowT = lax.broadcasted_iota(jnp.int32, (T, T), 0)
    colT = lax.broadcasted_iota(jnp.int32, (T, T), 1)
    s_own = _dot_nt(qb, kn_ref[...].astype(BF16)) * scale + _t5_bias_from_dist(rowT - colT, rb_ref, h)
    s_own = jnp.where(rowT >= colT, s_own, NEG_BIG)

    for_all_copies(b, h, cur, lambda c: c.wait())

    rowB = lax.broadcasted_iota(jnp.int32, (T, MOBA_BLOCK), 0)
    colB = lax.broadcasted_iota(jnp.int32, (T, MOBA_BLOCK), 1)
    row1 = lax.broadcasted_iota(jnp.int32, (T, 1), 0)
    s_slot = []
    for slot in range(MOBA_TOPK):
        s = jnp.zeros((T, MOBA_BLOCK), F32)
        blk_col = jnp.zeros((T, 1), jnp.int32)
        for qi in range(T):
            sq = _dot_nt(qb, kbuf[cur, qi * MOBA_TOPK + slot].astype(BF16))
            s = jnp.where(rowB == qi, sq, s)
            blk_col = jnp.where(row1 == qi, sel_ref[b * T + qi, h * 4 + slot], blk_col)
        dist = past + rowB - (blk_col * MOBA_BLOCK + colB)
        s_slot.append(s * scale + _t5_bias_from_dist(dist, rb_ref, h))

    m = jnp.max(s_own, axis=1, keepdims=True)
    for s in s_slot:
        m = jnp.maximum(m, jnp.max(s, axis=1, keepdims=True))
    p_own = jnp.exp(s_own - m)
    l = jnp.sum(p_own, axis=1, keepdims=True)
    acc = _dot(p_own.astype(BF16), vn_ref[...].astype(BF16))
    for slot in range(MOBA_TOPK):
        p = jnp.exp(s_slot[slot] - m)
        l = l + jnp.sum(p, axis=1, keepdims=True)
        for qi in range(T):
            pq = jnp.where(rowB == qi, p, 0.0).astype(BF16)
            acc = acc + _dot(pq, vbuf[cur, qi * MOBA_TOPK + slot].astype(BF16))
    o_ref[...] = acc / l


def _moba_sample(sel, page_table, rel_bias, proj_s, cache_k, cache_v, B, T, past, layer):
    n_slots = T * MOBA_TOPK
    return pl.pallas_call(
        functools.partial(_moba_sample_kernel, T=T, past=past, layer=layer),
        grid_spec=pltpu.PrefetchScalarGridSpec(
            num_scalar_prefetch=3,
            grid=(B, H_A),
            in_specs=[pl.BlockSpec((T, HEAD_DIM), lambda b, h, *_: (b, h)),
                      pl.BlockSpec((T, HEAD_DIM), lambda b, h, *_: (b, H_A + h)),
                      pl.BlockSpec((T, HEAD_DIM), lambda b, h, *_: (b, 2 * H_A + h)),
                      pl.BlockSpec(memory_space=pl.ANY),
                      pl.BlockSpec(memory_space=pl.ANY)],
            out_specs=pl.BlockSpec((T, HEAD_DIM), lambda b, h, *_: (b, h)),
            scratch_shapes=[pltpu.VMEM((2, n_slots, MOBA_BLOCK, HEAD_DIM), F32),
                            pltpu.VMEM((2, n_slots, MOBA_BLOCK, HEAD_DIM), F32),
                            pltpu.SemaphoreType.DMA((2, 2))],
        ),
        out_shape=jax.ShapeDtypeStruct((B * T, W_A), F32),
        compiler_params=_cparams(("arbitrary", "arbitrary")),
        name="moba_sample",
    )(sel, page_table, rel_bias, proj_s, proj_s, proj_s, cache_k, cache_v)


def _pad_rows(a, rows):
    return jnp.pad(a, ((0, 0), (0, rows - a.shape[1])) + ((0, 0),) * (a.ndim - 2))


def _mixer_states_in(c0, n0, m0, conv0):
    B = c0.shape[0]
    n0p = _pad_rows(n0, SUBLANES)
    m0p = _pad_rows(jnp.broadcast_to(m0[:, :, None], (B, H_B, LANES)), SUBLANES)
    conv0p = jnp.pad(conv0, ((0, 0), (SUBLANES - (CONV_W - 1), 0), (0, 0)))
    return c0, n0p, m0p, conv0p


def _tile(m, pref):
    return pref if m % pref == 0 else m


def _layer(x_f32, x_bf16, B, T, wts, states, attn_fn, mlstm_chunk, hgrn_chunk):
    (w_main, w_gate, bgate, conv_w, conv_b, gn_b, gn_c, lb, w_out, ln1_g, ln1_b, w_up, w_down,
     ln2_g, ln2_b, alpha) = wts
    c0, n0, m0, conv0, s0 = states
    M = B * T
    tm = _tile(M, 1024)
    proj = _matmul(x_bf16, w_main, tm, 512)
    gates = _matmul(x_bf16, w_gate, tm, LANES)
    ya = attn_fn(proj)
    c0, n0p, m0p, conv0p = _mixer_states_in(c0, n0, m0, conv0)
    y_dtype = BF16 if T % 16 == 0 else F32
    yb, c_new, n_new, m_new = _mlstm(proj, gates, conv0p, conv_w, conv_b, bgate, gn_b, c0, n0p, m0p, B, T,
                                     mlstm_chunk, y_dtype)
    yc, s_new = _hgrn(proj, lb, gn_c, s0, B, T, hgrn_chunk, min(HGRN_SUB, hgrn_chunk), y_dtype)
    tm2 = _tile(M, 512)
    hf, hb = _outproj(ya, yb, yc, x_f32, w_out, ln1_g, ln1_b, alpha, tm2)
    of, ob = _mlp(hb, hf, w_up, w_down, ln2_g, ln2_b, alpha, tm2, 1024)
    return of, ob, proj, (c_new, n_new[:, :H_B, :], m_new[:, :H_B, 0], s_new)


def kernel(x_prompt, x_sample, cache_k, cache_v, page_table, state_b_C, state_b_n, state_b_m, state_b_conv,
           state_c_S, w_in, b_gate, conv_w, conv_b, gn_b, gn_c, lower_bounds, rel_bias, w_out, ln1_g, ln1_b,
           w_up, w_down, ln2_g, ln2_b):
    depth = w_in.shape[0]
    Bp, Tp, D = x_prompt.shape
    Bs, Ts, _ = x_sample.shape
    n_pages = page_table.shape[1]
    past = n_pages * PAGE_SIZE
    alpha = (2 * depth) ** 0.25

    sm = jax.nn.softmax(lower_bounds.astype(F32), axis=0)
    lb_all = jnp.cumsum(sm, axis=0) - sm[0]

    w_main = jnp.concatenate([w_in[:, :, :GATE_COL0], w_in[:, :, GATE_COL0 + 2 * H_B:]], axis=-1).astype(BF16)
    w_gate = jnp.pad(w_in[:, :, GATE_COL0:GATE_COL0 + 2 * H_B], ((0, 0), (0, 0), (0, LANES - 2 * H_B))).astype(BF16)
    bgate = jnp.pad(b_gate, ((0, 0), (0, LANES - 2 * H_B)))[:, None, :]
    w_out_b = w_out.astype(BF16)
    w_up_b = w_up.astype(BF16)
    w_down_b = w_down.astype(BF16)

    bias_tab = _bias_table(rel_bias, Tp // MOBA_BLOCK)
    nb_past = past // MOBA_BLOCK
    kmean_all = _cache_kmean(cache_k, page_table).reshape(depth, Bs, nb_past, W_A)
    kmean_pad = jnp.pad(kmean_all, ((0, 0), (0, 0), (0, LANES - nb_past), (0, 0)))

    zeros_p = (jnp.zeros((Bp, H_B, HEAD_DIM, HEAD_DIM), F32), jnp.zeros((Bp, H_B, HEAD_DIM), F32),
               jnp.zeros((Bp, H_B), F32), jnp.zeros((Bp, CONV_W - 1, 2 * W_B), F32),
               jnp.zeros((Bp, H_C, HEAD_DIM, HEAD_DIM), F32))

    xp_f = x_prompt.reshape(Bp * Tp, D)
    xs_f = x_sample.reshape(Bs * Ts, D)
    xp_b = xp_f.astype(BF16)
    xs_b = xs_f.astype(BF16)
    outs = [[] for _ in range(14)]
    mlstm_chunk_p = math.gcd(Tp, 256)
    for l in range(depth):
        wts = (w_main[l], w_gate[l], bgate[l], conv_w[l], conv_b[l][None, :], gn_b[l][None, :], gn_c[l][None, :],
               lb_all[l][None, :], w_out_b[l], ln1_g[l][None, :], ln1_b[l][None, :], w_up_b[l], w_down_b[l],
               ln2_g[l][None, :], ln2_b[l][None, :], alpha)

        attn_p = lambda proj: _moba_prompt(proj, bias_tab, Bp, Tp)
        xp_f, xp_b, proj_p, (Cp, nP, mP, SP) = _layer(xp_f, xp_b, Bp, Tp, wts, zeros_p, attn_p,
                                                      mlstm_chunk_p, mlstm_chunk_p)

        def attn_s(proj, l=l):
            sel = _sample_select(proj, kmean_pad[l], Bs, Ts, nb_past)
            return _moba_sample(sel, page_table, rel_bias, proj, cache_k, cache_v, Bs, Ts, past, l)

        st_s = (state_b_C[l], state_b_n[l], state_b_m[l], state_b_conv[l], state_c_S[l])
        xs_f, xs_b, proj_s, (Cs, nS, mS, SS) = _layer(xs_f, xs_b, Bs, Ts, wts, st_s, attn_s, Ts, Ts)

        kv_p = proj_p[:, W_A:3 * W_A].reshape(Bp, Tp, 2, H_A, HEAD_DIM)
        kv_s = proj_s[:, W_A:3 * W_A].reshape(Bs, Ts, 2, H_A, HEAD_DIM)
        conv_p = proj_p[:, 3 * W_A:3 * W_A + 2 * W_B].reshape(Bp, Tp, 2 * W_B)[:, Tp - (CONV_W - 1):]
        conv_s = proj_s[:, 3 * W_A:3 * W_A + 2 * W_B].reshape(Bs, Ts, 2 * W_B)[:, Ts - (CONV_W - 1):]
        for lst, val in zip(outs, (kv_p[:, :, 0], kv_p[:, :, 1], kv_s[:, :, 0], kv_s[:, :, 1],
                                   Cp, nP, mP, conv_p, Cs, nS, mS, conv_s, SP, SS)):
            lst.append(val)

    return (xp_f.reshape(Bp, Tp, D), xs_f.reshape(Bs, Ts, D)) + tuple(jnp.stack(o) for o in outs)
```

```python
import functools
import math

import numpy as np
import jax
import jax.numpy as jnp
from jax import lax
from jax.experimental import pallas as pl
from jax.experimental.pallas import tpu as pltpu

F32 = jnp.float32
BF16 = jnp.bfloat16

HEAD_DIM = 128
H_A, H_B, H_C = 8, 4, 4
W_A, W_B, W_C = H_A * HEAD_DIM, H_B * HEAD_DIM, H_C * HEAD_DIM
MOBA_BLOCK = 256
MOBA_TOPK = 3
NUM_BUCKETS = 32
MAX_DISTANCE = 2048
CONV_W = 4
EPS = 1e-5
GATE_MASK = -1e30
NEG_BIG = -1e30
PAGE_SIZE = 128
LANES = 128
SUBLANES = 8
HGRN_SUB = 32
MOBA_GROUP = 4
QB = 2
VT_EXTRA = 16
VMEM_LIMIT = 56 * 1024 * 1024

N_MAIN = 3 * W_A + 4 * W_B + 4 * W_C
GATE_COL0 = 3 * W_A + 3 * W_B


def _t5_thresholds():
    max_exact = NUM_BUCKETS // 2
    n = np.arange(1, 4 * MAX_DISTANCE, dtype=np.float32)
    large = max_exact + (np.log(n / np.float32(max_exact)) / np.float32(math.log(MAX_DISTANCE / max_exact))
                         * np.float32(NUM_BUCKETS - max_exact)).astype(np.int32)
    large = np.minimum(large, NUM_BUCKETS - 1)
    thr = []
    for b in range(max_exact + 1, NUM_BUCKETS):
        thr.append(int(np.argmax(large >= b)) + 1)
    return tuple(thr)


T5_THRESHOLDS = _t5_thresholds()


def _cparams(sem):
    return pltpu.CompilerParams(dimension_semantics=sem, vmem_limit_bytes=VMEM_LIMIT)


def _dot(a, b):
    return jnp.dot(a, b, preferred_element_type=F32)


def _dot_nt(a, b):
    return lax.dot_general(a, b, (((1,), (1,)), ((), ())), preferred_element_type=F32)


def _dot_tn(a, b):
    return lax.dot_general(a, b, (((0,), (0,)), ((), ())), preferred_element_type=F32)


def _dot_hi(a, b):
    return jnp.dot(a, b, precision=lax.Precision.HIGHEST, preferred_element_type=F32)


def _dot_nt_hi(a, b):
    return lax.dot_general(a, b, (((1,), (1,)), ((), ())), precision=lax.Precision.HIGHEST,
                           preferred_element_type=F32)


def _sigmoid(x):
    return 1.0 / (1.0 + jnp.exp(-x))


def _layer_norm(z, g, b):
    mu = jnp.mean(z, axis=-1, keepdims=True)
    zc = z - mu
    var = jnp.mean(zc * zc, axis=-1, keepdims=True)
    return zc * lax.rsqrt(var + EPS) * g + b


def _matmul_kernel(x_ref, w_ref, o_ref):
    o_ref[...] = _dot(x_ref[...], w_ref[...])


def _matmul(x, w, layer, tm, tn):
    M, K = x.shape
    N = w.shape[2]
    return pl.pallas_call(
        _matmul_kernel,
        grid=(M // tm, N // tn),
        in_specs=[pl.BlockSpec((tm, K), lambda i, j: (i, 0)),
                  pl.BlockSpec((None, K, tn), lambda i, j: (layer, 0, j))],
        out_specs=pl.BlockSpec((tm, tn), lambda i, j: (i, j)),
        out_shape=jax.ShapeDtypeStruct((M, N), F32),
        compiler_params=_cparams(("arbitrary", "arbitrary")),
        name="in_proj",
    )(x, w)


def _t5_bias_from_dist(dist, rb_ref, h):
    n = jnp.maximum(dist, 0)
    large = jnp.full(n.shape, NUM_BUCKETS // 2, jnp.int32)
    for thr in T5_THRESHOLDS:
        large = large + (n >= thr).astype(jnp.int32)
    bucket = jnp.where(n < NUM_BUCKETS // 2, n, large)
    val = jnp.zeros(n.shape, F32)
    for b in range(NUM_BUCKETS):
        val = jnp.where(bucket == b, rb_ref[b, h], val)
    return val


def _bias_table_kernel(rb_ref, o_ref, *, nb):
    h = pl.program_id(0)
    d = nb - 1 - pl.program_id(1)
    d_const = -(-(T5_THRESHOLDS[-1] - 1) // MOBA_BLOCK) + 1

    @pl.when(d >= d_const)
    def _():
        o_ref[...] = jnp.full(o_ref.shape, rb_ref[NUM_BUCKETS - 1, h], F32)

    @pl.when(d < 0)
    def _():
        o_ref[...] = jnp.full(o_ref.shape, rb_ref[0, h], F32)

    @pl.when((d >= 0) & (d < d_const))
    def _():
        row = lax.broadcasted_iota(jnp.int32, (MOBA_BLOCK, MOBA_BLOCK), 0)
        col = lax.broadcasted_iota(jnp.int32, (MOBA_BLOCK, MOBA_BLOCK), 1)
        dist = d * MOBA_BLOCK + col - row
        o_ref[...] = _t5_bias_from_dist(dist, rb_ref, h)


def _bias_table(rel_bias, nb):
    ne = 2 * nb - 1
    return pl.pallas_call(
        functools.partial(_bias_table_kernel, nb=nb),
        grid=(H_A, ne),
        in_specs=[pl.BlockSpec(memory_space=pltpu.SMEM)],
        out_specs=pl.BlockSpec((None, MOBA_BLOCK, MOBA_BLOCK), lambda h, e: (h, e, 0)),
        out_shape=jax.ShapeDtypeStruct((H_A, ne * MOBA_BLOCK, MOBA_BLOCK), F32),
        compiler_params=_cparams(("arbitrary", "arbitrary")),
        name="t5_bias_table",
    )(rel_bias)


def _topk_select(gate, n_valid, n_cand):
    lane = lax.broadcasted_iota(jnp.int32, gate.shape, 1)
    gm = jnp.where(lane < n_valid, gate, GATE_MASK)
    rank = jnp.zeros(gate.shape, jnp.int32)
    for c in range(n_cand):
        gc = gm[:, c:c + 1]
        ahead = (gc > gm) | ((gc == gm) & (c < lane))
        rank = rank + ahead.astype(jnp.int32)
    return rank, lane


def _moba_prompt_kernel(q_ref, k_ref, v_ref, bias_ref, o_ref, kout_ref, vout_ref,
                        ka_s, vt_s, vtd_s, qa_s, kmean_s, sem, *, nb):
    b = pl.program_id(0)
    h = pl.program_id(1)
    i = pl.program_id(2)
    blk = MOBA_BLOCK
    G = MOBA_GROUP
    nbp = -(-nb // SUBLANES) * SUBLANES

    kv_copies = (pltpu.make_async_copy(k_ref, kout_ref.at[b, :, h, :], sem.at[0]),
                 pltpu.make_async_copy(v_ref, vout_ref.at[b, :, h, :], sem.at[1]))

    @pl.when(i == 0)
    def _():
        for c in kv_copies:
            c.start()
        kmean_s[...] = jnp.zeros(kmean_s.shape, F32)
        lane = lax.broadcasted_iota(jnp.int32, (blk, HEAD_DIM), 1)
        for n in range(nb):
            rows = slice((n % G) * blk, (n % G + 1) * blk)
            kf = k_ref[n * blk:(n + 1) * blk, :]
            ka_s[n // G, rows, 0:HEAD_DIM] = kf.astype(BF16)
            ka_s[n // G, rows, HEAD_DIM:2 * HEAD_DIM] = jnp.where(lane == n, NEG_BIG, 0.0).astype(BF16)
            ones_row = jnp.where(lax.broadcasted_iota(jnp.int32, (VT_EXTRA, blk), 0) == 0, 1.0, 0.0)
            vt = jnp.concatenate([v_ref[n * blk:(n + 1) * blk, :].T, ones_row], axis=0).astype(BF16)
            vt_s[n // G, :, rows] = vt
            vtd_s[n] = vt
            kmean_s[n:n + 1, :] = jnp.mean(kf, axis=0, keepdims=True)
        km = kmean_s[...]
        sub = lax.broadcasted_iota(jnp.int32, (nbp, blk), 0)
        pad = jnp.zeros((HEAD_DIM - nbp, blk), F32)
        for t in range(nb):
            q = q_ref[t * blk:(t + 1) * blk, :]
            if t > MOBA_TOPK:
                gm = jnp.where(sub < t, _dot_nt_hi(km, q), GATE_MASK)
                rank = jnp.zeros((nbp, blk), jnp.int32)
                for c in range(t):
                    gc = gm[c:c + 1, :]
                    rank = rank + ((gc > gm) | ((gc == gm) & (c < sub))).astype(jnp.int32)
                notsel = jnp.where((sub < t) & (rank < MOBA_TOPK), 0.0, 1.0)
            else:
                notsel = jnp.where(sub < t, 0.0, 1.0)
            qt = (q * (HEAD_DIM ** -0.5)).T
            qa_s[t // QB, :, (t % QB) * blk:(t % QB + 1) * blk] = (
                jnp.concatenate([qt, notsel, pad], axis=0).astype(BF16))

    qaug = qa_s[i]
    rowk = lax.broadcasted_iota(jnp.int32, (blk, blk), 0)
    colq = lax.broadcasted_iota(jnp.int32, (blk, blk), 1)

    def tile(n_grp):
        m_parts, acc_parts = [], []
        for u in range(QB):
            iu = i * QB + u
            kd = ka_s[iu // G, pl.ds(pl.multiple_of((iu % G) * blk, blk), blk), 0:HEAD_DIM]
            sd = _dot(kd, qaug[0:HEAD_DIM, u * blk:(u + 1) * blk]) + bias_ref[(nb - 1) * blk:nb * blk, :]
            sd = jnp.where(colq >= rowk, sd, NEG_BIG)
            mu = jnp.max(sd, axis=0, keepdims=True)
            m_parts.append(mu)
            acc_parts.append(_dot(vtd_s[iu], jnp.exp(sd - mu).astype(BF16)))
        m = jnp.concatenate(m_parts, axis=1)
        acc = jnp.concatenate(acc_parts, axis=1)
        for g in range(n_grp):
            bias = jnp.concatenate(
                [bias_ref[pl.ds(pl.multiple_of((nb - 1 - (i * QB + u) + G * g) * blk, blk), G * blk), :]
                 for u in range(QB)], axis=1)
            s = _dot(ka_s[g], qaug) + bias
            m_new = jnp.maximum(m, jnp.max(s, axis=0, keepdims=True))
            acc = jnp.exp(m - m_new) * acc + _dot(vt_s[g], jnp.exp(s - m_new).astype(BF16))
            m = m_new
        o_ref[...] = (acc[0:HEAD_DIM] / acc[HEAD_DIM:HEAD_DIM + 1]).T.astype(o_ref.dtype)

    steps_per_grp = G // QB
    for grp in range(nb // G):
        pl.when(i // steps_per_grp == grp)(functools.partial(tile, grp + 1))

    @pl.when(i == nb // QB - 1)
    def _():
        for c in kv_copies:
            c.wait()


def _moba_prompt(proj, bias_tab, B, T):
    nb = T // MOBA_BLOCK
    G = MOBA_GROUP
    assert nb % G == 0 and G % QB == 0 and nb >= 2 * G, "key blocks are processed in groups"
    nq = nb // QB
    kv_shape = jax.ShapeDtypeStruct((B, T, H_A, HEAD_DIM), F32)
    return pl.pallas_call(
        functools.partial(_moba_prompt_kernel, nb=nb),
        grid=(B, H_A, nq),
        in_specs=[pl.BlockSpec((T, HEAD_DIM), lambda b, h, i: (b, h)),
                  pl.BlockSpec((T, HEAD_DIM), lambda b, h, i: (b, H_A + h)),
                  pl.BlockSpec((T, HEAD_DIM), lambda b, h, i: (b, 2 * H_A + h)),
                  pl.BlockSpec((None, (2 * nb - 1) * MOBA_BLOCK, MOBA_BLOCK), lambda b, h, i: (h, 0, 0))],
        out_specs=[pl.BlockSpec((QB * MOBA_BLOCK, HEAD_DIM), lambda b, h, i: (b * nq + i, h)),
                   pl.BlockSpec(memory_space=pl.ANY), pl.BlockSpec(memory_space=pl.ANY)],
        out_shape=[jax.ShapeDtypeStruct((B * T, W_A), BF16), kv_shape, kv_shape],
        scratch_shapes=[pltpu.VMEM((nb // G, G * MOBA_BLOCK, 2 * HEAD_DIM), BF16),
                        pltpu.VMEM((nb // G, HEAD_DIM + VT_EXTRA, G * MOBA_BLOCK), BF16),
                        pltpu.VMEM((nb, HEAD_DIM + VT_EXTRA, MOBA_BLOCK), BF16),
                        pltpu.VMEM((nq, 2 * HEAD_DIM, QB * MOBA_BLOCK), BF16),
                        pltpu.VMEM((-(-nb // SUBLANES) * SUBLANES, HEAD_DIM), F32),
                        pltpu.SemaphoreType.DMA((2,))],
        compiler_params=_cparams(("arbitrary", "arbitrary", "arbitrary")),
        name="moba_prompt",
    )(proj, proj, proj, bias_tab)


def _mlstm_kernel(q_ref, k_ref, qp_ref, kp_ref, v_ref, og_ref, g_ref, conv0_ref, cw_ref, cb_ref, bg_ref,
                  gn_ref, c0_ref, n0_ref, m0_ref,
                  y_ref, cout_ref, nout_ref, mout_ref,
                  c_s, n_s, m_s, ext_s, *, L):
    c = pl.program_id(1)
    last = pl.num_programs(1) - 1

    @pl.when(c == 0)
    def _():
        c_s[...] = c0_ref[...]
        n_s[...] = n0_ref[...]
        m_s[...] = m0_ref[...]

    def conv_silu(u_ref, up_ref, col0):
        u = u_ref[...]
        tail = jnp.where(c == 0, conv0_ref[:, col0:col0 + W_B], up_ref[L - SUBLANES:L, :])
        ext_s[0:SUBLANES, :] = tail
        ext_s[SUBLANES:SUBLANES + L, :] = u
        acc = u * cw_ref[CONV_W - 1:CONV_W, col0:col0 + W_B] + cb_ref[:, col0:col0 + W_B]
        for j in range(1, CONV_W):
            xj = ext_s[SUBLANES - j:SUBLANES - j + L, :]
            acc = acc + xj * cw_ref[CONV_W - 1 - j:CONV_W - j, col0:col0 + W_B]
        return acc * _sigmoid(acc)

    qc = conv_silu(q_ref, qp_ref, 0)
    kc = conv_silu(k_ref, kp_ref, W_B) * (HEAD_DIM ** -0.5)
    v = v_ref[...]
    og = og_ref[...]

    g = g_ref[...] + bg_ref[...]
    lf = jnp.minimum(g, 0.0) - jnp.log(1.0 + jnp.exp(-jnp.abs(g)))
    row = lax.broadcasted_iota(jnp.int32, (L, L), 0)
    col = lax.broadcasted_iota(jnp.int32, (L, L), 1)
    causal = row >= col
    fcum = _dot_hi(causal.astype(F32), lf)

    for h in range(H_B):
        hs = slice(h * HEAD_DIM, (h + 1) * HEAD_DIM)
        fcol = fcum[:, H_B + h:H_B + h + 1]
        rcol = g[:, h:h + 1] - fcol
        rrow = jnp.sum(jnp.where(row == col, rcol, 0.0), axis=0, keepdims=True)
        dm = jnp.where(causal, fcol + rrow, -jnp.inf)
        mprev = m_s[h:h + 1, 0:1]
        gcol = fcol + mprev
        mt = jnp.maximum(gcol, jnp.max(dm, axis=1, keepdims=True))
        w = jnp.exp(dm - mt)
        wg = jnp.exp(gcol - mt)
        qh = qc[:, hs]
        kh = kc[:, hs]
        vh = v[:, hs]
        qhb = qh.astype(BF16)
        s = _dot_nt(qhb, kh.astype(BF16)) * w
        num = _dot(s.astype(BF16), vh.astype(BF16)) + wg * _dot(qhb, c_s[h].astype(BF16))
        den = jnp.sum(s, axis=1, keepdims=True) + wg * jnp.sum(qh * n_s[h:h + 1, :], axis=1, keepdims=True)
        hh = num / jnp.maximum(jnp.abs(den), jnp.exp(-mt))
        ml = mt[L - 1:L, :]
        wl = jnp.exp(fcol[L - 1:L, :] + rcol - ml)
        gl = jnp.exp(gcol[L - 1:L, :] - ml)
        kw = kh * wl
        c_s[h] = gl * c_s[h] + _dot_tn(kw.astype(BF16), vh.astype(BF16))
        n_s[h:h + 1, :] = gl * n_s[h:h + 1, :] + jnp.sum(kw, axis=0, keepdims=True)
        m_s[h:h + 1, :] = jnp.broadcast_to(ml, (1, LANES))
        hc = hh - jnp.mean(hh, axis=1, keepdims=True)
        yn = hc * lax.rsqrt(jnp.mean(hc * hc, axis=1, keepdims=True) + EPS) * gn_ref[:, hs]
        y_ref[:, hs] = (_sigmoid(og[:, hs]) * yn).astype(y_ref.dtype)

    @pl.when(c == last)
    def _():
        cout_ref[...] = c_s[...]
        nout_ref[...] = n_s[...]
        mout_ref[...] = m_s[...]


def _mlstm(proj, gates, conv0, conv_w, conv_b, bgate, gn_b, c0, n0, m0, B, T, L, out_dtype):
    nc = T // L
    cb = W_B // W_B
    q_blk, k_blk, v_blk, o_blk = 3 * W_A // W_B, 3 * W_A // W_B + 1, 3 * W_A // W_B + 2, 3 * W_A // W_B + 3
    del cb

    def cur(colblk):
        return pl.BlockSpec((L, W_B), lambda b, c: (b * nc + c, colblk))

    def prev(colblk):
        return pl.BlockSpec((L, W_B), lambda b, c: (b * nc + jnp.maximum(c - 1, 0), colblk))

    full2 = lambda shape: pl.BlockSpec(shape, lambda b, c: (0, 0))
    per_b3 = lambda shape: pl.BlockSpec((None,) + shape, lambda b, c: (b, 0, 0))
    return pl.pallas_call(
        functools.partial(_mlstm_kernel, L=L),
        grid=(B, nc),
        in_specs=[cur(q_blk), cur(k_blk), prev(q_blk), prev(k_blk), cur(v_blk), cur(o_blk),
                  pl.BlockSpec((L, LANES), lambda b, c: (b * nc + c, 0)),
                  per_b3((SUBLANES, 2 * W_B)),
                  full2((CONV_W, 2 * W_B)), full2((1, 2 * W_B)), full2((1, LANES)), full2((1, W_B)),
                  pl.BlockSpec((None, H_B, HEAD_DIM, HEAD_DIM), lambda b, c: (b, 0, 0, 0)),
                  per_b3((SUBLANES, HEAD_DIM)), per_b3((SUBLANES, LANES))],
        out_specs=[pl.BlockSpec((L, W_B), lambda b, c: (b * nc + c, 0)),
                   pl.BlockSpec((None, H_B, HEAD_DIM, HEAD_DIM), lambda b, c: (b, 0, 0, 0)),
                   per_b3((SUBLANES, HEAD_DIM)), per_b3((SUBLANES, LANES))],
        out_shape=[jax.ShapeDtypeStruct((B * T, W_B), out_dtype),
                   jax.ShapeDtypeStruct((B, H_B, HEAD_DIM, HEAD_DIM), F32),
                   jax.ShapeDtypeStruct((B, SUBLANES, HEAD_DIM), F32),
                   jax.ShapeDtypeStruct((B, SUBLANES, LANES), F32)],
        scratch_shapes=[pltpu.VMEM((H_B, HEAD_DIM, HEAD_DIM), F32), pltpu.VMEM((SUBLANES, HEAD_DIM), F32),
                        pltpu.VMEM((SUBLANES, LANES), F32), pltpu.VMEM((L + SUBLANES, W_B), F32)],
        compiler_params=_cparams(("arbitrary", "arbitrary")),
        name="mlstm",
    )(proj, proj, proj, proj, proj, proj, gates, conv0, conv_w, conv_b, bgate, gn_b, c0, n0, m0)


def _hgrn_kernel(q_ref, f_ref, i_ref, g_ref, lb_ref, gn_ref, s0_ref, y_ref, sout_ref,
                 st_s, k_s, b_s, v_s, *, LC, LS):
    c = pl.program_id(1)
    last = pl.num_programs(1) - 1

    @pl.when(c == 0)
    def _():
        for h in range(H_C):
            st_s[h] = s0_ref[h].T

    lb = lb_ref[...]
    one_m_lb = 1.0 - lb
    row = lax.broadcasted_iota(jnp.int32, (LS, LS), 0)
    col = lax.broadcasted_iota(jnp.int32, (LS, LS), 1)
    tril = (row >= col).astype(F32)
    row8 = lax.broadcasted_iota(jnp.int32, (SUBLANES, W_C), 0)

    def sub(sc, carry):
        r = pl.multiple_of(sc * LS, LS)
        fc = f_ref[pl.ds(r, LS), :]
        qc = q_ref[pl.ds(r, LS), :]
        logf = jnp.log(lb + one_m_lb * _sigmoid(fc))
        kk = one_m_lb * _sigmoid(-fc)
        qq = qc * _sigmoid(qc)
        vv = i_ref[pl.ds(r, LS), :]
        b = _dot_hi(tril, logf)
        k_s[...] = kk
        b_s[...] = b
        v_s[...] = vv

        o_rows = []
        for rb in range(LS // SUBLANES):
            t0 = rb * SUBLANES
            qb_ = qq[t0:t0 + SUBLANES, :]
            bb_ = b[t0:t0 + SUBLANES, :]
            o_h = [jnp.zeros((SUBLANES, HEAD_DIM), F32) for _ in range(H_C)]
            for s in range(t0 + SUBLANES):
                d = bb_ - b_s[s:s + 1, :]
                if s >= t0:
                    d = jnp.where(row8 >= (s - t0), d, -jnp.inf)
                tmp = qb_ * k_s[s:s + 1, :] * jnp.exp(d)
                vs = v_s[s:s + 1, :]
                for h in range(H_C):
                    hs = slice(h * HEAD_DIM, (h + 1) * HEAD_DIM)
                    a = jnp.sum(tmp[:, hs], axis=1, keepdims=True)
                    o_h[h] = o_h[h] + a * vs[:, hs]
            o_rows.append(o_h)

        qe = qq * jnp.exp(b)
        bl = b[LS - 1:LS, :]
        ke = kk * jnp.exp(bl - b)
        dec = jnp.exp(bl)
        gg = g_ref[pl.ds(r, LS), :]
        for h in range(H_C):
            hs = slice(h * HEAD_DIM, (h + 1) * HEAD_DIM)
            st = st_s[h]
            o_inter = _dot_nt(qe[:, hs].astype(BF16), st.astype(BF16))
            o_intra = o_rows[0][h] if LS == SUBLANES else jnp.concatenate([o_rows[rb][h] for rb in range(LS // SUBLANES)], axis=0)
            o = o_inter + o_intra
            st_s[h] = dec[:, hs] * st + _dot_tn(vv[:, hs].astype(BF16), ke[:, hs].astype(BF16))
            yn = o * lax.rsqrt(jnp.mean(o * o, axis=1, keepdims=True) + EPS) * gn_ref[:, hs]
            gh = gg[:, hs]
            y_ref[pl.ds(r, LS), hs] = (yn * (gh * _sigmoid(gh))).astype(y_ref.dtype)
        return carry

    lax.fori_loop(0, LC // LS, sub, 0)

    @pl.when(c == last)
    def _():
        for h in range(H_C):
            sout_ref[h] = st_s[h].T


def _hgrn(proj, lb, gn_c, s0, B, T, LC, LS, out_dtype):
    nc = T // LC
    base = (3 * W_A + 4 * W_B) // W_C
    blk = lambda k: pl.BlockSpec((LC, W_C), lambda b, c: (b * nc + c, base + k))
    full2 = lambda shape: pl.BlockSpec(shape, lambda b, c: (0, 0))
    st_spec = pl.BlockSpec((None, H_C, HEAD_DIM, HEAD_DIM), lambda b, c: (b, 0, 0, 0))
    return pl.pallas_call(
        functools.partial(_hgrn_kernel, LC=LC, LS=LS),
        grid=(B, nc),
        in_specs=[blk(0), blk(1), blk(2), blk(3), full2((1, W_C)), full2((1, W_C)), st_spec],
        out_specs=[pl.BlockSpec((LC, W_C), lambda b, c: (b * nc + c, 0)), st_spec],
        out_shape=[jax.ShapeDtypeStruct((B * T, W_C), out_dtype),
                   jax.ShapeDtypeStruct((B, H_C, HEAD_DIM, HEAD_DIM), F32)],
        scratch_shapes=[pltpu.VMEM((H_C, HEAD_DIM, HEAD_DIM), F32)] + [pltpu.VMEM((LS, W_C), F32)] * 3,
        compiler_params=_cparams(("arbitrary", "arbitrary")),
        name="hgrn2",
    )(proj, proj, proj, proj, lb, gn_c, s0)


def _outproj_kernel(ya_ref, yb_ref, yc_ref, x_ref, w_ref, g_ref, b_ref, hf_ref, hb_ref, *, alpha):
    mix = (_dot(ya_ref[...].astype(BF16), w_ref[0:W_A, :])
           + _dot(yb_ref[...].astype(BF16), w_ref[W_A:W_A + W_B, :])
           + _dot(yc_ref[...].astype(BF16), w_ref[W_A + W_B:W_A + W_B + W_C, :]))
    h = _layer_norm(alpha * x_ref[...] + mix, g_ref[...], b_ref[...])
    hf_ref[...] = h
    hb_ref[...] = h.astype(BF16)


def _outproj(ya, yb, yc, x, w, layer, g, b, alpha, tm):
    M, D = x.shape
    rows = lambda n: pl.BlockSpec((tm, n), lambda i: (i, 0))
    full = lambda shape: pl.BlockSpec(shape, lambda i: (0, 0))
    w_spec = pl.BlockSpec((None,) + w.shape[1:], lambda i: (layer, 0, 0))
    return pl.pallas_call(
        functools.partial(_outproj_kernel, alpha=alpha),
        grid=(M // tm,),
        in_specs=[rows(W_A), rows(W_B), rows(W_C), rows(D), w_spec, full((1, D)), full((1, D))],
        out_specs=[rows(D), rows(D)],
        out_shape=[jax.ShapeDtypeStruct((M, D), F32), jax.ShapeDtypeStruct((M, D), BF16)],
        compiler_params=_cparams(("arbitrary",)),
        name="out_proj_ln1",
    )(ya, yb, yc, x, w, g, b)


def _mlp_kernel(hb_ref, hf_ref, wu_ref, wd_ref, g_ref, b_ref, of_ref, ob_ref, acc_s, *, alpha):
    f = pl.program_id(1)
    @pl.when(f == 0)
    def _():
        acc_s[...] = jnp.zeros(acc_s.shape, F32)

    u = jnp.maximum(_dot(hb_ref[...], wu_ref[...]), 0.0)
    acc_s[...] += _dot((u * u).astype(BF16), wd_ref[...])

    @pl.when(f == pl.num_programs(1) - 1)
    def _():
        o = _layer_norm(alpha * hf_ref[...] + acc_s[...], g_ref[...], b_ref[...])
        of_ref[...] = o
        ob_ref[...] = o.astype(BF16)


def _mlp(hb, hf, wu, wd, layer, g, b, alpha, tm, tf):
    M, D = hf.shape
    FF = wu.shape[2]
    rows = lambda: pl.BlockSpec((tm, D), lambda i, f: (i, 0))
    vec = lambda: pl.BlockSpec((1, D), lambda i, f: (0, 0))
    return pl.pallas_call(
        functools.partial(_mlp_kernel, alpha=alpha),
        grid=(M // tm, FF // tf),
        in_specs=[rows(), rows(), pl.BlockSpec((None, D, tf), lambda i, f: (layer, 0, f)),
                  pl.BlockSpec((None, tf, D), lambda i, f: (layer, f, 0)), vec(), vec()],
        out_specs=[rows(), rows()],
        out_shape=[jax.ShapeDtypeStruct((M, D), F32), jax.ShapeDtypeStruct((M, D), BF16)],
        scratch_shapes=[pltpu.VMEM((tm, D), F32)],
        compiler_params=_cparams(("arbitrary", "arbitrary")),
        name="mlp_ln2",
    )(hb, hf, wu, wd, g, b)


PAGES_PER_STEP = 16


def _kmean_kernel(pt_ref, *refs):
    page_refs, o_ref = refs[:PAGES_PER_STEP], refs[PAGES_PER_STEP]
    per_blk = MOBA_BLOCK // PAGE_SIZE
    for u in range(PAGES_PER_STEP // per_blk):
        tot = page_refs[per_blk * u][...].sum(axis=0)
        for e in range(1, per_blk):
            tot = tot + page_refs[per_blk * u + e][...].sum(axis=0)
        o_ref[u] = tot * (1.0 / MOBA_BLOCK)


def _cache_kmean(cache_k, page_table):
    depth = cache_k.shape[0]
    B, n_pages = page_table.shape
    nb = n_pages * PAGE_SIZE // MOBA_BLOCK
    steps = n_pages // PAGES_PER_STEP
    blocks_per_step = PAGES_PER_STEP * PAGE_SIZE // MOBA_BLOCK

    def page_spec(u):
        return pl.BlockSpec((None, None, PAGE_SIZE, H_A, HEAD_DIM),
                            lambda l, b, g, pt: (l, pt[b, g * PAGES_PER_STEP + u], 0, 0, 0))

    return pl.pallas_call(
        _kmean_kernel,
        grid_spec=pltpu.PrefetchScalarGridSpec(
            num_scalar_prefetch=1,
            grid=(depth, B, steps),
            in_specs=[page_spec(u) for u in range(PAGES_PER_STEP)],
            out_specs=pl.BlockSpec((None, None, blocks_per_step, H_A, HEAD_DIM),
                                   lambda l, b, g, pt: (l, b, g, 0, 0)),
        ),
        out_shape=jax.ShapeDtypeStruct((depth, B, nb, H_A, HEAD_DIM), F32),
        compiler_params=_cparams(("arbitrary", "arbitrary", "arbitrary")),
        name="cache_kmean",
    )(page_table, *([cache_k] * PAGES_PER_STEP))


def _sample_select_kernel(q_ref, km_ref, o_ref, *, nb):
    out = jnp.zeros(o_ref.shape, jnp.int32)
    lane_o = lax.broadcasted_iota(jnp.int32, o_ref.shape, 1)
    for h in range(H_A):
        hs = slice(h * HEAD_DIM, (h + 1) * HEAD_DIM)
        gate = _dot_nt_hi(q_ref[:, hs], km_ref[:, hs])
        rank, lane = _topk_select(gate, nb, nb)
        for slot in range(MOBA_TOPK):
            pick = (lane < nb) & (rank == slot)
            idx = jnp.sum(jnp.where(pick, lane, 0), axis=1, keepdims=True)
            out = jnp.where(lane_o == h * 4 + slot, idx, out)
    o_ref[...] = out


def _sample_select(proj_s, kmean_pad, B, T, nb):
    return pl.pallas_call(
        functools.partial(_sample_select_kernel, nb=nb),
        grid=(B,),
        in_specs=[pl.BlockSpec((T, W_A), lambda b: (b, 0)),
                  pl.BlockSpec((None, LANES, W_A), lambda b: (b, 0, 0))],
        out_specs=pl.BlockSpec((T, LANES), lambda b: (b, 0)),
        out_shape=jax.ShapeDtypeStruct((B * T, LANES), jnp.int32),
        compiler_params=_cparams(("arbitrary",)),
        name="moba_sample_select",
    )(proj_s, kmean_pad)


def _moba_sample_kernel(sel_ref, pt_ref, rb_ref, q_ref, kn_ref, vn_ref, ck_ref, cv_ref, o_ref,
                        kbuf, vbuf, sem, *, T, past, layer):
    b = pl.program_id(0)
    h = pl.program_id(1)
    nh = pl.num_programs(1)
    step = b * nh + h
    n_steps = pl.num_programs(0) * nh
    per_blk = MOBA_BLOCK // PAGE_SIZE

    def copies(bb, hh, buf, qi, slot, e):
        blk = sel_ref[bb * T + qi, hh * 4 + slot]
        page = pt_ref[bb, blk * per_blk + e]
        idx = qi * MOBA_TOPK + slot
        dst = pl.ds(e * PAGE_SIZE, PAGE_SIZE)
        return (pltpu.make_async_copy(ck_ref.at[layer, page, :, hh, :], kbuf.at[buf, idx, dst, :], sem.at[buf, 0]),
                pltpu.make_async_copy(cv_ref.at[layer, page, :, hh, :], vbuf.at[buf, idx, dst, :], sem.at[buf, 1]))

    def for_all_copies(bb, hh, buf, fn):
        for qi in range(T):
            for slot in range(MOBA_TOPK):
                for e in range(per_blk):
                    for c in copies(bb, hh, buf, qi, slot, e):
                        fn(c)

    cur = step % 2

    @pl.when(step == 0)
    def _():
        for_all_copies(b, h, 0, lambda c: c.start())

    @pl.when(step + 1 < n_steps)
    def _():
        nxt = step + 1
        for_all_copies(nxt // nh, nxt % nh, 1 - cur, lambda c: c.start())

    q = q_ref[...]
    qb = q.astype(BF16)
    scale = HEAD_DIM ** -0.5
    rowT = lax.broadcasted_iota(jnp.int32, (T, T), 0)
    colT = lax.broadcasted_iota(jnp.int32, (T, T), 1)
    s_own = _dot_nt(qb, kn_ref[...].astype(BF16)) * scale + _t5_bias_from_dist(rowT - colT, rb_ref, h)
    s_own = jnp.where(rowT >= colT, s_own, NEG_BIG)

    for_all_copies(b, h, cur, lambda c: c.wait())

    rowB = lax.broadcasted_iota(jnp.int32, (T, MOBA_BLOCK), 0)
    colB = lax.broadcasted_iota(jnp.int32, (T, MOBA_BLOCK), 1)
    row1 = lax.broadcasted_iota(jnp.int32, (T, 1), 0)
    s_slot = []
    for slot in range(MOBA_TOPK):
        s = jnp.zeros((T, MOBA_BLOCK), F32)
        blk_col = jnp.zeros((T, 1), jnp.int32)
        for qi in range(T):
            sq = _dot_nt(qb, kbuf[cur, qi * MOBA_TOPK + slot].astype(BF16))
            s = jnp.where(rowB == qi, sq, s)
            blk_col = jnp.where(row1 == qi, sel_ref[b * T + qi, h * 4 + slot], blk_col)
        dist = past + rowB - (blk_col * MOBA_BLOCK + colB)
        s_slot.append(s * scale + _t5_bias_from_dist(dist, rb_ref, h))

    m = jnp.max(s_own, axis=1, keepdims=True)
    for s in s_slot:
        m = jnp.maximum(m, jnp.max(s, axis=1, keepdims=True))
    p_own = jnp.exp(s_own - m)
    l = jnp.sum(p_own, axis=1, keepdims=True)
    acc = _dot(p_own.astype(BF16), vn_ref[...].astype(BF16))
    for slot in range(MOBA_TOPK):
        p = jnp.exp(s_slot[slot] - m)
        l = l + jnp.sum(p, axis=1, keepdims=True)
        for qi in range(T):
            pq = jnp.where(rowB == qi, p, 0.0).astype(BF16)
            acc = acc + _dot(pq, vbuf[cur, qi * MOBA_TOPK + slot].astype(BF16))
    o_ref[...] = acc / l


def _moba_sample(sel, page_table, rel_bias, proj_s, cache_k, cache_v, B, T, past, layer):
    n_slots = T * MOBA_TOPK
    return pl.pallas_call(
        functools.partial(_moba_sample_kernel, T=T, past=past, layer=layer),
        grid_spec=pltpu.PrefetchScalarGridSpec(
            num_scalar_prefetch=3,
            grid=(B, H_A),
            in_specs=[pl.BlockSpec((T, HEAD_DIM), lambda b, h, *_: (b, h)),
                      pl.BlockSpec((T, HEAD_DIM), lambda b, h, *_: (b, H_A + h)),
                      pl.BlockSpec((T, HEAD_DIM), lambda b, h, *_: (b, 2 * H_A + h)),
                      pl.BlockSpec(memory_space=pl.ANY),
                      pl.BlockSpec(memory_space=pl.ANY)],
            out_specs=pl.BlockSpec((T, HEAD_DIM), lambda b, h, *_: (b, h)),
            scratch_shapes=[pltpu.VMEM((2, n_slots, MOBA_BLOCK, HEAD_DIM), F32),
                            pltpu.VMEM((2, n_slots, MOBA_BLOCK, HEAD_DIM), F32),
                            pltpu.SemaphoreType.DMA((2, 2))],
        ),
        out_shape=jax.ShapeDtypeStruct((B * T, W_A), F32),
        compiler_params=_cparams(("arbitrary", "arbitrary")),
        name="moba_sample",
    )(sel, page_table, rel_bias, proj_s, proj_s, proj_s, cache_k, cache_v)


def _pad_rows(a, rows):
    return jnp.pad(a, ((0, 0), (0, rows - a.shape[1])) + ((0, 0),) * (a.ndim - 2))


def _mixer_states_in(c0, n0, m0, conv0):
    B = c0.shape[0]
    n0p = _pad_rows(n0, SUBLANES)
    m0p = _pad_rows(jnp.broadcast_to(m0[:, :, None], (B, H_B, LANES)), SUBLANES)
    conv0p = jnp.pad(conv0, ((0, 0), (SUBLANES - (CONV_W - 1), 0), (0, 0)))
    return c0, n0p, m0p, conv0p


def _tile(m, pref):
    return pref if m % pref == 0 else m


def _layer(x_f32, x_bf16, B, T, layer, wts, states, attn_fn, mlstm_chunk, hgrn_chunk):
    (w_main, w_gate, bgate, conv_w, conv_b, gn_b, gn_c, lb, w_out, ln1_g, ln1_b, w_up, w_down,
     ln2_g, ln2_b, alpha) = wts
    c0, n0, m0, conv0, s0 = states
    M = B * T
    tm = _tile(M, 1024)
    proj = _matmul(x_bf16, w_main, layer, tm, 512)
    gates = _matmul(x_bf16, w_gate, layer, tm, LANES)
    ya, k_new, v_new = attn_fn(proj)
    c0, n0p, m0p, conv0p = _mixer_states_in(c0, n0, m0, conv0)
    y_dtype = BF16 if T % 16 == 0 else F32
    yb, c_new, n_new, m_new = _mlstm(proj, gates, conv0p, conv_w, conv_b, bgate, gn_b, c0, n0p, m0p, B, T,
                                     mlstm_chunk, y_dtype)
    yc, s_new = _hgrn(proj, lb, gn_c, s0, B, T, hgrn_chunk, min(HGRN_SUB, hgrn_chunk), y_dtype)
    tm2 = _tile(M, 512)
    hf, hb = _outproj(ya, yb, yc, x_f32, w_out, layer, ln1_g, ln1_b, alpha, tm2)
    of, ob = _mlp(hb, hf, w_up, w_down, layer, ln2_g, ln2_b, alpha, tm2, 1024)
    conv_new = proj.reshape(B, T, N_MAIN)[:, T - (CONV_W - 1):, 3 * W_A:3 * W_A + 2 * W_B]
    return of, ob, (k_new, v_new, c_new, n_new[:, :H_B, :], m_new[:, :H_B, 0], conv_new, s_new)


def kernel(x_prompt, x_sample, cache_k, cache_v, page_table, state_b_C, state_b_n, state_b_m, state_b_conv,
           state_c_S, w_in, b_gate, conv_w, conv_b, gn_b, gn_c, lower_bounds, rel_bias, w_out, ln1_g, ln1_b,
           w_up, w_down, ln2_g, ln2_b):
    depth = w_in.shape[0]
    Bp, Tp, D = x_prompt.shape
    Bs, Ts, _ = x_sample.shape
    n_pages = page_table.shape[1]
    past = n_pages * PAGE_SIZE
    alpha = (2 * depth) ** 0.25

    sm = jax.nn.softmax(lower_bounds.astype(F32), axis=0)
    lb_all = jnp.cumsum(sm, axis=0) - sm[0]

    w_main = jnp.concatenate([w_in[:, :, :GATE_COL0], w_in[:, :, GATE_COL0 + 2 * H_B:]], axis=-1).astype(BF16)
    w_gate = jnp.pad(w_in[:, :, GATE_COL0:GATE_COL0 + 2 * H_B], ((0, 0), (0, 0), (0, LANES - 2 * H_B))).astype(BF16)
    bgate = jnp.pad(b_gate, ((0, 0), (0, LANES - 2 * H_B)))[:, None, :]
    w_out_b = w_out.astype(BF16)
    w_up_b = w_up.astype(BF16)
    w_down_b = w_down.astype(BF16)

    bias_tab = _bias_table(rel_bias, Tp // MOBA_BLOCK)
    nb_past = past // MOBA_BLOCK
    kmean_all = _cache_kmean(cache_k, page_table).reshape(depth, Bs, nb_past, W_A)
    kmean_pad = jnp.pad(kmean_all, ((0, 0), (0, 0), (0, LANES - nb_past), (0, 0)))

    zeros_p = (jnp.zeros((Bp, H_B, HEAD_DIM, HEAD_DIM), F32), jnp.zeros((Bp, H_B, HEAD_DIM), F32),
               jnp.zeros((Bp, H_B), F32), jnp.zeros((Bp, CONV_W - 1, 2 * W_B), F32),
               jnp.zeros((Bp, H_C, HEAD_DIM, HEAD_DIM), F32))

    xp_f = x_prompt.reshape(Bp * Tp, D)
    xs_f = x_sample.reshape(Bs * Ts, D)
    xp_b = xp_f.astype(BF16)
    xs_b = xs_f.astype(BF16)
    outs = [[] for _ in range(14)]
    mlstm_chunk_p = math.gcd(Tp, 256)
    for l in range(depth):
        wts = (w_main, w_gate, bgate[l], conv_w[l], conv_b[l][None, :], gn_b[l][None, :], gn_c[l][None, :],
               lb_all[l][None, :], w_out_b, ln1_g[l][None, :], ln1_b[l][None, :], w_up_b, w_down_b,
               ln2_g[l][None, :], ln2_b[l][None, :], alpha)

        attn_p = lambda proj: _moba_prompt(proj, bias_tab, Bp, Tp)
        xp_f, xp_b, (kp, vp, Cp, nP, mP, cP, SP) = _layer(xp_f, xp_b, Bp, Tp, l, wts, zeros_p, attn_p,
                                                          mlstm_chunk_p, mlstm_chunk_p)

        def attn_s(proj, l=l):
            sel = _sample_select(proj, kmean_pad[l], Bs, Ts, nb_past)
            ya = _moba_sample(sel, page_table, rel_bias, proj, cache_k, cache_v, Bs, Ts, past, l)
            kv = proj[:, W_A:3 * W_A].reshape(Bs, Ts, 2, H_A, HEAD_DIM)
            return ya, kv[:, :, 0], kv[:, :, 1]

        st_s = (state_b_C[l], state_b_n[l], state_b_m[l], state_b_conv[l], state_c_S[l])
        xs_f, xs_b, (ks, vs, Cs, nS, mS, cS, SS) = _layer(xs_f, xs_b, Bs, Ts, l, wts, st_s, attn_s, Ts, Ts)

        for lst, val in zip(outs, (kp, vp, ks, vs, Cp, nP, mP, cP, Cs, nS, mS, cS, SP, SS)):
            lst.append(val)

    return (xp_f.reshape(Bp, Tp, D), xs_f.reshape(Bs, Ts, D)) + tuple(jnp.stack(o) for o in outs)
```

```python
import functools
import math

import numpy as np
import jax
import jax.numpy as jnp
from jax import lax
from jax.experimental import pallas as pl
from jax.experimental.pallas import tpu as pltpu

F32 = jnp.float32
BF16 = jnp.bfloat16

HEAD_DIM = 128
H_A, H_B, H_C = 8, 4, 4
W_A, W_B, W_C = H_A * HEAD_DIM, H_B * HEAD_DIM, H_C * HEAD_DIM
MOBA_BLOCK = 256
MOBA_TOPK = 3
NUM_BUCKETS = 32
MAX_DISTANCE = 2048
CONV_W = 4
EPS = 1e-5
GATE_MASK = -1e30
NEG_BIG = -1e30
PAGE_SIZE = 128
LANES = 128
SUBLANES = 8
HGRN_SUB = 32
HGRN_SAFE_DECAY = 60.0
MOBA_GROUP = 4
QB = 4
VT_EXTRA = 16
VMEM_LIMIT = 56 * 1024 * 1024

N_MAIN = 3 * W_A + 4 * W_B + 4 * W_C
GATE_COL0 = 3 * W_A + 3 * W_B


def _t5_thresholds():
    max_exact = NUM_BUCKETS // 2
    n = np.arange(1, 4 * MAX_DISTANCE, dtype=np.float32)
    large = max_exact + (np.log(n / np.float32(max_exact)) / np.float32(math.log(MAX_DISTANCE / max_exact))
                         * np.float32(NUM_BUCKETS - max_exact)).astype(np.int32)
    large = np.minimum(large, NUM_BUCKETS - 1)
    thr = []
    for b in range(max_exact + 1, NUM_BUCKETS):
        thr.append(int(np.argmax(large >= b)) + 1)
    return tuple(thr)


T5_THRESHOLDS = _t5_thresholds()


def _cparams(sem):
    return pltpu.CompilerParams(dimension_semantics=sem, vmem_limit_bytes=VMEM_LIMIT)


def _dot(a, b):
    return jnp.dot(a, b, preferred_element_type=F32)


def _dot_nt(a, b):
    return lax.dot_general(a, b, (((1,), (1,)), ((), ())), preferred_element_type=F32)


def _dot_tn(a, b):
    return lax.dot_general(a, b, (((0,), (0,)), ((), ())), preferred_element_type=F32)


def _dot_hi(a, b):
    return jnp.dot(a, b, precision=lax.Precision.HIGHEST, preferred_element_type=F32)


def _dot_nt_hi(a, b):
    return lax.dot_general(a, b, (((1,), (1,)), ((), ())), precision=lax.Precision.HIGHEST,
                           preferred_element_type=F32)


def _sigmoid(x):
    return 1.0 / (1.0 + jnp.exp(-x))


def _layer_norm(z, g, b):
    mu = jnp.mean(z, axis=-1, keepdims=True)
    zc = z - mu
    var = jnp.mean(zc * zc, axis=-1, keepdims=True)
    return zc * lax.rsqrt(var + EPS) * g + b


def _matmul_kernel(x_ref, w_ref, o_ref):
    o_ref[...] = _dot(x_ref[...], w_ref[...])


def _matmul(x, w, layer, tm, tn, col_blk0, n_blk):
    M, K = x.shape
    N = n_blk * tn
    return pl.pallas_call(
        _matmul_kernel,
        grid=(M // tm, n_blk),
        in_specs=[pl.BlockSpec((tm, K), lambda i, j: (i, 0)),
                  pl.BlockSpec((None, K, tn), lambda i, j: (layer, 0, col_blk0 + j))],
        out_specs=pl.BlockSpec((tm, tn), lambda i, j: (i, j)),
        out_shape=jax.ShapeDtypeStruct((M, N), F32),
        compiler_params=_cparams(("arbitrary", "arbitrary")),
        name="in_proj",
    )(x, w)


def _t5_bias_from_dist(dist, rb_ref, h):
    n = jnp.maximum(dist, 0)
    large = jnp.full(n.shape, NUM_BUCKETS // 2, jnp.int32)
    for thr in T5_THRESHOLDS:
        large = large + (n >= thr).astype(jnp.int32)
    bucket = jnp.where(n < NUM_BUCKETS // 2, n, large)
    val = jnp.zeros(n.shape, F32)
    for b in range(NUM_BUCKETS):
        val = jnp.where(bucket == b, rb_ref[b, h], val)
    return val


def _bias_table_kernel(rb_ref, o_ref, *, nb):
    h = pl.program_id(0)
    blk = MOBA_BLOCK
    d_const = -(-(T5_THRESHOLDS[-1] - 1) // blk) + 1
    row = lax.broadcasted_iota(jnp.int32, (blk, blk), 0)
    col = lax.broadcasted_iota(jnp.int32, (blk, blk), 1)
    for e in range(2 * nb - 1):
        d = nb - 1 - e
        if d >= d_const:
            tile = jnp.full((blk, blk), rb_ref[NUM_BUCKETS - 1, h], F32)
        elif d < 0:
            tile = jnp.full((blk, blk), rb_ref[0, h], F32)
        else:
            tile = _t5_bias_from_dist(d * blk + col - row, rb_ref, h)
        o_ref[e * blk:(e + 1) * blk, :] = tile


def _bias_table(rel_bias, nb):
    ne = 2 * nb - 1
    return pl.pallas_call(
        functools.partial(_bias_table_kernel, nb=nb),
        grid=(H_A,),
        in_specs=[pl.BlockSpec(memory_space=pltpu.SMEM)],
        out_specs=pl.BlockSpec((None, ne * MOBA_BLOCK, MOBA_BLOCK), lambda h: (h, 0, 0)),
        out_shape=jax.ShapeDtypeStruct((H_A, ne * MOBA_BLOCK, MOBA_BLOCK), F32),
        compiler_params=_cparams(("arbitrary",)),
        name="t5_bias_table",
    )(rel_bias)


def _topk_select(gate, n_valid, n_cand):
    lane = lax.broadcasted_iota(jnp.int32, gate.shape, 1)
    gm = jnp.where(lane < n_valid, gate, GATE_MASK)
    rank = jnp.zeros(gate.shape, jnp.int32)
    for c in range(n_cand):
        gc = gm[:, c:c + 1]
        ahead = (gc > gm) | ((gc == gm) & (c < lane))
        rank = rank + ahead.astype(jnp.int32)
    return rank, lane


def _moba_prompt_kernel(q_ref, k_ref, v_ref, bias_ref, o_ref, kout_ref, vout_ref,
                        ka_s, vt_s, vtd_s, qa_s, kmean_s, sem, *, nb):
    b = pl.program_id(0)
    h = pl.program_id(1)
    i = pl.program_id(2)
    blk = MOBA_BLOCK
    G = MOBA_GROUP
    nbp = -(-nb // SUBLANES) * SUBLANES

    kv_copies = (pltpu.make_async_copy(k_ref, kout_ref.at[b, :, h, :], sem.at[0]),
                 pltpu.make_async_copy(v_ref, vout_ref.at[b, :, h, :], sem.at[1]))

    @pl.when(i == 0)
    def _():
        for c in kv_copies:
            c.start()
        kmean_s[...] = jnp.zeros(kmean_s.shape, F32)
        lane = lax.broadcasted_iota(jnp.int32, (blk, HEAD_DIM), 1)
        for n in range(nb):
            rows = slice((n % G) * blk, (n % G + 1) * blk)
            kf = k_ref[n * blk:(n + 1) * blk, :]
            ka_s[n // G, rows, 0:HEAD_DIM] = kf.astype(BF16)
            ka_s[n // G, rows, HEAD_DIM:2 * HEAD_DIM] = jnp.where(lane == n, NEG_BIG, 0.0).astype(BF16)
            ones_row = jnp.where(lax.broadcasted_iota(jnp.int32, (VT_EXTRA, blk), 0) == 0, 1.0, 0.0)
            vt = jnp.concatenate([v_ref[n * blk:(n + 1) * blk, :].T, ones_row], axis=0).astype(BF16)
            vt_s[n // G, :, rows] = vt
            vtd_s[n] = vt
            kmean_s[n:n + 1, :] = jnp.mean(kf, axis=0, keepdims=True)
        km = kmean_s[...]
        sub = lax.broadcasted_iota(jnp.int32, (nbp, blk), 0)
        pad = jnp.zeros((HEAD_DIM - nbp, blk), F32)
        for t in range(nb):
            q = q_ref[t * blk:(t + 1) * blk, :]
            if t > MOBA_TOPK:
                gm = jnp.where(sub < t, _dot_nt_hi(km, q), GATE_MASK)
                rank = jnp.zeros((nbp, blk), jnp.int32)
                for c in range(t):
                    gc = gm[c:c + 1, :]
                    rank = rank + ((gc > gm) | ((gc == gm) & (c < sub))).astype(jnp.int32)
                notsel = jnp.where((sub < t) & (rank < MOBA_TOPK), 0.0, 1.0)
            else:
                notsel = jnp.where(sub < t, 0.0, 1.0)
            qt = (q * (HEAD_DIM ** -0.5)).T
            qa_s[t // QB, :, (t % QB) * blk:(t % QB + 1) * blk] = (
                jnp.concatenate([qt, notsel, pad], axis=0).astype(BF16))

    qaug = qa_s[i]
    rowk = lax.broadcasted_iota(jnp.int32, (blk, blk), 0)
    colq = lax.broadcasted_iota(jnp.int32, (blk, blk), 1)

    def tile(n_grp):
        m_parts, acc_parts = [], []
        for u in range(QB):
            iu = i * QB + u
            kd = ka_s[iu // G, pl.ds(pl.multiple_of((iu % G) * blk, blk), blk), 0:HEAD_DIM]
            sd = _dot(kd, qaug[0:HEAD_DIM, u * blk:(u + 1) * blk]) + bias_ref[(nb - 1) * blk:nb * blk, :]
            sd = jnp.where(colq >= rowk, sd, NEG_BIG)
            mu = jnp.max(sd, axis=0, keepdims=True)
            m_parts.append(mu)
            acc_parts.append(_dot(vtd_s[iu], jnp.exp(sd - mu).astype(BF16)))
        m = jnp.concatenate(m_parts, axis=1)
        acc = jnp.concatenate(acc_parts, axis=1)
        for g in range(n_grp):
            bias = jnp.concatenate(
                [bias_ref[pl.ds(pl.multiple_of((nb - 1 - (i * QB + u) + G * g) * blk, blk), G * blk), :]
                 for u in range(QB)], axis=1)
            s = _dot(ka_s[g], qaug) + bias
            m_new = jnp.maximum(m, jnp.max(s, axis=0, keepdims=True))
            acc = jnp.exp(m - m_new) * acc + _dot(vt_s[g], jnp.exp(s - m_new).astype(BF16))
            m = m_new
        o_ref[...] = (acc[0:HEAD_DIM] / acc[HEAD_DIM:HEAD_DIM + 1]).T.astype(o_ref.dtype)

    steps_per_grp = G // QB
    for grp in range(nb // G):
        pl.when(i // steps_per_grp == grp)(functools.partial(tile, grp + 1))

    @pl.when(i == nb // QB - 1)
    def _():
        for c in kv_copies:
            c.wait()


def _moba_prompt(proj, bias_tab, B, T):
    nb = T // MOBA_BLOCK
    G = MOBA_GROUP
    assert nb % G == 0 and G % QB == 0 and nb >= 2 * G, "key blocks are processed in groups"
    nq = nb // QB
    kv_shape = jax.ShapeDtypeStruct((B, T, H_A, HEAD_DIM), F32)
    return pl.pallas_call(
        functools.partial(_moba_prompt_kernel, nb=nb),
        grid=(B, H_A, nq),
        in_specs=[pl.BlockSpec((T, HEAD_DIM), lambda b, h, i: (b, h)),
                  pl.BlockSpec((T, HEAD_DIM), lambda b, h, i: (b, H_A + h)),
                  pl.BlockSpec((T, HEAD_DIM), lambda b, h, i: (b, 2 * H_A + h)),
                  pl.BlockSpec((None, (2 * nb - 1) * MOBA_BLOCK, MOBA_BLOCK), lambda b, h, i: (h, 0, 0))],
        out_specs=[pl.BlockSpec((QB * MOBA_BLOCK, HEAD_DIM), lambda b, h, i: (b * nq + i, h)),
                   pl.BlockSpec(memory_space=pl.ANY), pl.BlockSpec(memory_space=pl.ANY)],
        out_shape=[jax.ShapeDtypeStruct((B * T, W_A), BF16), kv_shape, kv_shape],
        scratch_shapes=[pltpu.VMEM((nb // G, G * MOBA_BLOCK, 2 * HEAD_DIM), BF16),
                        pltpu.VMEM((nb // G, HEAD_DIM + VT_EXTRA, G * MOBA_BLOCK), BF16),
                        pltpu.VMEM((nb, HEAD_DIM + VT_EXTRA, MOBA_BLOCK), BF16),
                        pltpu.VMEM((nq, 2 * HEAD_DIM, QB * MOBA_BLOCK), BF16),
                        pltpu.VMEM((-(-nb // SUBLANES) * SUBLANES, HEAD_DIM), F32),
                        pltpu.SemaphoreType.DMA((2,))],
        compiler_params=_cparams(("arbitrary", "arbitrary", "arbitrary")),
        name="moba_prompt",
    )(proj, proj, proj, bias_tab)


def _mlstm_kernel(q_ref, k_ref, qp_ref, kp_ref, v_ref, og_ref, g_ref, conv0_ref, cw_ref, cb_ref, bg_ref,
                  gn_ref, c0_ref, n0_ref, m0_ref,
                  y_ref, cout_ref, nout_ref, mout_ref,
                  c_s, n_s, m_s, ext_s, *, L):
    c = pl.program_id(1)
    last = pl.num_programs(1) - 1

    @pl.when(c == 0)
    def _():
        c_s[...] = c0_ref[...]
        n_s[...] = n0_ref[...]
        m_s[...] = m0_ref[...]

    def conv_silu(u_ref, up_ref, col0):
        u = u_ref[...]
        tail = jnp.where(c == 0, conv0_ref[:, col0:col0 + W_B], up_ref[L - SUBLANES:L, :])
        ext_s[0:SUBLANES, :] = tail
        ext_s[SUBLANES:SUBLANES + L, :] = u
        acc = u * cw_ref[CONV_W - 1:CONV_W, col0:col0 + W_B] + cb_ref[:, col0:col0 + W_B]
        for j in range(1, CONV_W):
            xj = ext_s[SUBLANES - j:SUBLANES - j + L, :]
            acc = acc + xj * cw_ref[CONV_W - 1 - j:CONV_W - j, col0:col0 + W_B]
        return acc * _sigmoid(acc)

    qc = conv_silu(q_ref, qp_ref, 0)
    kc = conv_silu(k_ref, kp_ref, W_B) * (HEAD_DIM ** -0.5)
    v = v_ref[...]
    og = og_ref[...]

    g = g_ref[...] + bg_ref[...]
    lf = jnp.minimum(g, 0.0) - jnp.log(1.0 + jnp.exp(-jnp.abs(g)))
    row = lax.broadcasted_iota(jnp.int32, (L, L), 0)
    col = lax.broadcasted_iota(jnp.int32, (L, L), 1)
    causal = row >= col
    fcum = _dot_hi(causal.astype(F32), lf)

    for h in range(H_B):
        hs = slice(h * HEAD_DIM, (h + 1) * HEAD_DIM)
        fcol = fcum[:, H_B + h:H_B + h + 1]
        rcol = g[:, h:h + 1] - fcol
        rrow = jnp.sum(jnp.where(row == col, rcol, 0.0), axis=0, keepdims=True)
        dm = jnp.where(causal, fcol + rrow, -jnp.inf)
        mprev = m_s[h:h + 1, 0:1]
        gcol = fcol + mprev
        mt = jnp.maximum(gcol, jnp.max(dm, axis=1, keepdims=True))
        w = jnp.exp(dm - mt)
        wg = jnp.exp(gcol - mt)
        qh = qc[:, hs]
        kh = kc[:, hs]
        vh = v[:, hs]
        qhb = qh.astype(BF16)
        s = _dot_nt(qhb, kh.astype(BF16)) * w
        num = _dot(s.astype(BF16), vh.astype(BF16)) + wg * _dot(qhb, c_s[h].astype(BF16))
        den = jnp.sum(s, axis=1, keepdims=True) + wg * jnp.sum(qh * n_s[h:h + 1, :], axis=1, keepdims=True)
        hh = num / jnp.maximum(jnp.abs(den), jnp.exp(-mt))
        ml = mt[L - 1:L, :]
        wl = jnp.exp(fcol[L - 1:L, :] + rcol - ml)
        gl = jnp.exp(gcol[L - 1:L, :] - ml)
        kw = kh * wl
        c_s[h] = gl * c_s[h] + _dot_tn(kw.astype(BF16), vh.astype(BF16))
        n_s[h:h + 1, :] = gl * n_s[h:h + 1, :] + jnp.sum(kw, axis=0, keepdims=True)
        m_s[h:h + 1, :] = jnp.broadcast_to(ml, (1, LANES))
        hc = hh - jnp.mean(hh, axis=1, keepdims=True)
        yn = hc * lax.rsqrt(jnp.mean(hc * hc, axis=1, keepdims=True) + EPS) * gn_ref[:, hs]
        y_ref[:, hs] = (_sigmoid(og[:, hs]) * yn).astype(y_ref.dtype)

    @pl.when(c == last)
    def _():
        cout_ref[...] = c_s[...]
        nout_ref[...] = n_s[...]
        mout_ref[...] = m_s[...]


def _mlstm(proj, proj_b, gates, conv0, conv_w, conv_b, bgate, gn_b, c0, n0, m0, B, T, L, out_dtype):
    nc = T // L
    q_blk, k_blk, v_blk = 3 * W_A // W_B, 3 * W_A // W_B + 1, 3 * W_A // W_B + 2

    def cur(colblk):
        return pl.BlockSpec((L, W_B), lambda b, c: (b * nc + c, colblk))

    def prev(colblk):
        return pl.BlockSpec((L, W_B), lambda b, c: (b * nc + jnp.maximum(c - 1, 0), colblk))

    full2 = lambda shape: pl.BlockSpec(shape, lambda b, c: (0, 0))
    per_b3 = lambda shape: pl.BlockSpec((None,) + shape, lambda b, c: (b, 0, 0))
    return pl.pallas_call(
        functools.partial(_mlstm_kernel, L=L),
        grid=(B, nc),
        in_specs=[cur(q_blk), cur(k_blk), prev(q_blk), prev(k_blk), cur(v_blk), cur(0),
                  pl.BlockSpec((L, LANES), lambda b, c: (b * nc + c, 0)),
                  per_b3((SUBLANES, 2 * W_B)),
                  full2((CONV_W, 2 * W_B)), full2((1, 2 * W_B)), full2((1, LANES)), full2((1, W_B)),
                  pl.BlockSpec((None, H_B, HEAD_DIM, HEAD_DIM), lambda b, c: (b, 0, 0, 0)),
                  per_b3((SUBLANES, HEAD_DIM)), per_b3((SUBLANES, LANES))],
        out_specs=[pl.BlockSpec((L, W_B), lambda b, c: (b * nc + c, 0)),
                   pl.BlockSpec((None, H_B, HEAD_DIM, HEAD_DIM), lambda b, c: (b, 0, 0, 0)),
                   per_b3((SUBLANES, HEAD_DIM)), per_b3((SUBLANES, LANES))],
        out_shape=[jax.ShapeDtypeStruct((B * T, W_B), out_dtype),
                   jax.ShapeDtypeStruct((B, H_B, HEAD_DIM, HEAD_DIM), F32),
                   jax.ShapeDtypeStruct((B, SUBLANES, HEAD_DIM), F32),
                   jax.ShapeDtypeStruct((B, SUBLANES, LANES), F32)],
        scratch_shapes=[pltpu.VMEM((H_B, HEAD_DIM, HEAD_DIM), F32), pltpu.VMEM((SUBLANES, HEAD_DIM), F32),
                        pltpu.VMEM((SUBLANES, LANES), F32), pltpu.VMEM((L + SUBLANES, W_B), F32)],
        compiler_params=_cparams(("arbitrary", "arbitrary")),
        name="mlstm",
    )(proj, proj, proj, proj, proj, proj_b, gates, conv0, conv_w, conv_b, bgate, gn_b, c0, n0, m0)


def _hgrn_kernel(q_ref, f_ref, i_ref, g_ref, lb_ref, gn_ref, s0_ref, y_ref, sout_ref,
                 st_s, k_s, b_s, v_s, q_s, o_s, *, LC, LS):
    c = pl.program_id(1)
    last = pl.num_programs(1) - 1

    @pl.when(c == 0)
    def _():
        for h in range(H_C):
            st_s[h] = s0_ref[h].T

    lb = lb_ref[...]
    one_m_lb = 1.0 - lb
    row = lax.broadcasted_iota(jnp.int32, (LC, LC), 0)
    col = lax.broadcasted_iota(jnp.int32, (LC, LC), 1)
    same_sub = (row // LS) == (col // LS)
    intra = same_sub & (row >= col)
    row8 = lax.broadcasted_iota(jnp.int32, (SUBLANES, W_C), 0)

    fc = f_ref[...]
    qc = q_ref[...]
    logf = jnp.log(lb + one_m_lb * _sigmoid(fc))
    kk = one_m_lb * _sigmoid(-fc)
    qq = qc * _sigmoid(qc)
    vv = i_ref[...]
    b = _dot_hi(intra.astype(F32), logf)
    k_s[...] = kk
    b_s[...] = b
    v_s[...] = vv
    q_s[...] = qq

    safe = jnp.min(b) > -HGRN_SAFE_DECAY

    @pl.when(safe)
    def _():
        qe = qq * jnp.exp(b)
        kinv = kk * jnp.exp(-b)
        for h in range(H_C):
            hs = slice(h * HEAD_DIM, (h + 1) * HEAD_DIM)
            a = jnp.where(intra, _dot_nt(qe[:, hs].astype(BF16), kinv[:, hs].astype(BF16)), 0.0)
            o_s[:, hs] = _dot(a.astype(BF16), vv[:, hs].astype(BF16))

    @pl.when(jnp.logical_not(safe))
    def _():
        def exact_sub(sc, carry):
            r = pl.multiple_of(sc * LS, LS)
            for rb in range(LS // SUBLANES):
                t0 = rb * SUBLANES
                qb_ = q_s[pl.ds(r + t0, SUBLANES), :]
                bb_ = b_s[pl.ds(r + t0, SUBLANES), :]
                o_h = [jnp.zeros((SUBLANES, HEAD_DIM), F32) for _ in range(H_C)]
                for s in range(t0 + SUBLANES):
                    d = bb_ - b_s[pl.ds(r + s, 1), :]
                    if s >= t0:
                        d = jnp.where(row8 >= (s - t0), d, -jnp.inf)
                    tmp = qb_ * k_s[pl.ds(r + s, 1), :] * jnp.exp(d)
                    vs = v_s[pl.ds(r + s, 1), :]
                    for h in range(H_C):
                        hs = slice(h * HEAD_DIM, (h + 1) * HEAD_DIM)
                        o_h[h] = o_h[h] + jnp.sum(tmp[:, hs], axis=1, keepdims=True) * vs[:, hs]
                for h in range(H_C):
                    o_s[pl.ds(r + t0, SUBLANES), h * HEAD_DIM:(h + 1) * HEAD_DIM] = o_h[h]
            return carry

        lax.fori_loop(0, LC // LS, exact_sub, 0)

    def sub(sc, carry):
        r = pl.multiple_of(sc * LS, LS)
        bs = b_s[pl.ds(r, LS), :]
        qe = q_s[pl.ds(r, LS), :] * jnp.exp(bs)
        bl = bs[LS - 1:LS, :]
        ke = k_s[pl.ds(r, LS), :] * jnp.exp(bl - bs)
        dec = jnp.exp(bl)
        vs = v_s[pl.ds(r, LS), :]
        gg = g_ref[pl.ds(r, LS), :]
        for h in range(H_C):
            hs = slice(h * HEAD_DIM, (h + 1) * HEAD_DIM)
            st = st_s[h]
            o = _dot_nt(qe[:, hs].astype(BF16), st.astype(BF16)) + o_s[pl.ds(r, LS), hs]
            st_s[h] = dec[:, hs] * st + _dot_tn(vs[:, hs].astype(BF16), ke[:, hs].astype(BF16))
            yn = o * lax.rsqrt(jnp.mean(o * o, axis=1, keepdims=True) + EPS) * gn_ref[:, hs]
            gh = gg[:, hs]
            y_ref[pl.ds(r, LS), hs] = (yn * (gh * _sigmoid(gh))).astype(y_ref.dtype)
        return carry

    lax.fori_loop(0, LC // LS, sub, 0)

    @pl.when(c == last)
    def _():
        for h in range(H_C):
            sout_ref[h] = st_s[h].T


def _hgrn(proj, lb, gn_c, s0, B, T, LC, LS, out_dtype):
    nc = T // LC
    base = W_B // W_C
    blk = lambda k: pl.BlockSpec((LC, W_C), lambda b, c: (b * nc + c, base + k))
    full2 = lambda shape: pl.BlockSpec(shape, lambda b, c: (0, 0))
    st_spec = pl.BlockSpec((None, H_C, HEAD_DIM, HEAD_DIM), lambda b, c: (b, 0, 0, 0))
    return pl.pallas_call(
        functools.partial(_hgrn_kernel, LC=LC, LS=LS),
        grid=(B, nc),
        in_specs=[blk(0), blk(1), blk(2), blk(3), full2((1, W_C)), full2((1, W_C)), st_spec],
        out_specs=[pl.BlockSpec((LC, W_C), lambda b, c: (b * nc + c, 0)), st_spec],
        out_shape=[jax.ShapeDtypeStruct((B * T, W_C), out_dtype),
                   jax.ShapeDtypeStruct((B, H_C, HEAD_DIM, HEAD_DIM), F32)],
        scratch_shapes=[pltpu.VMEM((H_C, HEAD_DIM, HEAD_DIM), F32)] + [pltpu.VMEM((LC, W_C), F32)] * 5,
        compiler_params=_cparams(("arbitrary", "arbitrary")),
        name="hgrn2",
    )(proj, proj, proj, proj, lb, gn_c, s0)


def _outproj_kernel(ya_ref, yb_ref, yc_ref, x_ref, w_ref, g_ref, b_ref, hf_ref, hb_ref, *, alpha):
    mix = (_dot(ya_ref[...].astype(BF16), w_ref[0:W_A, :])
           + _dot(yb_ref[...].astype(BF16), w_ref[W_A:W_A + W_B, :])
           + _dot(yc_ref[...].astype(BF16), w_ref[W_A + W_B:W_A + W_B + W_C, :]))
    h = _layer_norm(alpha * x_ref[...] + mix, g_ref[...], b_ref[...])
    hf_ref[...] = h
    hb_ref[...] = h.astype(BF16)


def _outproj(ya, yb, yc, x, w, layer, g, b, alpha, tm):
    M, D = x.shape
    rows = lambda n: pl.BlockSpec((tm, n), lambda i: (i, 0))
    full = lambda shape: pl.BlockSpec(shape, lambda i: (0, 0))
    w_spec = pl.BlockSpec((None,) + w.shape[1:], lambda i: (layer, 0, 0))
    return pl.pallas_call(
        functools.partial(_outproj_kernel, alpha=alpha),
        grid=(M // tm,),
        in_specs=[rows(W_A), rows(W_B), rows(W_C), rows(D), w_spec, full((1, D)), full((1, D))],
        out_specs=[rows(D), rows(D)],
        out_shape=[jax.ShapeDtypeStruct((M, D), F32), jax.ShapeDtypeStruct((M, D), BF16)],
        compiler_params=_cparams(("arbitrary",)),
        name="out_proj_ln1",
    )(ya, yb, yc, x, w, g, b)


def _mlp_kernel(hb_ref, hf_ref, wu_ref, wd_ref, g_ref, b_ref, of_ref, ob_ref, acc_s, *, alpha):
    f = pl.program_id(1)
    @pl.when(f == 0)
    def _():
        acc_s[...] = jnp.zeros(acc_s.shape, F32)

    u = jnp.maximum(_dot(hb_ref[...], wu_ref[...]), 0.0)
    acc_s[...] += _dot((u * u).astype(BF16), wd_ref[...])

    @pl.when(f == pl.num_programs(1) - 1)
    def _():
        o = _layer_norm(alpha * hf_ref[...] + acc_s[...], g_ref[...], b_ref[...])
        of_ref[...] = o
        ob_ref[...] = o.astype(BF16)


def _mlp(hb, hf, wu, wd, layer, g, b, alpha, tm, tf):
    M, D = hf.shape
    FF = wu.shape[2]
    rows = lambda: pl.BlockSpec((tm, D), lambda i, f: (i, 0))
    vec = lambda: pl.BlockSpec((1, D), lambda i, f: (0, 0))
    return pl.pallas_call(
        functools.partial(_mlp_kernel, alpha=alpha),
        grid=(M // tm, FF // tf),
        in_specs=[rows(), rows(), pl.BlockSpec((None, D, tf), lambda i, f: (layer, 0, f)),
                  pl.BlockSpec((None, tf, D), lambda i, f: (layer, f, 0)), vec(), vec()],
        out_specs=[rows(), rows()],
        out_shape=[jax.ShapeDtypeStruct((M, D), F32), jax.ShapeDtypeStruct((M, D), BF16)],
        scratch_shapes=[pltpu.VMEM((tm, D), F32)],
        compiler_params=_cparams(("arbitrary", "arbitrary")),
        name="mlp_ln2",
    )(hb, hf, wu, wd, g, b)


PAGES_PER_STEP = 16


def _kmean_kernel(pt_ref, *refs):
    page_refs, o_ref = refs[:PAGES_PER_STEP], refs[PAGES_PER_STEP]
    per_blk = MOBA_BLOCK // PAGE_SIZE
    for u in range(PAGES_PER_STEP // per_blk):
        tot = page_refs[per_blk * u][...].sum(axis=0)
        for e in range(1, per_blk):
            tot = tot + page_refs[per_blk * u + e][...].sum(axis=0)
        o_ref[u] = tot * (1.0 / MOBA_BLOCK)


def _cache_kmean(cache_k, page_table):
    depth = cache_k.shape[0]
    B, n_pages = page_table.shape
    nb = n_pages * PAGE_SIZE // MOBA_BLOCK
    steps = n_pages // PAGES_PER_STEP
    blocks_per_step = PAGES_PER_STEP * PAGE_SIZE // MOBA_BLOCK

    def page_spec(u):
        return pl.BlockSpec((None, None, PAGE_SIZE, H_A, HEAD_DIM),
                            lambda l, b, g, pt: (l, pt[b, g * PAGES_PER_STEP + u], 0, 0, 0))

    return pl.pallas_call(
        _kmean_kernel,
        grid_spec=pltpu.PrefetchScalarGridSpec(
            num_scalar_prefetch=1,
            grid=(depth, B, steps),
            in_specs=[page_spec(u) for u in range(PAGES_PER_STEP)],
            out_specs=pl.BlockSpec((None, None, blocks_per_step, H_A, HEAD_DIM),
                                   lambda l, b, g, pt: (l, b, g, 0, 0)),
        ),
        out_shape=jax.ShapeDtypeStruct((depth, B, nb, H_A, HEAD_DIM), F32),
        compiler_params=_cparams(("arbitrary", "arbitrary", "arbitrary")),
        name="cache_kmean",
    )(page_table, *([cache_k] * PAGES_PER_STEP))


def _sample_select_kernel(q_ref, km_ref, o_ref, *, nb):
    out = jnp.zeros(o_ref.shape, jnp.int32)
    lane_o = lax.broadcasted_iota(jnp.int32, o_ref.shape, 1)
    for h in range(H_A):
        hs = slice(h * HEAD_DIM, (h + 1) * HEAD_DIM)
        gate = _dot_nt_hi(q_ref[:, hs], km_ref[:, hs])
        rank, lane = _topk_select(gate, nb, nb)
        for slot in range(MOBA_TOPK):
            pick = (lane < nb) & (rank == slot)
            idx = jnp.sum(jnp.where(pick, lane, 0), axis=1, keepdims=True)
            out = jnp.where(lane_o == h * 4 + slot, idx, out)
    o_ref[...] = out


def _sample_select(proj_s, kmean_pad, B, T, nb):
    return pl.pallas_call(
        functools.partial(_sample_select_kernel, nb=nb),
        grid=(B,),
        in_specs=[pl.BlockSpec((T, W_A), lambda b: (b, 0)),
                  pl.BlockSpec((None, LANES, W_A), lambda b: (b, 0, 0))],
        out_specs=pl.BlockSpec((T, LANES), lambda b: (b, 0)),
        out_shape=jax.ShapeDtypeStruct((B * T, LANES), jnp.int32),
        compiler_params=_cparams(("arbitrary",)),
        name="moba_sample_select",
    )(proj_s, kmean_pad)


def _moba_sample_kernel(sel_ref, pt_ref, rb_ref, q_ref, kn_ref, vn_ref, ck_ref, cv_ref, o_ref,
                        kbuf, vbuf, sem, *, T, past, layer):
    b = pl.program_id(0)
    h = pl.program_id(1)
    nh = pl.num_programs(1)
    step = b * nh + h
    n_steps = pl.num_programs(0) * nh
    per_blk = MOBA_BLOCK // PAGE_SIZE

    def copies(bb, hh, buf, qi, slot, e):
        blk = sel_ref[bb * T + qi, hh * 4 + slot]
        page = pt_ref[bb, blk * per_blk + e]
        idx = qi * MOBA_TOPK + slot
        dst = pl.ds(e * PAGE_SIZE, PAGE_SIZE)
        return (pltpu.make_async_copy(ck_ref.at[layer, page, :, hh, :], kbuf.at[buf, idx, dst, :], sem.at[buf, 0]),
                pltpu.make_async_copy(cv_ref.at[layer, page, :, hh, :], vbuf.at[buf, idx, dst, :], sem.at[buf, 1]))

    def for_all_copies(bb, hh, buf, fn):
        for qi in range(T):
            for slot in range(MOBA_TOPK):
                for e in range(per_blk):
                    for c in copies(bb, hh, buf, qi, slot, e):
                        fn(c)

    cur = step % 2

    @pl.when(step == 0)
    def _():
        for_all_copies(b, h, 0, lambda c: c.start())

    @pl.when(step + 1 < n_steps)
    def _():
        nxt = step + 1
        for_all_copies(nxt // nh, nxt % nh, 1 - cur, lambda c: c.start())

    q = q_ref[...]
    qb = q.astype(BF16)
    scale = HEAD_DIM ** -0.5
    rowT = lax.broadcasted_iota(jnp.int32, (T, T), 0)
    colT = lax.broadcasted_iota(jnp.int32, (T, T), 1)
    s_own = _dot_nt(qb, kn_ref[...].astype(BF16)) * scale + _t5_bias_from_dist(rowT - colT, rb_ref, h)
    s_own = jnp.where(rowT >= colT, s_own, NEG_BIG)

    for_all_copies(b, h, cur, lambda c: c.wait())

    rowB = lax.broadcasted_iota(jnp.int32, (T, MOBA_BLOCK), 0)
    colB = lax.broadcasted_iota(jnp.int32, (T, MOBA_BLOCK), 1)
    row1 = lax.broadcasted_iota(jnp.int32, (T, 1), 0)
    s_slot = []
    for slot in range(MOBA_TOPK):
        s = jnp.zeros((T, MOBA_BLOCK), F32)
        blk_col = jnp.zeros((T, 1), jnp.int32)
        for qi in range(T):
            sq = _dot_nt(qb, kbuf[cur, qi * MOBA_TOPK + slot].astype(BF16))
            s = jnp.where(rowB == qi, sq, s)
            blk_col = jnp.where(row1 == qi, sel_ref[b * T + qi, h * 4 + slot], blk_col)
        dist = past + rowB - (blk_col * MOBA_BLOCK + colB)
        s_slot.append(s * scale + _t5_bias_from_dist(dist, rb_ref, h))

    m = jnp.max(s_own, axis=1, keepdims=True)
    for s in s_slot:
        m = jnp.maximum(m, jnp.max(s, axis=1, keepdims=True))
    p_own = jnp.exp(s_own - m)
    l = jnp.sum(p_own, axis=1, keepdims=True)
    acc = _dot(p_own.astype(BF16), vn_ref[...].astype(BF16))
    for slot in range(MOBA_TOPK):
        p = jnp.exp(s_slot[slot] - m)
        l = l + jnp.sum(p, axis=1, keepdims=True)
        for qi in range(T):
            pq = jnp.where(rowB == qi, p, 0.0).astype(BF16)
            acc = acc + _dot(pq, vbuf[cur, qi * MOBA_TOPK + slot].astype(BF16))
    o_ref[...] = acc / l


def _moba_sample(sel, page_table, rel_bias, proj_s, cache_k, cache_v, B, T, past, layer):
    n_slots = T * MOBA_TOPK
    return pl.pallas_call(
        functools.partial(_moba_sample_kernel, T=T, past=past, layer=layer),
        grid_spec=pltpu.PrefetchScalarGridSpec(
            num_scalar_prefetch=3,
            grid=(B, H_A),
            in_specs=[pl.BlockSpec((T, HEAD_DIM), lambda b, h, *_: (b, h)),
                      pl.BlockSpec((T, HEAD_DIM), lambda b, h, *_: (b, H_A + h)),
                      pl.BlockSpec((T, HEAD_DIM), lambda b, h, *_: (b, 2 * H_A + h)),
                      pl.BlockSpec(memory_space=pl.ANY),
                      pl.BlockSpec(memory_space=pl.ANY)],
            out_specs=pl.BlockSpec((T, HEAD_DIM), lambda b, h, *_: (b, h)),
            scratch_shapes=[pltpu.VMEM((2, n_slots, MOBA_BLOCK, HEAD_DIM), F32),
                            pltpu.VMEM((2, n_slots, MOBA_BLOCK, HEAD_DIM), F32),
                            pltpu.SemaphoreType.DMA((2, 2))],
        ),
        out_shape=jax.ShapeDtypeStruct((B * T, W_A), F32),
        compiler_params=_cparams(("arbitrary", "arbitrary")),
        name="moba_sample",
    )(sel, page_table, rel_bias, proj_s, proj_s, proj_s, cache_k, cache_v)


def _pad_rows(a, rows):
    return jnp.pad(a, ((0, 0), (0, rows - a.shape[1])) + ((0, 0),) * (a.ndim - 2))


def _mixer_states_in(c0, n0, m0, conv0):
    B = c0.shape[0]
    n0p = _pad_rows(n0, SUBLANES)
    m0p = _pad_rows(jnp.broadcast_to(m0[:, :, None], (B, H_B, LANES)), SUBLANES)
    conv0p = jnp.pad(conv0, ((0, 0), (SUBLANES - (CONV_W - 1), 0), (0, 0)))
    return c0, n0p, m0p, conv0p


def _tile(m, pref):
    return pref if m % pref == 0 else m


def _layer(x_f32, x_bf16, B, T, layer, wts, states, attn_fn, mlstm_chunk, hgrn_chunk):
    (w_in_b, w_tail, bgate, conv_w, conv_b, gn_b, gn_c, lb, w_out, ln1_g, ln1_b, w_up, w_down,
     ln2_g, ln2_b, alpha) = wts
    c0, n0, m0, conv0, s0 = states
    M = B * T
    tm = _tile(M, 1024)
    tn = 512
    proj = _matmul(x_bf16, w_in_b, layer, tm, tn, 0, GATE_COL0 // tn)
    gates = _matmul(x_bf16, w_in_b, layer, tm, LANES, GATE_COL0 // LANES, 1)
    proj_b = _matmul(x_bf16, w_tail, layer, tm, tn, 0, w_tail.shape[2] // tn)
    ya, k_new, v_new = attn_fn(proj)
    c0, n0p, m0p, conv0p = _mixer_states_in(c0, n0, m0, conv0)
    y_dtype = BF16 if T % 16 == 0 else F32
    yb, c_new, n_new, m_new = _mlstm(proj, proj_b, gates, conv0p, conv_w, conv_b, bgate, gn_b, c0, n0p, m0p, B, T,
                                     mlstm_chunk, y_dtype)
    yc, s_new = _hgrn(proj_b, lb, gn_c, s0, B, T, hgrn_chunk, min(HGRN_SUB, hgrn_chunk), y_dtype)
    tm2 = _tile(M, 512)
    hf, hb = _outproj(ya, yb, yc, x_f32, w_out, layer, ln1_g, ln1_b, alpha, tm2)
    of, ob = _mlp(hb, hf, w_up, w_down, layer, ln2_g, ln2_b, alpha, tm2, 1024)
    conv_new = proj.reshape(B, T, GATE_COL0)[:, T - (CONV_W - 1):, 3 * W_A:3 * W_A + 2 * W_B]
    return of, ob, (k_new, v_new, c_new, n_new[:, :H_B, :], m_new[:, :H_B, 0], conv_new, s_new)


def kernel(x_prompt, x_sample, cache_k, cache_v, page_table, state_b_C, state_b_n, state_b_m, state_b_conv,
           state_c_S, w_in, b_gate, conv_w, conv_b, gn_b, gn_c, lower_bounds, rel_bias, w_out, ln1_g, ln1_b,
           w_up, w_down, ln2_g, ln2_b):
    depth = w_in.shape[0]
    Bp, Tp, D = x_prompt.shape
    Bs, Ts, _ = x_sample.shape
    n_pages = page_table.shape[1]
    past = n_pages * PAGE_SIZE
    alpha = (2 * depth) ** 0.25

    sm = jax.nn.softmax(lower_bounds.astype(F32), axis=0)
    lb_all = jnp.cumsum(sm, axis=0) - sm[0]

    w_in_b = w_in.astype(BF16)
    w_tail = w_in_b[:, :, GATE_COL0 + 2 * H_B:]
    bgate = jnp.pad(b_gate, ((0, 0), (0, LANES - 2 * H_B)))[:, None, :]
    w_out_b = w_out.astype(BF16)
    w_up_b = w_up.astype(BF16)
    w_down_b = w_down.astype(BF16)

    bias_tab = _bias_table(rel_bias, Tp // MOBA_BLOCK)
    nb_past = past // MOBA_BLOCK
    kmean_all = _cache_kmean(cache_k, page_table).reshape(depth, Bs, nb_past, W_A)
    kmean_pad = jnp.pad(kmean_all, ((0, 0), (0, 0), (0, LANES - nb_past), (0, 0)))

    zeros_p = (jnp.zeros((Bp, H_B, HEAD_DIM, HEAD_DIM), F32), jnp.zeros((Bp, H_B, HEAD_DIM), F32),
               jnp.zeros((Bp, H_B), F32), jnp.zeros((Bp, CONV_W - 1, 2 * W_B), F32),
               jnp.zeros((Bp, H_C, HEAD_DIM, HEAD_DIM), F32))

    xp_f = x_prompt.reshape(Bp * Tp, D)
    xs_f = x_sample.reshape(Bs * Ts, D)
    xp_b = xp_f.astype(BF16)
    xs_b = xs_f.astype(BF16)
    outs = [[] for _ in range(14)]
    mlstm_chunk_p = math.gcd(Tp, 256)
    for l in range(depth):
        wts = (w_in_b, w_tail, bgate[l], conv_w[l], conv_b[l][None, :], gn_b[l][None, :], gn_c[l][None, :],
               lb_all[l][None, :], w_out_b, ln1_g[l][None, :], ln1_b[l][None, :], w_up_b, w_down_b,
               ln2_g[l][None, :], ln2_b[l][None, :], alpha)

        attn_p = lambda proj: _moba_prompt(proj, bias_tab, Bp, Tp)
        xp_f, xp_b, (kp, vp, Cp, nP, mP, cP, SP) = _layer(xp_f, xp_b, Bp, Tp, l, wts, zeros_p, attn_p,
                                                          mlstm_chunk_p, mlstm_chunk_p)

        def attn_s(proj, l=l):
            sel = _sample_select(proj, kmean_pad[l], Bs, Ts, nb_past)
            ya = _moba_sample(sel, page_table, rel_bias, proj, cache_k, cache_v, Bs, Ts, past, l)
            kv = proj[:, W_A:3 * W_A].reshape(Bs, Ts, 2, H_A, HEAD_DIM)
            return ya, kv[:, :, 0], kv[:, :, 1]

        st_s = (state_b_C[l], state_b_n[l], state_b_m[l], state_b_conv[l], state_c_S[l])
        xs_f, xs_b, (ks, vs, Cs, nS, mS, cS, SS) = _layer(xs_f, xs_b, Bs, Ts, l, wts, st_s, attn_s, Ts, Ts)

        for lst, val in zip(outs, (kp, vp, ks, vs, Cp, nP, mP, cP, Cs, nS, mS, cS, SP, SS)):
            lst.append(val)

    return (xp_f.reshape(Bp, Tp, D), xs_f.reshape(Bs, Ts, D)) + tuple(jnp.stack(o) for o in outs)
```

```python
import functools
import math

import numpy as np
import jax
import jax.numpy as jnp
from jax import lax
from jax.experimental import pallas as pl
from jax.experimental.pallas import tpu as pltpu

F32 = jnp.float32
BF16 = jnp.bfloat16

HEAD_DIM = 128
H_A, H_B, H_C = 8, 4, 4
W_A, W_B, W_C = H_A * HEAD_DIM, H_B * HEAD_DIM, H_C * HEAD_DIM
MOBA_BLOCK = 256
MOBA_TOPK = 3
NUM_BUCKETS = 32
MAX_DISTANCE = 2048
CONV_W = 4
EPS = 1e-5
GATE_MASK = -1e30
NEG_BIG = -1e30
LOG2E = math.log2(math.e)
PAGE_SIZE = 128
LANES = 128
SUBLANES = 8
HGRN_SUB = 32
HGRN_SAFE_DECAY = 60.0
MOBA_GROUP = 4
QB = 4
VT_EXTRA = 16
VMEM_LIMIT = 56 * 1024 * 1024

N_MAIN = 3 * W_A + 4 * W_B + 4 * W_C
GATE_COL0 = 3 * W_A + 3 * W_B


def _t5_thresholds():
    max_exact = NUM_BUCKETS // 2
    n = np.arange(1, 4 * MAX_DISTANCE, dtype=np.float32)
    large = max_exact + (np.log(n / np.float32(max_exact)) / np.float32(math.log(MAX_DISTANCE / max_exact))
                         * np.float32(NUM_BUCKETS - max_exact)).astype(np.int32)
    large = np.minimum(large, NUM_BUCKETS - 1)
    thr = []
    for b in range(max_exact + 1, NUM_BUCKETS):
        thr.append(int(np.argmax(large >= b)) + 1)
    return tuple(thr)


T5_THRESHOLDS = _t5_thresholds()


def _cparams(sem):
    return pltpu.CompilerParams(dimension_semantics=sem, vmem_limit_bytes=VMEM_LIMIT)


def _dot(a, b):
    return jnp.dot(a, b, preferred_element_type=F32)


def _dot_nt(a, b):
    return lax.dot_general(a, b, (((1,), (1,)), ((), ())), preferred_element_type=F32)


def _dot_tn(a, b):
    return lax.dot_general(a, b, (((0,), (0,)), ((), ())), preferred_element_type=F32)


def _dot_hi(a, b):
    return jnp.dot(a, b, precision=lax.Precision.HIGHEST, preferred_element_type=F32)


def _dot_nt_hi(a, b):
    return lax.dot_general(a, b, (((1,), (1,)), ((), ())), precision=lax.Precision.HIGHEST,
                           preferred_element_type=F32)


def _sigmoid(x):
    return 1.0 / (1.0 + jnp.exp(-x))


def _layer_norm(z, g, b):
    mu = jnp.mean(z, axis=-1, keepdims=True)
    zc = z - mu
    var = jnp.mean(zc * zc, axis=-1, keepdims=True)
    return zc * lax.rsqrt(var + EPS) * g + b


def _matmul_kernel(x_ref, w_ref, o_ref):
    o_ref[...] = _dot(x_ref[...], w_ref[...])


def _matmul(x, w, layer, tm, tn, col_blk0, n_blk):
    M, K = x.shape
    N = n_blk * tn
    return pl.pallas_call(
        _matmul_kernel,
        grid=(M // tm, n_blk),
        in_specs=[pl.BlockSpec((tm, K), lambda i, j: (i, 0)),
                  pl.BlockSpec((None, K, tn), lambda i, j: (layer, 0, col_blk0 + j))],
        out_specs=pl.BlockSpec((tm, tn), lambda i, j: (i, j)),
        out_shape=jax.ShapeDtypeStruct((M, N), F32),
        compiler_params=_cparams(("arbitrary", "arbitrary")),
        name="in_proj",
    )(x, w)


def _t5_bias_from_dist(dist, rb_ref, h):
    n = jnp.maximum(dist, 0)
    large = jnp.full(n.shape, NUM_BUCKETS // 2, jnp.int32)
    for thr in T5_THRESHOLDS:
        large = large + (n >= thr).astype(jnp.int32)
    bucket = jnp.where(n < NUM_BUCKETS // 2, n, large)
    val = jnp.zeros(n.shape, F32)
    for b in range(NUM_BUCKETS):
        val = jnp.where(bucket == b, rb_ref[b, h], val)
    return val


def _bias_table_kernel(rb_ref, o_ref, *, nb):
    h = pl.program_id(0)
    blk = MOBA_BLOCK
    d_const = -(-(T5_THRESHOLDS[-1] - 1) // blk) + 1
    row = lax.broadcasted_iota(jnp.int32, (blk, blk), 0)
    col = lax.broadcasted_iota(jnp.int32, (blk, blk), 1)
    for e in range(2 * nb - 1):
        d = nb - 1 - e
        if d >= d_const:
            tile = jnp.full((blk, blk), rb_ref[NUM_BUCKETS - 1, h], F32)
        elif d < 0:
            tile = jnp.full((blk, blk), rb_ref[0, h], F32)
        else:
            tile = _t5_bias_from_dist(d * blk + col - row, rb_ref, h)
        o_ref[e * blk:(e + 1) * blk, :] = tile * LOG2E


def _bias_table(rel_bias, nb):
    ne = 2 * nb - 1
    return pl.pallas_call(
        functools.partial(_bias_table_kernel, nb=nb),
        grid=(H_A,),
        in_specs=[pl.BlockSpec(memory_space=pltpu.SMEM)],
        out_specs=pl.BlockSpec((None, ne * MOBA_BLOCK, MOBA_BLOCK), lambda h: (h, 0, 0)),
        out_shape=jax.ShapeDtypeStruct((H_A, ne * MOBA_BLOCK, MOBA_BLOCK), F32),
        compiler_params=_cparams(("arbitrary",)),
        name="t5_bias_table",
    )(rel_bias)


def _topk_select(gate, n_valid, n_cand):
    lane = lax.broadcasted_iota(jnp.int32, gate.shape, 1)
    gm = jnp.where(lane < n_valid, gate, GATE_MASK)
    rank = jnp.zeros(gate.shape, jnp.int32)
    for c in range(n_cand):
        gc = gm[:, c:c + 1]
        ahead = (gc > gm) | ((gc == gm) & (c < lane))
        rank = rank + ahead.astype(jnp.int32)
    return rank, lane


def _moba_prompt_kernel(q_ref, k_ref, v_ref, bias_ref, o_ref, kout_ref, vout_ref,
                        ka_s, vt_s, vtd_s, qa_s, kmean_s, sem, *, nb):
    b = pl.program_id(0)
    h = pl.program_id(1)
    i = pl.program_id(2)
    blk = MOBA_BLOCK
    G = MOBA_GROUP
    nbp = -(-nb // SUBLANES) * SUBLANES

    kv_copies = (pltpu.make_async_copy(k_ref, kout_ref.at[b, :, h, :], sem.at[0]),
                 pltpu.make_async_copy(v_ref, vout_ref.at[b, :, h, :], sem.at[1]))

    @pl.when(i == 0)
    def _():
        for c in kv_copies:
            c.start()
        kmean_s[...] = jnp.zeros(kmean_s.shape, F32)
        lane = lax.broadcasted_iota(jnp.int32, (blk, HEAD_DIM), 1)
        for n in range(nb):
            rows = slice((n % G) * blk, (n % G + 1) * blk)
            kf = k_ref[n * blk:(n + 1) * blk, :]
            ka_s[n // G, rows, 0:HEAD_DIM] = kf.astype(BF16)
            ka_s[n // G, rows, HEAD_DIM:2 * HEAD_DIM] = jnp.where(lane == n, NEG_BIG, 0.0).astype(BF16)
            ones_row = jnp.where(lax.broadcasted_iota(jnp.int32, (VT_EXTRA, blk), 0) == 0, 1.0, 0.0)
            vt = jnp.concatenate([v_ref[n * blk:(n + 1) * blk, :].T, ones_row], axis=0).astype(BF16)
            vt_s[n // G, :, rows] = vt
            vtd_s[n] = vt
            kmean_s[n:n + 1, :] = jnp.mean(kf, axis=0, keepdims=True)
        km = kmean_s[...]
        sub = lax.broadcasted_iota(jnp.int32, (nbp, blk), 0)
        pad = jnp.zeros((HEAD_DIM - nbp, blk), F32)
        for t in range(nb):
            q = q_ref[t * blk:(t + 1) * blk, :]
            if t > MOBA_TOPK:
                gm = jnp.where(sub < t, _dot_nt_hi(km, q), GATE_MASK)
                rank = jnp.zeros((nbp, blk), jnp.int32)
                for c in range(t):
                    gc = gm[c:c + 1, :]
                    rank = rank + ((gc > gm) | ((gc == gm) & (c < sub))).astype(jnp.int32)
                notsel = jnp.where((sub < t) & (rank < MOBA_TOPK), 0.0, 1.0)
            else:
                notsel = jnp.where(sub < t, 0.0, 1.0)
            qt = (q * (HEAD_DIM ** -0.5 * LOG2E)).T
            qa_s[t // QB, :, (t % QB) * blk:(t % QB + 1) * blk] = (
                jnp.concatenate([qt, notsel, pad], axis=0).astype(BF16))

    qaug = qa_s[i]
    rowk = lax.broadcasted_iota(jnp.int32, (blk, blk), 0)
    colq = lax.broadcasted_iota(jnp.int32, (blk, blk), 1)

    def tile(n_grp):
        m_parts, acc_parts = [], []
        for u in range(QB):
            iu = i * QB + u
            kd = ka_s[iu // G, pl.ds(pl.multiple_of((iu % G) * blk, blk), blk), 0:HEAD_DIM]
            sd = _dot(kd, qaug[0:HEAD_DIM, u * blk:(u + 1) * blk]) + bias_ref[(nb - 1) * blk:nb * blk, :]
            sd = jnp.where(colq >= rowk, sd, NEG_BIG)
            mu = jnp.max(sd, axis=0, keepdims=True)
            m_parts.append(mu)
            acc_parts.append(_dot(vtd_s[iu], jnp.exp2(sd - mu).astype(BF16)))
        m = jnp.concatenate(m_parts, axis=1)
        acc = jnp.concatenate(acc_parts, axis=1)
        for g in range(n_grp):
            bias = jnp.concatenate(
                [bias_ref[pl.ds(pl.multiple_of((nb - 1 - (i * QB + u) + G * g) * blk, blk), G * blk), :]
                 for u in range(QB)], axis=1)
            s = _dot(ka_s[g], qaug) + bias
            m_new = jnp.maximum(m, jnp.max(s, axis=0, keepdims=True))
            acc = jnp.exp2(m - m_new) * acc + _dot(vt_s[g], jnp.exp2(s - m_new).astype(BF16))
            m = m_new
        o_ref[...] = (acc[0:HEAD_DIM] / acc[HEAD_DIM:HEAD_DIM + 1]).T.astype(o_ref.dtype)

    steps_per_grp = G // QB
    for grp in range(nb // G):
        pl.when(i // steps_per_grp == grp)(functools.partial(tile, grp + 1))

    @pl.when(i == nb // QB - 1)
    def _():
        for c in kv_copies:
            c.wait()


def _moba_prompt(proj, bias_tab, B, T):
    nb = T // MOBA_BLOCK
    G = MOBA_GROUP
    assert nb % G == 0 and G % QB == 0 and nb >= 2 * G, "key blocks are processed in groups"
    nq = nb // QB
    kv_shape = jax.ShapeDtypeStruct((B, T, H_A, HEAD_DIM), F32)
    return pl.pallas_call(
        functools.partial(_moba_prompt_kernel, nb=nb),
        grid=(B, H_A, nq),
        in_specs=[pl.BlockSpec((T, HEAD_DIM), lambda b, h, i: (b, h)),
                  pl.BlockSpec((T, HEAD_DIM), lambda b, h, i: (b, H_A + h)),
                  pl.BlockSpec((T, HEAD_DIM), lambda b, h, i: (b, 2 * H_A + h)),
                  pl.BlockSpec((None, (2 * nb - 1) * MOBA_BLOCK, MOBA_BLOCK), lambda b, h, i: (h, 0, 0))],
        out_specs=[pl.BlockSpec((QB * MOBA_BLOCK, HEAD_DIM), lambda b, h, i: (b * nq + i, h)),
                   pl.BlockSpec(memory_space=pl.ANY), pl.BlockSpec(memory_space=pl.ANY)],
        out_shape=[jax.ShapeDtypeStruct((B * T, W_A), BF16), kv_shape, kv_shape],
        scratch_shapes=[pltpu.VMEM((nb // G, G * MOBA_BLOCK, 2 * HEAD_DIM), BF16),
                        pltpu.VMEM((nb // G, HEAD_DIM + VT_EXTRA, G * MOBA_BLOCK), BF16),
                        pltpu.VMEM((nb, HEAD_DIM + VT_EXTRA, MOBA_BLOCK), BF16),
                        pltpu.VMEM((nq, 2 * HEAD_DIM, QB * MOBA_BLOCK), BF16),
                        pltpu.VMEM((-(-nb // SUBLANES) * SUBLANES, HEAD_DIM), F32),
                        pltpu.SemaphoreType.DMA((2,))],
        compiler_params=_cparams(("arbitrary", "arbitrary", "arbitrary")),
        name="moba_prompt",
    )(proj, proj, proj, bias_tab)


def _mlstm_kernel(q_ref, k_ref, qp_ref, kp_ref, v_ref, og_ref, g_ref, conv0_ref, cw_ref, cb_ref, bg_ref,
                  gn_ref, c0_ref, n0_ref, m0_ref,
                  y_ref, cout_ref, nout_ref, mout_ref,
                  c_s, n_s, m_s, ext_s, *, L):
    c = pl.program_id(1)
    last = pl.num_programs(1) - 1

    @pl.when(c == 0)
    def _():
        c_s[...] = c0_ref[...]
        n_s[...] = n0_ref[...]
        m_s[...] = m0_ref[...]

    def conv_silu(u_ref, up_ref, col0):
        u = u_ref[...]
        tail = jnp.where(c == 0, conv0_ref[:, col0:col0 + W_B], up_ref[L - SUBLANES:L, :])
        ext_s[0:SUBLANES, :] = tail
        ext_s[SUBLANES:SUBLANES + L, :] = u
        acc = u * cw_ref[CONV_W - 1:CONV_W, col0:col0 + W_B] + cb_ref[:, col0:col0 + W_B]
        for j in range(1, CONV_W):
            xj = ext_s[SUBLANES - j:SUBLANES - j + L, :]
            acc = acc + xj * cw_ref[CONV_W - 1 - j:CONV_W - j, col0:col0 + W_B]
        return acc * _sigmoid(acc)

    qc = conv_silu(q_ref, qp_ref, 0)
    kc = conv_silu(k_ref, kp_ref, W_B) * (HEAD_DIM ** -0.5)
    v = v_ref[...]
    og = og_ref[...]

    g = g_ref[...] + bg_ref[...]
    lf = jnp.minimum(g, 0.0) - jnp.log(1.0 + jnp.exp(-jnp.abs(g)))
    row = lax.broadcasted_iota(jnp.int32, (L, L), 0)
    col = lax.broadcasted_iota(jnp.int32, (L, L), 1)
    causal = row >= col
    fcum = _dot_hi(causal.astype(F32), lf)

    for h in range(H_B):
        hs = slice(h * HEAD_DIM, (h + 1) * HEAD_DIM)
        fcol = fcum[:, H_B + h:H_B + h + 1]
        rcol = g[:, h:h + 1] - fcol
        rrow = jnp.sum(jnp.where(row == col, rcol, 0.0), axis=0, keepdims=True)
        dm = jnp.where(causal, fcol + rrow, -jnp.inf)
        mprev = m_s[h:h + 1, 0:1]
        gcol = fcol + mprev
        mt = jnp.maximum(gcol, jnp.max(dm, axis=1, keepdims=True))
        w = jnp.exp(dm - mt)
        wg = jnp.exp(gcol - mt)
        qh = qc[:, hs]
        kh = kc[:, hs]
        vh = v[:, hs]
        qhb = qh.astype(BF16)
        s = _dot_nt(qhb, kh.astype(BF16)) * w
        num = _dot(s.astype(BF16), vh.astype(BF16)) + wg * _dot(qhb, c_s[h].astype(BF16))
        den = jnp.sum(s, axis=1, keepdims=True) + wg * jnp.sum(qh * n_s[h:h + 1, :], axis=1, keepdims=True)
        hh = num / jnp.maximum(jnp.abs(den), jnp.exp(-mt))
        ml = mt[L - 1:L, :]
        wl = jnp.exp(fcol[L - 1:L, :] + rcol - ml)
        gl = jnp.exp(gcol[L - 1:L, :] - ml)
        kw = kh * wl
        c_s[h] = gl * c_s[h] + _dot_tn(kw.astype(BF16), vh.astype(BF16))
        n_s[h:h + 1, :] = gl * n_s[h:h + 1, :] + jnp.sum(kw, axis=0, keepdims=True)
        m_s[h:h + 1, :] = jnp.broadcast_to(ml, (1, LANES))
        hc = hh - jnp.mean(hh, axis=1, keepdims=True)
        yn = hc * lax.rsqrt(jnp.mean(hc * hc, axis=1, keepdims=True) + EPS) * gn_ref[:, hs]
        y_ref[:, hs] = (_sigmoid(og[:, hs]) * yn).astype(y_ref.dtype)

    @pl.when(c == last)
    def _():
        cout_ref[...] = c_s[...]
        nout_ref[...] = n_s[...]
        mout_ref[...] = m_s[...]


def _mlstm(proj, gates, conv0, conv_w, conv_b, bgate, gn_b, c0, n0, m0, B, T, L, out_dtype):
    nc = T // L
    q_blk = 3 * W_A // W_B
    k_blk, v_blk, o_blk = q_blk + 1, q_blk + 2, q_blk + 3

    def cur(colblk):
        return pl.BlockSpec((L, W_B), lambda b, c: (b * nc + c, colblk))

    def prev(colblk):
        return pl.BlockSpec((L, W_B), lambda b, c: (b * nc + jnp.maximum(c - 1, 0), colblk))

    full2 = lambda shape: pl.BlockSpec(shape, lambda b, c: (0, 0))
    per_b3 = lambda shape: pl.BlockSpec((None,) + shape, lambda b, c: (b, 0, 0))
    return pl.pallas_call(
        functools.partial(_mlstm_kernel, L=L),
        grid=(B, nc),
        in_specs=[cur(q_blk), cur(k_blk), prev(q_blk), prev(k_blk), cur(v_blk), cur(o_blk),
                  pl.BlockSpec((L, LANES), lambda b, c: (b * nc + c, 0)),
                  per_b3((SUBLANES, 2 * W_B)),
                  full2((CONV_W, 2 * W_B)), full2((1, 2 * W_B)), full2((1, LANES)), full2((1, W_B)),
                  pl.BlockSpec((None, H_B, HEAD_DIM, HEAD_DIM), lambda b, c: (b, 0, 0, 0)),
                  per_b3((SUBLANES, HEAD_DIM)), per_b3((SUBLANES, LANES))],
        out_specs=[pl.BlockSpec((L, W_B), lambda b, c: (b * nc + c, 0)),
                   pl.BlockSpec((None, H_B, HEAD_DIM, HEAD_DIM), lambda b, c: (b, 0, 0, 0)),
                   per_b3((SUBLANES, HEAD_DIM)), per_b3((SUBLANES, LANES))],
        out_shape=[jax.ShapeDtypeStruct((B * T, W_B), out_dtype),
                   jax.ShapeDtypeStruct((B, H_B, HEAD_DIM, HEAD_DIM), F32),
                   jax.ShapeDtypeStruct((B, SUBLANES, HEAD_DIM), F32),
                   jax.ShapeDtypeStruct((B, SUBLANES, LANES), F32)],
        scratch_shapes=[pltpu.VMEM((H_B, HEAD_DIM, HEAD_DIM), F32), pltpu.VMEM((SUBLANES, HEAD_DIM), F32),
                        pltpu.VMEM((SUBLANES, LANES), F32), pltpu.VMEM((L + SUBLANES, W_B), F32)],
        compiler_params=_cparams(("arbitrary", "arbitrary")),
        name="mlstm",
    )(proj, proj, proj, proj, proj, proj, gates, conv0, conv_w, conv_b, bgate, gn_b, c0, n0, m0)


def _hgrn_kernel(q_ref, f_ref, i_ref, g_ref, lb_ref, gn_ref, s0_ref, y_ref, sout_ref,
                 st_s, k_s, b_s, v_s, q_s, o_s, *, LC, LS):
    c = pl.program_id(1)
    last = pl.num_programs(1) - 1

    @pl.when(c == 0)
    def _():
        for h in range(H_C):
            st_s[h] = s0_ref[h].T

    lb = lb_ref[...]
    one_m_lb = 1.0 - lb
    row = lax.broadcasted_iota(jnp.int32, (LC, LC), 0)
    col = lax.broadcasted_iota(jnp.int32, (LC, LC), 1)
    same_sub = (row // LS) == (col // LS)
    intra = same_sub & (row >= col)
    row8 = lax.broadcasted_iota(jnp.int32, (SUBLANES, W_C), 0)

    fc = f_ref[...]
    qc = q_ref[...]
    logf = jnp.log(lb + one_m_lb * _sigmoid(fc))
    kk = one_m_lb * _sigmoid(-fc)
    qq = qc * _sigmoid(qc)
    vv = i_ref[...]
    b = _dot_hi(intra.astype(F32), logf)
    k_s[...] = kk
    b_s[...] = b
    v_s[...] = vv
    q_s[...] = qq

    safe = jnp.min(b) > -HGRN_SAFE_DECAY

    @pl.when(safe)
    def _():
        qe = qq * jnp.exp(b)
        kinv = kk * jnp.exp(-b)
        for h in range(H_C):
            hs = slice(h * HEAD_DIM, (h + 1) * HEAD_DIM)
            a = jnp.where(intra, _dot_nt(qe[:, hs].astype(BF16), kinv[:, hs].astype(BF16)), 0.0)
            o_s[:, hs] = _dot(a.astype(BF16), vv[:, hs].astype(BF16))

    @pl.when(jnp.logical_not(safe))
    def _():
        def exact_sub(sc, carry):
            r = pl.multiple_of(sc * LS, LS)
            for rb in range(LS // SUBLANES):
                t0 = rb * SUBLANES
                qb_ = q_s[pl.ds(r + t0, SUBLANES), :]
                bb_ = b_s[pl.ds(r + t0, SUBLANES), :]
                o_h = [jnp.zeros((SUBLANES, HEAD_DIM), F32) for _ in range(H_C)]
                for s in range(t0 + SUBLANES):
                    d = bb_ - b_s[pl.ds(r + s, 1), :]
                    if s >= t0:
                        d = jnp.where(row8 >= (s - t0), d, -jnp.inf)
                    tmp = qb_ * k_s[pl.ds(r + s, 1), :] * jnp.exp(d)
                    vs = v_s[pl.ds(r + s, 1), :]
                    for h in range(H_C):
                        hs = slice(h * HEAD_DIM, (h + 1) * HEAD_DIM)
                        o_h[h] = o_h[h] + jnp.sum(tmp[:, hs], axis=1, keepdims=True) * vs[:, hs]
                for h in range(H_C):
                    o_s[pl.ds(r + t0, SUBLANES), h * HEAD_DIM:(h + 1) * HEAD_DIM] = o_h[h]
            return carry

        lax.fori_loop(0, LC // LS, exact_sub, 0)

    def sub(sc, carry):
        r = pl.multiple_of(sc * LS, LS)
        bs = b_s[pl.ds(r, LS), :]
        qe = q_s[pl.ds(r, LS), :] * jnp.exp(bs)
        bl = bs[LS - 1:LS, :]
        ke = k_s[pl.ds(r, LS), :] * jnp.exp(bl - bs)
        dec = jnp.exp(bl)
        vs = v_s[pl.ds(r, LS), :]
        gg = g_ref[pl.ds(r, LS), :]
        for h in range(H_C):
            hs = slice(h * HEAD_DIM, (h + 1) * HEAD_DIM)
            st = st_s[h]
            o = _dot_nt(qe[:, hs].astype(BF16), st.astype(BF16)) + o_s[pl.ds(r, LS), hs]
            st_s[h] = dec[:, hs] * st + _dot_tn(vs[:, hs].astype(BF16), ke[:, hs].astype(BF16))
            yn = o * lax.rsqrt(jnp.mean(o * o, axis=1, keepdims=True) + EPS) * gn_ref[:, hs]
            gh = gg[:, hs]
            y_ref[pl.ds(r, LS), hs] = (yn * (gh * _sigmoid(gh))).astype(y_ref.dtype)
        return carry

    lax.fori_loop(0, LC // LS, sub, 0, unroll=True)

    @pl.when(c == last)
    def _():
        for h in range(H_C):
            sout_ref[h] = st_s[h].T


def _hgrn(proj, lb, gn_c, s0, B, T, LC, LS, out_dtype):
    nc = T // LC
    base = (3 * W_A + 4 * W_B) // W_C
    blk = lambda k: pl.BlockSpec((LC, W_C), lambda b, c: (b * nc + c, base + k))
    full2 = lambda shape: pl.BlockSpec(shape, lambda b, c: (0, 0))
    st_spec = pl.BlockSpec((None, H_C, HEAD_DIM, HEAD_DIM), lambda b, c: (b, 0, 0, 0))
    return pl.pallas_call(
        functools.partial(_hgrn_kernel, LC=LC, LS=LS),
        grid=(B, nc),
        in_specs=[blk(0), blk(1), blk(2), blk(3), full2((1, W_C)), full2((1, W_C)), st_spec],
        out_specs=[pl.BlockSpec((LC, W_C), lambda b, c: (b * nc + c, 0)), st_spec],
        out_shape=[jax.ShapeDtypeStruct((B * T, W_C), out_dtype),
                   jax.ShapeDtypeStruct((B, H_C, HEAD_DIM, HEAD_DIM), F32)],
        scratch_shapes=[pltpu.VMEM((H_C, HEAD_DIM, HEAD_DIM), F32)] + [pltpu.VMEM((LC, W_C), F32)] * 5,
        compiler_params=_cparams(("arbitrary", "arbitrary")),
        name="hgrn2",
    )(proj, proj, proj, proj, lb, gn_c, s0)


def _outproj_kernel(ya_ref, yb_ref, yc_ref, x_ref, w_ref, g_ref, b_ref, hf_ref, hb_ref, *, alpha):
    mix = (_dot(ya_ref[...].astype(BF16), w_ref[0:W_A, :])
           + _dot(yb_ref[...].astype(BF16), w_ref[W_A:W_A + W_B, :])
           + _dot(yc_ref[...].astype(BF16), w_ref[W_A + W_B:W_A + W_B + W_C, :]))
    h = _layer_norm(alpha * x_ref[...] + mix, g_ref[...], b_ref[...])
    hf_ref[...] = h
    hb_ref[...] = h.astype(BF16)


def _outproj(ya, yb, yc, x, w, layer, g, b, alpha, tm):
    M, D = x.shape
    rows = lambda n: pl.BlockSpec((tm, n), lambda i: (i, 0))
    full = lambda shape: pl.BlockSpec(shape, lambda i: (0, 0))
    w_spec = pl.BlockSpec((None,) + w.shape[1:], lambda i: (layer, 0, 0))
    return pl.pallas_call(
        functools.partial(_outproj_kernel, alpha=alpha),
        grid=(M // tm,),
        in_specs=[rows(W_A), rows(W_B), rows(W_C), rows(D), w_spec, full((1, D)), full((1, D))],
        out_specs=[rows(D), rows(D)],
        out_shape=[jax.ShapeDtypeStruct((M, D), F32), jax.ShapeDtypeStruct((M, D), BF16)],
        compiler_params=_cparams(("arbitrary",)),
        name="out_proj_ln1",
    )(ya, yb, yc, x, w, g, b)


def _mlp_kernel(hb_ref, hf_ref, wu_ref, wd_ref, g_ref, b_ref, of_ref, ob_ref, acc_s, *, alpha):
    f = pl.program_id(1)
    @pl.when(f == 0)
    def _():
        acc_s[...] = jnp.zeros(acc_s.shape, F32)

    u = jnp.maximum(_dot(hb_ref[...], wu_ref[...]), 0.0)
    acc_s[...] += _dot((u * u).astype(BF16), wd_ref[...])

    @pl.when(f == pl.num_programs(1) - 1)
    def _():
        o = _layer_norm(alpha * hf_ref[...] + acc_s[...], g_ref[...], b_ref[...])
        of_ref[...] = o
        ob_ref[...] = o.astype(BF16)


def _mlp(hb, hf, wu, wd, layer, g, b, alpha, tm, tf):
    M, D = hf.shape
    FF = wu.shape[2]
    rows = lambda: pl.BlockSpec((tm, D), lambda i, f: (i, 0))
    vec = lambda: pl.BlockSpec((1, D), lambda i, f: (0, 0))
    return pl.pallas_call(
        functools.partial(_mlp_kernel, alpha=alpha),
        grid=(M // tm, FF // tf),
        in_specs=[rows(), rows(), pl.BlockSpec((None, D, tf), lambda i, f: (layer, 0, f)),
                  pl.BlockSpec((None, tf, D), lambda i, f: (layer, f, 0)), vec(), vec()],
        out_specs=[rows(), rows()],
        out_shape=[jax.ShapeDtypeStruct((M, D), F32), jax.ShapeDtypeStruct((M, D), BF16)],
        scratch_shapes=[pltpu.VMEM((tm, D), F32)],
        compiler_params=_cparams(("arbitrary", "arbitrary")),
        name="mlp_ln2",
    )(hb, hf, wu, wd, g, b)


PAGES_PER_STEP = 16


def _kmean_kernel(pt_ref, *refs):
    page_refs, o_ref = refs[:PAGES_PER_STEP], refs[PAGES_PER_STEP]
    per_blk = MOBA_BLOCK // PAGE_SIZE
    for u in range(PAGES_PER_STEP // per_blk):
        tot = page_refs[per_blk * u][...].sum(axis=0)
        for e in range(1, per_blk):
            tot = tot + page_refs[per_blk * u + e][...].sum(axis=0)
        o_ref[u] = tot * (1.0 / MOBA_BLOCK)


def _cache_kmean(cache_k, page_table):
    depth = cache_k.shape[0]
    B, n_pages = page_table.shape
    nb = n_pages * PAGE_SIZE // MOBA_BLOCK
    steps = n_pages // PAGES_PER_STEP
    blocks_per_step = PAGES_PER_STEP * PAGE_SIZE // MOBA_BLOCK

    def page_spec(u):
        return pl.BlockSpec((None, None, PAGE_SIZE, H_A, HEAD_DIM),
                            lambda l, b, g, pt: (l, pt[b, g * PAGES_PER_STEP + u], 0, 0, 0))

    return pl.pallas_call(
        _kmean_kernel,
        grid_spec=pltpu.PrefetchScalarGridSpec(
            num_scalar_prefetch=1,
            grid=(depth, B, steps),
            in_specs=[page_spec(u) for u in range(PAGES_PER_STEP)],
            out_specs=pl.BlockSpec((None, None, blocks_per_step, H_A, HEAD_DIM),
                                   lambda l, b, g, pt: (l, b, g, 0, 0)),
        ),
        out_shape=jax.ShapeDtypeStruct((depth, B, nb, H_A, HEAD_DIM), F32),
        compiler_params=_cparams(("arbitrary", "arbitrary", "arbitrary")),
        name="cache_kmean",
    )(page_table, *([cache_k] * PAGES_PER_STEP))


def _sample_select_kernel(q_ref, km_ref, o_ref, *, nb):
    out = jnp.zeros(o_ref.shape, jnp.int32)
    lane_o = lax.broadcasted_iota(jnp.int32, o_ref.shape, 1)
    for h in range(H_A):
        hs = slice(h * HEAD_DIM, (h + 1) * HEAD_DIM)
        gate = _dot_nt_hi(q_ref[:, hs], km_ref[:, hs])
        rank, lane = _topk_select(gate, nb, nb)
        for slot in range(MOBA_TOPK):
            pick = (lane < nb) & (rank == slot)
            idx = jnp.sum(jnp.where(pick, lane, 0), axis=1, keepdims=True)
            out = jnp.where(lane_o == h * 4 + slot, idx, out)
    o_ref[...] = out


def _sample_select(proj_s, kmean_pad, B, T, nb):
    return pl.pallas_call(
        functools.partial(_sample_select_kernel, nb=nb),
        grid=(B,),
        in_specs=[pl.BlockSpec((T, W_A), lambda b: (b, 0)),
                  pl.BlockSpec((None, LANES, W_A), lambda b: (b, 0, 0))],
        out_specs=pl.BlockSpec((T, LANES), lambda b: (b, 0)),
        out_shape=jax.ShapeDtypeStruct((B * T, LANES), jnp.int32),
        compiler_params=_cparams(("arbitrary",)),
        name="moba_sample_select",
    )(proj_s, kmean_pad)


def _moba_sample_kernel(sel_ref, pt_ref, rb_ref, q_ref, kn_ref, vn_ref, ck_ref, cv_ref, o_ref,
                        kbuf, vbuf, sem, *, T, past, layer):
    b = pl.program_id(0)
    h = pl.program_id(1)
    nh = pl.num_programs(1)
    step = b * nh + h
    n_steps = pl.num_programs(0) * nh
    per_blk = MOBA_BLOCK // PAGE_SIZE

    def copies(bb, hh, buf, qi, slot, e):
        blk = sel_ref[bb * T + qi, hh * 4 + slot]
        page = pt_ref[bb, blk * per_blk + e]
        idx = qi * MOBA_TOPK + slot
        dst = pl.ds(e * PAGE_SIZE, PAGE_SIZE)
        return (pltpu.make_async_copy(ck_ref.at[layer, page, :, hh, :], kbuf.at[buf, idx, dst, :], sem.at[buf, 0]),
                pltpu.make_async_copy(cv_ref.at[layer, page, :, hh, :], vbuf.at[buf, idx, dst, :], sem.at[buf, 1]))

    def for_all_copies(bb, hh, buf, fn):
        for qi in range(T):
            for slot in range(MOBA_TOPK):
                for e in range(per_blk):
                    for c in copies(bb, hh, buf, qi, slot, e):
                        fn(c)

    cur = step % 2

    @pl.when(step == 0)
    def _():
        for_all_copies(b, h, 0, lambda c: c.start())

    @pl.when(step + 1 < n_steps)
    def _():
        nxt = step + 1
        for_all_copies(nxt // nh, nxt % nh, 1 - cur, lambda c: c.start())

    q = q_ref[...]
    qb = q.astype(BF16)
    scale = HEAD_DIM ** -0.5
    rowT = lax.broadcasted_iota(jnp.int32, (T, T), 0)
    colT = lax.broadcasted_iota(jnp.int32, (T, T), 1)
    s_own = _dot_nt(qb, kn_ref[...].astype(BF16)) * scale + _t5_bias_from_dist(rowT - colT, rb_ref, h)
    s_own = jnp.where(rowT >= colT, s_own, NEG_BIG)

    for_all_copies(b, h, cur, lambda c: c.wait())

    rowB = lax.broadcasted_iota(jnp.int32, (T, MOBA_BLOCK), 0)
    colB = lax.broadcasted_iota(jnp.int32, (T, MOBA_BLOCK), 1)
    row1 = lax.broadcasted_iota(jnp.int32, (T, 1), 0)
    s_slot = []
    for slot in range(MOBA_TOPK):
        s = jnp.zeros((T, MOBA_BLOCK), F32)
        blk_col = jnp.zeros((T, 1), jnp.int32)
        for qi in range(T):
            sq = _dot_nt(qb, kbuf[cur, qi * MOBA_TOPK + slot].astype(BF16))
            s = jnp.where(rowB == qi, sq, s)
            blk_col = jnp.where(row1 == qi, sel_ref[b * T + qi, h * 4 + slot], blk_col)
        dist = past + rowB - (blk_col * MOBA_BLOCK + colB)
        s_slot.append(s * scale + _t5_bias_from_dist(dist, rb_ref, h))

    m = jnp.max(s_own, axis=1, keepdims=True)
    for s in s_slot:
        m = jnp.maximum(m, jnp.max(s, axis=1, keepdims=True))
    p_own = jnp.exp(s_own - m)
    l = jnp.sum(p_own, axis=1, keepdims=True)
    acc = _dot(p_own.astype(BF16), vn_ref[...].astype(BF16))
    for slot in range(MOBA_TOPK):
        p = jnp.exp(s_slot[slot] - m)
        l = l + jnp.sum(p, axis=1, keepdims=True)
        for qi in range(T):
            pq = jnp.where(rowB == qi, p, 0.0).astype(BF16)
            acc = acc + _dot(pq, vbuf[cur, qi * MOBA_TOPK + slot].astype(BF16))
    o_ref[...] = acc / l


def _moba_sample(sel, page_table, rel_bias, proj_s, cache_k, cache_v, B, T, past, layer):
    n_slots = T * MOBA_TOPK
    return pl.pallas_call(
        functools.partial(_moba_sample_kernel, T=T, past=past, layer=layer),
        grid_spec=pltpu.PrefetchScalarGridSpec(
            num_scalar_prefetch=3,
            grid=(B, H_A),
            in_specs=[pl.BlockSpec((T, HEAD_DIM), lambda b, h, *_: (b, h)),
                      pl.BlockSpec((T, HEAD_DIM), lambda b, h, *_: (b, H_A + h)),
                      pl.BlockSpec((T, HEAD_DIM), lambda b, h, *_: (b, 2 * H_A + h)),
                      pl.BlockSpec(memory_space=pl.ANY),
                      pl.BlockSpec(memory_space=pl.ANY)],
            out_specs=pl.BlockSpec((T, HEAD_DIM), lambda b, h, *_: (b, h)),
            scratch_shapes=[pltpu.VMEM((2, n_slots, MOBA_BLOCK, HEAD_DIM), F32),
                            pltpu.VMEM((2, n_slots, MOBA_BLOCK, HEAD_DIM), F32),
                            pltpu.SemaphoreType.DMA((2, 2))],
        ),
        out_shape=jax.ShapeDtypeStruct((B * T, W_A), F32),
        compiler_params=_cparams(("arbitrary", "arbitrary")),
        name="moba_sample",
    )(sel, page_table, rel_bias, proj_s, proj_s, proj_s, cache_k, cache_v)


def _pad_rows(a, rows):
    return jnp.pad(a, ((0, 0), (0, rows - a.shape[1])) + ((0, 0),) * (a.ndim - 2))


def _mixer_states_in(c0, n0, m0, conv0):
    B = c0.shape[0]
    n0p = _pad_rows(n0, SUBLANES)
    m0p = _pad_rows(jnp.broadcast_to(m0[:, :, None], (B, H_B, LANES)), SUBLANES)
    conv0p = jnp.pad(conv0, ((0, 0), (SUBLANES - (CONV_W - 1), 0), (0, 0)))
    return c0, n0p, m0p, conv0p


def _tile(m, pref):
    return pref if m % pref == 0 else m


def _layer(x_f32, x_bf16, B, T, layer, wts, states, attn_fn, mlstm_chunk, hgrn_chunk):
    (w_main, w_gate, bgate, conv_w, conv_b, gn_b, gn_c, lb, w_out, ln1_g, ln1_b, w_up, w_down,
     ln2_g, ln2_b, alpha) = wts
    c0, n0, m0, conv0, s0 = states
    M = B * T
    tm = _tile(M, 2048)
    tn = 512
    proj = _matmul(x_bf16, w_main, layer, tm, tn, 0, N_MAIN // tn)
    gates = _matmul(x_bf16, w_gate, layer, tm, LANES, 0, 1)
    ya, k_new, v_new = attn_fn(proj)
    c0, n0p, m0p, conv0p = _mixer_states_in(c0, n0, m0, conv0)
    y_dtype = BF16 if T % 16 == 0 else F32
    yb, c_new, n_new, m_new = _mlstm(proj, gates, conv0p, conv_w, conv_b, bgate, gn_b, c0, n0p, m0p, B, T,
                                     mlstm_chunk, y_dtype)
    yc, s_new = _hgrn(proj, lb, gn_c, s0, B, T, hgrn_chunk, min(HGRN_SUB, hgrn_chunk), y_dtype)
    tm2 = _tile(M, 512)
    hf, hb = _outproj(ya, yb, yc, x_f32, w_out, layer, ln1_g, ln1_b, alpha, tm2)
    of, ob = _mlp(hb, hf, w_up, w_down, layer, ln2_g, ln2_b, alpha, tm2, 1024)
    conv_new = proj.reshape(B, T, N_MAIN)[:, T - (CONV_W - 1):, 3 * W_A:3 * W_A + 2 * W_B]
    return of, ob, (k_new, v_new, c_new, n_new[:, :H_B, :], m_new[:, :H_B, 0], conv_new, s_new)


def kernel(x_prompt, x_sample, cache_k, cache_v, page_table, state_b_C, state_b_n, state_b_m, state_b_conv,
           state_c_S, w_in, b_gate, conv_w, conv_b, gn_b, gn_c, lower_bounds, rel_bias, w_out, ln1_g, ln1_b,
           w_up, w_down, ln2_g, ln2_b):
    depth = w_in.shape[0]
    Bp, Tp, D = x_prompt.shape
    Bs, Ts, _ = x_sample.shape
    n_pages = page_table.shape[1]
    past = n_pages * PAGE_SIZE
    alpha = (2 * depth) ** 0.25

    sm = jax.nn.softmax(lower_bounds.astype(F32), axis=0)
    lb_all = jnp.cumsum(sm, axis=0) - sm[0]

    w_main = jnp.concatenate([w_in[:, :, :GATE_COL0], w_in[:, :, GATE_COL0 + 2 * H_B:]], axis=-1).astype(BF16)
    w_gate = jnp.pad(w_in[:, :, GATE_COL0:GATE_COL0 + 2 * H_B], ((0, 0), (0, 0), (0, LANES - 2 * H_B))).astype(BF16)
    bgate = jnp.pad(b_gate, ((0, 0), (0, LANES - 2 * H_B)))[:, None, :]
    w_out_b = w_out.astype(BF16)
    w_up_b = w_up.astype(BF16)
    w_down_b = w_down.astype(BF16)

    bias_tab = _bias_table(rel_bias, Tp // MOBA_BLOCK)
    nb_past = past // MOBA_BLOCK
    kmean_all = _cache_kmean(cache_k, page_table).reshape(depth, Bs, nb_past, W_A)
    kmean_pad = jnp.pad(kmean_all, ((0, 0), (0, 0), (0, LANES - nb_past), (0, 0)))

    zeros_p = (jnp.zeros((Bp, H_B, HEAD_DIM, HEAD_DIM), F32), jnp.zeros((Bp, H_B, HEAD_DIM), F32),
               jnp.zeros((Bp, H_B), F32), jnp.zeros((Bp, CONV_W - 1, 2 * W_B), F32),
               jnp.zeros((Bp, H_C, HEAD_DIM, HEAD_DIM), F32))

    xp_f = x_prompt.reshape(Bp * Tp, D)
    xs_f = x_sample.reshape(Bs * Ts, D)
    xp_b = xp_f.astype(BF16)
    xs_b = xs_f.astype(BF16)
    outs = [[] for _ in range(14)]
    mlstm_chunk_p = math.gcd(Tp, 256)
    for l in range(depth):
        wts = (w_main, w_gate, bgate[l], conv_w[l], conv_b[l][None, :], gn_b[l][None, :], gn_c[l][None, :],
               lb_all[l][None, :], w_out_b, ln1_g[l][None, :], ln1_b[l][None, :], w_up_b, w_down_b,
               ln2_g[l][None, :], ln2_b[l][None, :], alpha)

        attn_p = lambda proj: _moba_prompt(proj, bias_tab, Bp, Tp)
        xp_f, xp_b, (kp, vp, Cp, nP, mP, cP, SP) = _layer(xp_f, xp_b, Bp, Tp, l, wts, zeros_p, attn_p,
                                                          mlstm_chunk_p, mlstm_chunk_p)

        def attn_s(proj, l=l):
            sel = _sample_select(proj, kmean_pad[l], Bs, Ts, nb_past)
            ya = _moba_sample(sel, page_table, rel_bias, proj, cache_k, cache_v, Bs, Ts, past, l)
            kv = proj[:, W_A:3 * W_A].reshape(Bs, Ts, 2, H_A, HEAD_DIM)
            return ya, kv[:, :, 0], kv[:, :, 1]

        st_s = (state_b_C[l], state_b_n[l], state_b_m[l], state_b_conv[l], state_c_S[l])
        xs_f, xs_b, (ks, vs, Cs, nS, mS, cS, SS) = _layer(xs_f, xs_b, Bs, Ts, l, wts, st_s, attn_s, Ts, Ts)

        for lst, val in zip(outs, (kp, vp, ks, vs, Cp, nP, mP, cP, Cs, nS, mS, cS, SP, SS)):
            lst.append(val)

    return (xp_f.reshape(Bp, Tp, D), xs_f.reshape(Bs, Ts, D)) + tuple(jnp.stack(o) for o in outs)
```

```python
import functools
import math

import numpy as np
import jax
import jax.numpy as jnp
from jax import lax
from jax.experimental import pallas as pl
from jax.experimental.pallas import tpu as pltpu

F32 = jnp.float32
BF16 = jnp.bfloat16

HEAD_DIM = 128
H_A, H_B, H_C = 8, 4, 4
W_A, W_B, W_C = H_A * HEAD_DIM, H_B * HEAD_DIM, H_C * HEAD_DIM
MOBA_BLOCK = 256
MOBA_TOPK = 3
NUM_BUCKETS = 32
MAX_DISTANCE = 2048
CONV_W = 4
EPS = 1e-5
GATE_MASK = -1e30
NEG_BIG = -1e30
LOG2E = math.log2(math.e)
PAGE_SIZE = 128
LANES = 128
SUBLANES = 8
HGRN_SUB = 32
HGRN_SAFE_DECAY = 60.0
MOBA_GROUP = 4
QB = 4
VT_EXTRA = 16
VMEM_LIMIT = 56 * 1024 * 1024

N_MAIN = 3 * W_A + 4 * W_B + 4 * W_C
GATE_COL0 = 3 * W_A + 3 * W_B


def _t5_thresholds():
    max_exact = NUM_BUCKETS // 2
    n = np.arange(1, 4 * MAX_DISTANCE, dtype=np.float32)
    large = max_exact + (np.log(n / np.float32(max_exact)) / np.float32(math.log(MAX_DISTANCE / max_exact))
                         * np.float32(NUM_BUCKETS - max_exact)).astype(np.int32)
    large = np.minimum(large, NUM_BUCKETS - 1)
    thr = []
    for b in range(max_exact + 1, NUM_BUCKETS):
        thr.append(int(np.argmax(large >= b)) + 1)
    return tuple(thr)


T5_THRESHOLDS = _t5_thresholds()


def _cparams(sem):
    return pltpu.CompilerParams(dimension_semantics=sem, vmem_limit_bytes=VMEM_LIMIT)


def _dot(a, b):
    return jnp.dot(a, b, preferred_element_type=F32)


def _dot_nt(a, b):
    return lax.dot_general(a, b, (((1,), (1,)), ((), ())), preferred_element_type=F32)


def _dot_tn(a, b):
    return lax.dot_general(a, b, (((0,), (0,)), ((), ())), preferred_element_type=F32)


def _dot_hi(a, b):
    return jnp.dot(a, b, precision=lax.Precision.HIGHEST, preferred_element_type=F32)


def _dot_nt_hi(a, b):
    return lax.dot_general(a, b, (((1,), (1,)), ((), ())), precision=lax.Precision.HIGHEST,
                           preferred_element_type=F32)


def _sigmoid(x):
    return 1.0 / (1.0 + jnp.exp(-x))


def _layer_norm(z, g, b):
    mu = jnp.mean(z, axis=-1, keepdims=True)
    zc = z - mu
    var = jnp.mean(zc * zc, axis=-1, keepdims=True)
    return zc * lax.rsqrt(var + EPS) * g + b


def _matmul_kernel(x_ref, w_ref, o_ref):
    o_ref[...] = _dot(x_ref[...], w_ref[...])


def _matmul(x, w, layer, tm, tn, col_blk0, n_blk):
    M, K = x.shape
    N = n_blk * tn
    return pl.pallas_call(
        _matmul_kernel,
        grid=(M // tm, n_blk),
        in_specs=[pl.BlockSpec((tm, K), lambda i, j: (i, 0)),
                  pl.BlockSpec((None, K, tn), lambda i, j: (layer, 0, col_blk0 + j))],
        out_specs=pl.BlockSpec((tm, tn), lambda i, j: (i, j)),
        out_shape=jax.ShapeDtypeStruct((M, N), F32),
        compiler_params=_cparams(("arbitrary", "arbitrary")),
        name="in_proj",
    )(x, w)


def _t5_bias_from_dist(dist, rb_ref, h):
    n = jnp.maximum(dist, 0)
    large = jnp.full(n.shape, NUM_BUCKETS // 2, jnp.int32)
    for thr in T5_THRESHOLDS:
        large = large + (n >= thr).astype(jnp.int32)
    bucket = jnp.where(n < NUM_BUCKETS // 2, n, large)
    val = jnp.zeros(n.shape, F32)
    for b in range(NUM_BUCKETS):
        val = jnp.where(bucket == b, rb_ref[b, h], val)
    return val


def _bias_table_kernel(rb_ref, o_ref, *, nb):
    h = pl.program_id(0)
    blk = MOBA_BLOCK
    d_const = -(-(T5_THRESHOLDS[-1] - 1) // blk) + 1
    row = lax.broadcasted_iota(jnp.int32, (blk, blk), 0)
    col = lax.broadcasted_iota(jnp.int32, (blk, blk), 1)
    for e in range(2 * nb - 1):
        d = nb - 1 - e
        if d >= d_const:
            tile = jnp.full((blk, blk), rb_ref[NUM_BUCKETS - 1, h], F32)
        elif d < 0:
            tile = jnp.full((blk, blk), rb_ref[0, h], F32)
        else:
            tile = _t5_bias_from_dist(d * blk + col - row, rb_ref, h)
        o_ref[e * blk:(e + 1) * blk, :] = tile * LOG2E


def _bias_table(rel_bias, nb):
    ne = 2 * nb - 1
    return pl.pallas_call(
        functools.partial(_bias_table_kernel, nb=nb),
        grid=(H_A,),
        in_specs=[pl.BlockSpec(memory_space=pltpu.SMEM)],
        out_specs=pl.BlockSpec((None, ne * MOBA_BLOCK, MOBA_BLOCK), lambda h: (h, 0, 0)),
        out_shape=jax.ShapeDtypeStruct((H_A, ne * MOBA_BLOCK, MOBA_BLOCK), F32),
        compiler_params=_cparams(("arbitrary",)),
        name="t5_bias_table",
    )(rel_bias)


def _topk_select(gate, n_valid, n_cand):
    lane = lax.broadcasted_iota(jnp.int32, gate.shape, 1)
    gm = jnp.where(lane < n_valid, gate, GATE_MASK)
    rank = jnp.zeros(gate.shape, jnp.int32)
    for c in range(n_cand):
        gc = gm[:, c:c + 1]
        ahead = (gc > gm) | ((gc == gm) & (c < lane))
        rank = rank + ahead.astype(jnp.int32)
    return rank, lane


def _moba_prompt_kernel(q_ref, k_ref, v_ref, bias_ref, o_ref, kout_ref, vout_ref,
                        ka_s, vt_s, vtd_s, qa_s, kmean_s, sem, *, nb):
    b = pl.program_id(0)
    h = pl.program_id(1)
    i = pl.program_id(2)
    blk = MOBA_BLOCK
    G = MOBA_GROUP
    nbp = -(-nb // SUBLANES) * SUBLANES

    kv_copies = (pltpu.make_async_copy(k_ref, kout_ref.at[b, :, h, :], sem.at[0]),
                 pltpu.make_async_copy(v_ref, vout_ref.at[b, :, h, :], sem.at[1]))

    @pl.when(i == 0)
    def _():
        for c in kv_copies:
            c.start()
        kmean_s[...] = jnp.zeros(kmean_s.shape, F32)
        lane = lax.broadcasted_iota(jnp.int32, (blk, HEAD_DIM), 1)
        for n in range(nb):
            rows = slice((n % G) * blk, (n % G + 1) * blk)
            kf = k_ref[n * blk:(n + 1) * blk, :]
            ka_s[n // G, rows, 0:HEAD_DIM] = kf.astype(BF16)
            ka_s[n // G, rows, HEAD_DIM:2 * HEAD_DIM] = jnp.where(lane == n, NEG_BIG, 0.0).astype(BF16)
            ones_row = jnp.where(lax.broadcasted_iota(jnp.int32, (VT_EXTRA, blk), 0) == 0, 1.0, 0.0)
            vt = jnp.concatenate([v_ref[n * blk:(n + 1) * blk, :].T, ones_row], axis=0).astype(BF16)
            vt_s[n // G, :, rows] = vt
            vtd_s[n] = vt
            kmean_s[n:n + 1, :] = jnp.mean(kf, axis=0, keepdims=True)
        km = kmean_s[...]
        sub = lax.broadcasted_iota(jnp.int32, (nbp, blk), 0)
        pad = jnp.zeros((HEAD_DIM - nbp, blk), F32)
        for t in range(nb):
            q = q_ref[t * blk:(t + 1) * blk, :]
            if t > MOBA_TOPK:
                gm = jnp.where(sub < t, _dot_nt_hi(km, q), GATE_MASK)
                rank = jnp.zeros((nbp, blk), jnp.int32)
                for c in range(t):
                    gc = gm[c:c + 1, :]
                    rank = rank + ((gc > gm) | ((gc == gm) & (c < sub))).astype(jnp.int32)
                notsel = jnp.where((sub < t) & (rank < MOBA_TOPK), 0.0, 1.0)
            else:
                notsel = jnp.where(sub < t, 0.0, 1.0)
            qt = (q * (HEAD_DIM ** -0.5 * LOG2E)).T
            qa_s[t // QB, :, (t % QB) * blk:(t % QB + 1) * blk] = (
                jnp.concatenate([qt, notsel, pad], axis=0).astype(BF16))

    qaug = qa_s[i]
    rowk = lax.broadcasted_iota(jnp.int32, (blk, blk), 0)
    colq = lax.broadcasted_iota(jnp.int32, (blk, blk), 1)

    def tile(n_grp):
        m_parts, acc_parts = [], []
        for u in range(QB):
            iu = i * QB + u
            kd = ka_s[iu // G, pl.ds(pl.multiple_of((iu % G) * blk, blk), blk), 0:HEAD_DIM]
            sd = _dot(kd, qaug[0:HEAD_DIM, u * blk:(u + 1) * blk]) + bias_ref[(nb - 1) * blk:nb * blk, :]
            sd = jnp.where(colq >= rowk, sd, NEG_BIG)
            mu = jnp.max(sd, axis=0, keepdims=True)
            m_parts.append(mu)
            acc_parts.append(_dot(vtd_s[iu], jnp.exp2(sd - mu).astype(BF16)))
        m = jnp.concatenate(m_parts, axis=1)
        acc = jnp.concatenate(acc_parts, axis=1)
        for g in range(n_grp):
            bias = jnp.concatenate(
                [bias_ref[pl.ds(pl.multiple_of((nb - 1 - (i * QB + u) + G * g) * blk, blk), G * blk), :]
                 for u in range(QB)], axis=1)
            s = _dot(ka_s[g], qaug) + bias
            m_new = jnp.maximum(m, jnp.max(s, axis=0, keepdims=True))
            acc = jnp.exp2(m - m_new) * acc + _dot(vt_s[g], jnp.exp2(s - m_new).astype(BF16))
            m = m_new
        o_ref[...] = (acc[0:HEAD_DIM] / acc[HEAD_DIM:HEAD_DIM + 1]).T.astype(o_ref.dtype)

    steps_per_grp = G // QB
    for grp in range(nb // G):
        pl.when(i // steps_per_grp == grp)(functools.partial(tile, grp + 1))

    @pl.when(i == nb // QB - 1)
    def _():
        for c in kv_copies:
            c.wait()


def _moba_prompt(proj, bias_tab, B, T):
    nb = T // MOBA_BLOCK
    G = MOBA_GROUP
    assert nb % G == 0 and G % QB == 0 and nb >= 2 * G, "key blocks are processed in groups"
    nq = nb // QB
    kv_shape = jax.ShapeDtypeStruct((B, T, H_A, HEAD_DIM), F32)
    return pl.pallas_call(
        functools.partial(_moba_prompt_kernel, nb=nb),
        grid=(B, H_A, nq),
        in_specs=[pl.BlockSpec((T, HEAD_DIM), lambda b, h, i: (b, h)),
                  pl.BlockSpec((T, HEAD_DIM), lambda b, h, i: (b, H_A + h)),
                  pl.BlockSpec((T, HEAD_DIM), lambda b, h, i: (b, 2 * H_A + h)),
                  pl.BlockSpec((None, (2 * nb - 1) * MOBA_BLOCK, MOBA_BLOCK), lambda b, h, i: (h, 0, 0))],
        out_specs=[pl.BlockSpec((QB * MOBA_BLOCK, HEAD_DIM), lambda b, h, i: (b * nq + i, h)),
                   pl.BlockSpec(memory_space=pl.ANY), pl.BlockSpec(memory_space=pl.ANY)],
        out_shape=[jax.ShapeDtypeStruct((B * T, W_A), BF16), kv_shape, kv_shape],
        scratch_shapes=[pltpu.VMEM((nb // G, G * MOBA_BLOCK, 2 * HEAD_DIM), BF16),
                        pltpu.VMEM((nb // G, HEAD_DIM + VT_EXTRA, G * MOBA_BLOCK), BF16),
                        pltpu.VMEM((nb, HEAD_DIM + VT_EXTRA, MOBA_BLOCK), BF16),
                        pltpu.VMEM((nq, 2 * HEAD_DIM, QB * MOBA_BLOCK), BF16),
                        pltpu.VMEM((-(-nb // SUBLANES) * SUBLANES, HEAD_DIM), F32),
                        pltpu.SemaphoreType.DMA((2,))],
        compiler_params=_cparams(("arbitrary", "arbitrary", "arbitrary")),
        name="moba_prompt",
    )(proj, proj, proj, bias_tab)


def _mlstm_kernel(q_ref, k_ref, qp_ref, kp_ref, v_ref, og_ref, g_ref, conv0_ref, cw_ref, cb_ref, bg_ref,
                  gn_ref, c0_ref, n0_ref, m0_ref,
                  y_ref, cout_ref, nout_ref, mout_ref,
                  c_s, n_s, m_s, ext_s, *, L):
    c = pl.program_id(1)
    last = pl.num_programs(1) - 1

    @pl.when(c == 0)
    def _():
        c_s[...] = c0_ref[...]
        n_s[...] = n0_ref[...]
        m_s[...] = m0_ref[...]

    def conv_silu(u_ref, up_ref, col0):
        u = u_ref[...]
        tail = jnp.where(c == 0, conv0_ref[:, col0:col0 + W_B], up_ref[L - SUBLANES:L, :])
        ext_s[0:SUBLANES, :] = tail
        ext_s[SUBLANES:SUBLANES + L, :] = u
        acc = u * cw_ref[CONV_W - 1:CONV_W, col0:col0 + W_B] + cb_ref[:, col0:col0 + W_B]
        for j in range(1, CONV_W):
            xj = ext_s[SUBLANES - j:SUBLANES - j + L, :]
            acc = acc + xj * cw_ref[CONV_W - 1 - j:CONV_W - j, col0:col0 + W_B]
        return acc * _sigmoid(acc)

    qc = conv_silu(q_ref, qp_ref, 0)
    kc = conv_silu(k_ref, kp_ref, W_B) * (HEAD_DIM ** -0.5)
    v = v_ref[...]
    og = og_ref[...]

    g = g_ref[...] + bg_ref[...]
    lf = jnp.minimum(g, 0.0) - jnp.log(1.0 + jnp.exp(-jnp.abs(g)))
    row = lax.broadcasted_iota(jnp.int32, (L, L), 0)
    col = lax.broadcasted_iota(jnp.int32, (L, L), 1)
    causal = row >= col
    fcum = _dot_hi(causal.astype(F32), lf)

    for h in range(H_B):
        hs = slice(h * HEAD_DIM, (h + 1) * HEAD_DIM)
        fcol = fcum[:, H_B + h:H_B + h + 1]
        rcol = g[:, h:h + 1] - fcol
        rrow = jnp.sum(jnp.where(row == col, rcol, 0.0), axis=0, keepdims=True)
        dm = jnp.where(causal, fcol + rrow, -jnp.inf)
        mprev = m_s[h:h + 1, 0:1]
        gcol = fcol + mprev
        mt = jnp.maximum(gcol, jnp.max(dm, axis=1, keepdims=True))
        w = jnp.exp(dm - mt)
        wg = jnp.exp(gcol - mt)
        qh = qc[:, hs]
        kh = kc[:, hs]
        vh = v[:, hs]
        qhb = qh.astype(BF16)
        s = _dot_nt(qhb, kh.astype(BF16)) * w
        num = _dot(s.astype(BF16), vh.astype(BF16)) + wg * _dot(qhb, c_s[h].astype(BF16))
        den = jnp.sum(s, axis=1, keepdims=True) + wg * jnp.sum(qh * n_s[h:h + 1, :], axis=1, keepdims=True)
        hh = num / jnp.maximum(jnp.abs(den), jnp.exp(-mt))
        ml = mt[L - 1:L, :]
        wl = jnp.exp(fcol[L - 1:L, :] + rcol - ml)
        gl = jnp.exp(gcol[L - 1:L, :] - ml)
        kw = kh * wl
        c_s[h] = gl * c_s[h] + _dot_tn(kw.astype(BF16), vh.astype(BF16))
        n_s[h:h + 1, :] = gl * n_s[h:h + 1, :] + jnp.sum(kw, axis=0, keepdims=True)
        m_s[h:h + 1, :] = jnp.broadcast_to(ml, (1, LANES))
        hc = hh - jnp.mean(hh, axis=1, keepdims=True)
        yn = hc * lax.rsqrt(jnp.mean(hc * hc, axis=1, keepdims=True) + EPS) * gn_ref[:, hs]
        y_ref[:, hs] = (_sigmoid(og[:, hs]) * yn).astype(y_ref.dtype)

    @pl.when(c == last)
    def _():
        cout_ref[...] = c_s[...]
        nout_ref[...] = n_s[...]
        mout_ref[...] = m_s[...]


def _mlstm(proj, gates, conv0, conv_w, conv_b, bgate, gn_b, c0, n0, m0, B, T, L, out_dtype):
    nc = T // L
    q_blk = 3 * W_A // W_B
    k_blk, v_blk, o_blk = q_blk + 1, q_blk + 2, q_blk + 3

    def cur(colblk):
        return pl.BlockSpec((L, W_B), lambda b, c: (b * nc + c, colblk))

    def prev(colblk):
        return pl.BlockSpec((L, W_B), lambda b, c: (b * nc + jnp.maximum(c - 1, 0), colblk))

    full2 = lambda shape: pl.BlockSpec(shape, lambda b, c: (0, 0))
    per_b3 = lambda shape: pl.BlockSpec((None,) + shape, lambda b, c: (b, 0, 0))
    return pl.pallas_call(
        functools.partial(_mlstm_kernel, L=L),
        grid=(B, nc),
        in_specs=[cur(q_blk), cur(k_blk), prev(q_blk), prev(k_blk), cur(v_blk), cur(o_blk),
                  pl.BlockSpec((L, LANES), lambda b, c: (b * nc + c, 0)),
                  per_b3((SUBLANES, 2 * W_B)),
                  full2((CONV_W, 2 * W_B)), full2((1, 2 * W_B)), full2((1, LANES)), full2((1, W_B)),
                  pl.BlockSpec((None, H_B, HEAD_DIM, HEAD_DIM), lambda b, c: (b, 0, 0, 0)),
                  per_b3((SUBLANES, HEAD_DIM)), per_b3((SUBLANES, LANES))],
        out_specs=[pl.BlockSpec((L, W_B), lambda b, c: (b * nc + c, 0)),
                   pl.BlockSpec((None, H_B, HEAD_DIM, HEAD_DIM), lambda b, c: (b, 0, 0, 0)),
                   per_b3((SUBLANES, HEAD_DIM)), per_b3((SUBLANES, LANES))],
        out_shape=[jax.ShapeDtypeStruct((B * T, W_B), out_dtype),
                   jax.ShapeDtypeStruct((B, H_B, HEAD_DIM, HEAD_DIM), F32),
                   jax.ShapeDtypeStruct((B, SUBLANES, HEAD_DIM), F32),
                   jax.ShapeDtypeStruct((B, SUBLANES, LANES), F32)],
        scratch_shapes=[pltpu.VMEM((H_B, HEAD_DIM, HEAD_DIM), F32), pltpu.VMEM((SUBLANES, HEAD_DIM), F32),
                        pltpu.VMEM((SUBLANES, LANES), F32), pltpu.VMEM((L + SUBLANES, W_B), F32)],
        compiler_params=_cparams(("arbitrary", "arbitrary")),
        name="mlstm",
    )(proj, proj, proj, proj, proj, proj, gates, conv0, conv_w, conv_b, bgate, gn_b, c0, n0, m0)


def _hgrn_kernel(q_ref, f_ref, i_ref, g_ref, lb_ref, gn_ref, s0_ref, y_ref, sout_ref,
                 st_s, k_s, b_s, v_s, q_s, o_s, *, LC, LS):
    c = pl.program_id(1)
    last = pl.num_programs(1) - 1

    @pl.when(c == 0)
    def _():
        for h in range(H_C):
            st_s[h] = s0_ref[h].T

    lb = lb_ref[...]
    one_m_lb = 1.0 - lb
    row = lax.broadcasted_iota(jnp.int32, (LC, LC), 0)
    col = lax.broadcasted_iota(jnp.int32, (LC, LC), 1)
    same_sub = (row // LS) == (col // LS)
    intra = same_sub & (row >= col)
    row8 = lax.broadcasted_iota(jnp.int32, (SUBLANES, W_C), 0)

    fc = f_ref[...]
    qc = q_ref[...]
    logf = jnp.log(lb + one_m_lb * _sigmoid(fc))
    kk = one_m_lb * _sigmoid(-fc)
    qq = qc * _sigmoid(qc)
    vv = i_ref[...]
    b = _dot_hi(intra.astype(F32), logf)
    k_s[...] = kk
    b_s[...] = b
    v_s[...] = vv
    q_s[...] = qq

    safe = jnp.min(b) > -HGRN_SAFE_DECAY

    @pl.when(safe)
    def _():
        qe = qq * jnp.exp(b)
        kinv = kk * jnp.exp(-b)
        for h in range(H_C):
            hs = slice(h * HEAD_DIM, (h + 1) * HEAD_DIM)
            a = jnp.where(intra, _dot_nt(qe[:, hs].astype(BF16), kinv[:, hs].astype(BF16)), 0.0)
            o_s[:, hs] = _dot(a.astype(BF16), vv[:, hs].astype(BF16))

    @pl.when(jnp.logical_not(safe))
    def _():
        def exact_sub(sc, carry):
            r = pl.multiple_of(sc * LS, LS)
            for rb in range(LS // SUBLANES):
                t0 = rb * SUBLANES
                qb_ = q_s[pl.ds(r + t0, SUBLANES), :]
                bb_ = b_s[pl.ds(r + t0, SUBLANES), :]
                o_h = [jnp.zeros((SUBLANES, HEAD_DIM), F32) for _ in range(H_C)]
                for s in range(t0 + SUBLANES):
                    d = bb_ - b_s[pl.ds(r + s, 1), :]
                    if s >= t0:
                        d = jnp.where(row8 >= (s - t0), d, -jnp.inf)
                    tmp = qb_ * k_s[pl.ds(r + s, 1), :] * jnp.exp(d)
                    vs = v_s[pl.ds(r + s, 1), :]
                    for h in range(H_C):
                        hs = slice(h * HEAD_DIM, (h + 1) * HEAD_DIM)
                        o_h[h] = o_h[h] + jnp.sum(tmp[:, hs], axis=1, keepdims=True) * vs[:, hs]
                for h in range(H_C):
                    o_s[pl.ds(r + t0, SUBLANES), h * HEAD_DIM:(h + 1) * HEAD_DIM] = o_h[h]
            return carry

        lax.fori_loop(0, LC // LS, exact_sub, 0)

    def sub(sc, carry):
        r = pl.multiple_of(sc * LS, LS)
        bs = b_s[pl.ds(r, LS), :]
        qe = q_s[pl.ds(r, LS), :] * jnp.exp(bs)
        bl = bs[LS - 1:LS, :]
        ke = k_s[pl.ds(r, LS), :] * jnp.exp(bl - bs)
        dec = jnp.exp(bl)
        vs = v_s[pl.ds(r, LS), :]
        gg = g_ref[pl.ds(r, LS), :]
        for h in range(H_C):
            hs = slice(h * HEAD_DIM, (h + 1) * HEAD_DIM)
            st = st_s[h]
            o = _dot_nt(qe[:, hs].astype(BF16), st.astype(BF16)) + o_s[pl.ds(r, LS), hs]
            st_s[h] = dec[:, hs] * st + _dot_tn(vs[:, hs].astype(BF16), ke[:, hs].astype(BF16))
            yn = o * lax.rsqrt(jnp.mean(o * o, axis=1, keepdims=True) + EPS) * gn_ref[:, hs]
            gh = gg[:, hs]
            y_ref[pl.ds(r, LS), hs] = (yn * (gh * _sigmoid(gh))).astype(y_ref.dtype)
        return carry

    lax.fori_loop(0, LC // LS, sub, 0, unroll=True)

    @pl.when(c == last)
    def _():
        for h in range(H_C):
            sout_ref[h] = st_s[h].T


def _hgrn(proj, lb, gn_c, s0, B, T, LC, LS, out_dtype):
    nc = T // LC
    base = (3 * W_A + 4 * W_B) // W_C
    blk = lambda k: pl.BlockSpec((LC, W_C), lambda b, c: (b * nc + c, base + k))
    full2 = lambda shape: pl.BlockSpec(shape, lambda b, c: (0, 0))
    st_spec = pl.BlockSpec((None, H_C, HEAD_DIM, HEAD_DIM), lambda b, c: (b, 0, 0, 0))
    return pl.pallas_call(
        functools.partial(_hgrn_kernel, LC=LC, LS=LS),
        grid=(B, nc),
        in_specs=[blk(0), blk(1), blk(2), blk(3), full2((1, W_C)), full2((1, W_C)), st_spec],
        out_specs=[pl.BlockSpec((LC, W_C), lambda b, c: (b * nc + c, 0)), st_spec],
        out_shape=[jax.ShapeDtypeStruct((B * T, W_C), out_dtype),
                   jax.ShapeDtypeStruct((B, H_C, HEAD_DIM, HEAD_DIM), F32)],
        scratch_shapes=[pltpu.VMEM((H_C, HEAD_DIM, HEAD_DIM), F32)] + [pltpu.VMEM((LC, W_C), F32)] * 5,
        compiler_params=_cparams(("arbitrary", "arbitrary")),
        name="hgrn2",
    )(proj, proj, proj, proj, lb, gn_c, s0)


def _outproj_kernel(ya_ref, yb_ref, yc_ref, x_ref, w_ref, g_ref, b_ref, hf_ref, hb_ref, *, alpha):
    half = x_ref.shape[0] // 2
    for r in (slice(0, half), slice(half, 2 * half)):
        cat = jnp.concatenate([ya_ref[r, :].astype(BF16), yb_ref[r, :].astype(BF16), yc_ref[r, :].astype(BF16)],
                              axis=1)
        h = _layer_norm(alpha * x_ref[r, :] + _dot(cat, w_ref[...]), g_ref[...], b_ref[...])
        hf_ref[r, :] = h
        hb_ref[r, :] = h.astype(BF16)


def _outproj(ya, yb, yc, x, w, layer, g, b, alpha, tm):
    M, D = x.shape
    rows = lambda n: pl.BlockSpec((tm, n), lambda i: (i, 0))
    full = lambda shape: pl.BlockSpec(shape, lambda i: (0, 0))
    w_spec = pl.BlockSpec((None,) + w.shape[1:], lambda i: (layer, 0, 0))
    return pl.pallas_call(
        functools.partial(_outproj_kernel, alpha=alpha),
        grid=(M // tm,),
        in_specs=[rows(W_A), rows(W_B), rows(W_C), rows(D), w_spec, full((1, D)), full((1, D))],
        out_specs=[rows(D), rows(D)],
        out_shape=[jax.ShapeDtypeStruct((M, D), F32), jax.ShapeDtypeStruct((M, D), BF16)],
        compiler_params=_cparams(("arbitrary",)),
        name="out_proj_ln1",
    )(ya, yb, yc, x, w, g, b)


def _mlp_body(hb_ref, hf_ref, wu_ref, wd_ref, g_ref, b_ref, of_ref, ob_ref, acc_s, alpha, side_work=None):
    f = pl.program_id(1)

    @pl.when(f == 0)
    def _():
        acc_s[...] = jnp.zeros(acc_s.shape, F32)

    if side_work is not None:
        side_work()
    hb = hf_ref[...].astype(BF16) if hb_ref is None else hb_ref[...]
    u = jnp.maximum(_dot(hb, wu_ref[...]), 0.0)
    acc_s[...] += _dot((u * u).astype(BF16), wd_ref[...])

    @pl.when(f == pl.num_programs(1) - 1)
    def _():
        o = _layer_norm(alpha * hf_ref[...] + acc_s[...], g_ref[...], b_ref[...])
        of_ref[...] = o
        ob_ref[...] = o.astype(BF16)


def _mlp_kernel(hb_ref, hf_ref, wu_ref, wd_ref, g_ref, b_ref, of_ref, ob_ref, acc_s, *, alpha):
    _mlp_body(hb_ref, hf_ref, wu_ref, wd_ref, g_ref, b_ref, of_ref, ob_ref, acc_s, alpha)


def _mlp_kmean_kernel(pt_ref, hf_ref, wu_ref, wd_ref, g_ref, b_ref, ck_ref, of_ref, ob_ref, km_ref,
                      acc_s, pbuf, sem, *, alpha, cache_layer, pages):
    step = pl.program_id(0) * pl.num_programs(1) + pl.program_id(1)
    n_steps = pl.num_programs(0) * pl.num_programs(1)
    n_pages = pt_ref.shape[1]
    cur = step % 2

    def copies(st, slot):
        first = st * pages
        return [pltpu.make_async_copy(ck_ref.at[cache_layer, pt_ref[first // n_pages, first % n_pages + u]],
                                      pbuf.at[slot, u], sem.at[slot]) for u in range(pages)]

    @pl.when(step == 0)
    def _():
        for c in copies(0, 0):
            c.start()

    @pl.when(step + 1 < n_steps)
    def _():
        for c in copies(step + 1, 1 - cur):
            c.start()

    def page_sums():
        for c in copies(step, cur):
            c.wait()
        per_blk = MOBA_BLOCK // PAGE_SIZE
        for u in range(pages // per_blk):
            tot = pbuf[cur, per_blk * u].sum(axis=0)
            for e in range(1, per_blk):
                tot = tot + pbuf[cur, per_blk * u + e].sum(axis=0)
            km_ref[u] = tot * (1.0 / MOBA_BLOCK)

    _mlp_body(None, hf_ref, wu_ref, wd_ref, g_ref, b_ref, of_ref, ob_ref, acc_s, alpha, page_sums)


def _mlp(hb, hf, wu, wd, layer, g, b, alpha, tm, tf, kmean_job=None):
    M, D = hf.shape
    FF = wu.shape[2]
    grid = (M // tm, FF // tf)
    rows = lambda: pl.BlockSpec((tm, D), lambda i, f, *_: (i, 0))
    vec = lambda: pl.BlockSpec((1, D), lambda i, f, *_: (0, 0))
    in_specs = [rows(), rows(), pl.BlockSpec((None, D, tf), lambda i, f, *_: (layer, 0, f)),
                pl.BlockSpec((None, tf, D), lambda i, f, *_: (layer, f, 0)), vec(), vec()]
    out_shape = [jax.ShapeDtypeStruct((M, D), F32), jax.ShapeDtypeStruct((M, D), BF16)]
    acc = pltpu.VMEM((tm, D), F32)
    if kmean_job is None:
        return pl.pallas_call(
            functools.partial(_mlp_kernel, alpha=alpha),
            grid=grid, in_specs=in_specs, out_specs=[rows(), rows()], out_shape=out_shape,
            scratch_shapes=[acc],
            compiler_params=_cparams(("arbitrary", "arbitrary")),
            name="mlp_ln2",
        )(hb, hf, wu, wd, g, b)
    cache_k, page_table, cache_layer = kmean_job
    B, n_pages = page_table.shape
    n_steps = grid[0] * grid[1]
    per_blk = MOBA_BLOCK // PAGE_SIZE
    pages = B * n_pages // n_steps
    assert pages * n_steps == B * n_pages and pages % per_blk == 0 and n_pages % pages == 0
    nf = grid[1]
    steps_per_seq = n_pages // pages
    km_spec = pl.BlockSpec((None, pages // per_blk, H_A, HEAD_DIM),
                           lambda i, f, *_: ((i * nf + f) // steps_per_seq, (i * nf + f) % steps_per_seq, 0, 0))
    return pl.pallas_call(
        functools.partial(_mlp_kmean_kernel, alpha=alpha, cache_layer=cache_layer, pages=pages),
        grid_spec=pltpu.PrefetchScalarGridSpec(
            num_scalar_prefetch=1, grid=grid,
            in_specs=in_specs[1:] + [pl.BlockSpec(memory_space=pl.ANY)],
            out_specs=[rows(), rows(), km_spec],
            scratch_shapes=[acc, pltpu.VMEM((2, pages, PAGE_SIZE, H_A, HEAD_DIM), F32),
                            pltpu.SemaphoreType.DMA((2,))]),
        out_shape=out_shape + [jax.ShapeDtypeStruct((B, n_pages // per_blk, H_A, HEAD_DIM), F32)],
        compiler_params=_cparams(("arbitrary", "arbitrary")),
        name="mlp_ln2_kmean",
    )(page_table, hf, wu, wd, g, b, cache_k)


PAGES_PER_STEP = 16


def _kmean_kernel(pt_ref, *refs):
    page_refs, o_ref = refs[:PAGES_PER_STEP], refs[PAGES_PER_STEP]
    per_blk = MOBA_BLOCK // PAGE_SIZE
    for u in range(PAGES_PER_STEP // per_blk):
        tot = page_refs[per_blk * u][...].sum(axis=0)
        for e in range(1, per_blk):
            tot = tot + page_refs[per_blk * u + e][...].sum(axis=0)
        o_ref[u] = tot * (1.0 / MOBA_BLOCK)


def _cache_kmean(cache_k, page_table, layer):
    B, n_pages = page_table.shape
    nb = n_pages * PAGE_SIZE // MOBA_BLOCK
    steps = n_pages // PAGES_PER_STEP
    blocks_per_step = PAGES_PER_STEP * PAGE_SIZE // MOBA_BLOCK

    def page_spec(u):
        return pl.BlockSpec((None, None, PAGE_SIZE, H_A, HEAD_DIM),
                            lambda b, g, pt: (layer, pt[b, g * PAGES_PER_STEP + u], 0, 0, 0))

    return pl.pallas_call(
        _kmean_kernel,
        grid_spec=pltpu.PrefetchScalarGridSpec(
            num_scalar_prefetch=1,
            grid=(B, steps),
            in_specs=[page_spec(u) for u in range(PAGES_PER_STEP)],
            out_specs=pl.BlockSpec((None, blocks_per_step, H_A, HEAD_DIM), lambda b, g, pt: (b, g, 0, 0)),
        ),
        out_shape=jax.ShapeDtypeStruct((B, nb, H_A, HEAD_DIM), F32),
        compiler_params=_cparams(("arbitrary", "arbitrary")),
        name="cache_kmean",
    )(page_table, *([cache_k] * PAGES_PER_STEP))


def _sample_select_kernel(q_ref, km_ref, o_ref, *, nb):
    out = jnp.zeros(o_ref.shape, jnp.int32)
    lane_o = lax.broadcasted_iota(jnp.int32, o_ref.shape, 1)
    for h in range(H_A):
        hs = slice(h * HEAD_DIM, (h + 1) * HEAD_DIM)
        gate = _dot_nt_hi(q_ref[:, hs], km_ref[:, hs])
        rank, lane = _topk_select(gate, nb, nb)
        for slot in range(MOBA_TOPK):
            pick = (lane < nb) & (rank == slot)
            idx = jnp.sum(jnp.where(pick, lane, 0), axis=1, keepdims=True)
            out = jnp.where(lane_o == h * 4 + slot, idx, out)
    o_ref[...] = out


def _sample_select(proj_s, kmean_pad, B, T, nb):
    return pl.pallas_call(
        functools.partial(_sample_select_kernel, nb=nb),
        grid=(B,),
        in_specs=[pl.BlockSpec((T, W_A), lambda b: (b, 0)),
                  pl.BlockSpec((None, LANES, W_A), lambda b: (b, 0, 0))],
        out_specs=pl.BlockSpec((T, LANES), lambda b: (b, 0)),
        out_shape=jax.ShapeDtypeStruct((B * T, LANES), jnp.int32),
        compiler_params=_cparams(("arbitrary",)),
        name="moba_sample_select",
    )(proj_s, kmean_pad)


def _moba_sample_kernel(sel_ref, pt_ref, rb_ref, q_ref, kn_ref, vn_ref, ck_ref, cv_ref, o_ref,
                        kbuf, vbuf, sem, *, T, past, layer):
    b = pl.program_id(0)
    h = pl.program_id(1)
    nh = pl.num_programs(1)
    step = b * nh + h
    n_steps = pl.num_programs(0) * nh
    per_blk = MOBA_BLOCK // PAGE_SIZE

    def copies(bb, hh, buf, qi, slot, e):
        blk = sel_ref[bb * T + qi, hh * 4 + slot]
        page = pt_ref[bb, blk * per_blk + e]
        idx = qi * MOBA_TOPK + slot
        dst = pl.ds(e * PAGE_SIZE, PAGE_SIZE)
        return (pltpu.make_async_copy(ck_ref.at[layer, page, :, hh, :], kbuf.at[buf, idx, dst, :], sem.at[buf, 0]),
                pltpu.make_async_copy(cv_ref.at[layer, page, :, hh, :], vbuf.at[buf, idx, dst, :], sem.at[buf, 1]))

    def for_all_copies(bb, hh, buf, fn):
        for qi in range(T):
            for slot in range(MOBA_TOPK):
                for e in range(per_blk):
                    for c in copies(bb, hh, buf, qi, slot, e):
                        fn(c)

    cur = step % 2

    @pl.when(step == 0)
    def _():
        for_all_copies(b, h, 0, lambda c: c.start())

    @pl.when(step + 1 < n_steps)
    def _():
        nxt = step + 1
        for_all_copies(nxt // nh, nxt % nh, 1 - cur, lambda c: c.start())

    q = q_ref[...]
    qb = q.astype(BF16)
    scale = HEAD_DIM ** -0.5
    rowT = lax.broadcasted_iota(jnp.int32, (T, T), 0)
    colT = lax.broadcasted_iota(jnp.int32, (T, T), 1)
    s_own = _dot_nt(qb, kn_ref[...].astype(BF16)) * scale + _t5_bias_from_dist(rowT - colT, rb_ref, h)
    s_own = jnp.where(rowT >= colT, s_own, NEG_BIG)

    for_all_copies(b, h, cur, lambda c: c.wait())

    rowB = lax.broadcasted_iota(jnp.int32, (T, MOBA_BLOCK), 0)
    colB = lax.broadcasted_iota(jnp.int32, (T, MOBA_BLOCK), 1)
    row1 = lax.broadcasted_iota(jnp.int32, (T, 1), 0)
    s_slot = []
    for slot in range(MOBA_TOPK):
        s = jnp.zeros((T, MOBA_BLOCK), F32)
        blk_col = jnp.zeros((T, 1), jnp.int32)
        for qi in range(T):
            sq = _dot_nt(qb, kbuf[cur, qi * MOBA_TOPK + slot].astype(BF16))
            s = jnp.where(rowB == qi, sq, s)
            blk_col = jnp.where(row1 == qi, sel_ref[b * T + qi, h * 4 + slot], blk_col)
        dist = past + rowB - (blk_col * MOBA_BLOCK + colB)
        s_slot.append(s * scale + _t5_bias_from_dist(dist, rb_ref, h))

    m = jnp.max(s_own, axis=1, keepdims=True)
    for s in s_slot:
        m = jnp.maximum(m, jnp.max(s, axis=1, keepdims=True))
    p_own = jnp.exp(s_own - m)
    l = jnp.sum(p_own, axis=1, keepdims=True)
    acc = _dot(p_own.astype(BF16), vn_ref[...].astype(BF16))
    for slot in range(MOBA_TOPK):
        p = jnp.exp(s_slot[slot] - m)
        l = l + jnp.sum(p, axis=1, keepdims=True)
        for qi in range(T):
            pq = jnp.where(rowB == qi, p, 0.0).astype(BF16)
            acc = acc + _dot(pq, vbuf[cur, qi * MOBA_TOPK + slot].astype(BF16))
    o_ref[...] = acc / l


def _moba_sample(sel, page_table, rel_bias, proj_s, cache_k, cache_v, B, T, past, layer):
    n_slots = T * MOBA_TOPK
    return pl.pallas_call(
        functools.partial(_moba_sample_kernel, T=T, past=past, layer=layer),
        grid_spec=pltpu.PrefetchScalarGridSpec(
            num_scalar_prefetch=3,
            grid=(B, H_A),
            in_specs=[pl.BlockSpec((T, HEAD_DIM), lambda b, h, *_: (b, h)),
                      pl.BlockSpec((T, HEAD_DIM), lambda b, h, *_: (b, H_A + h)),
                      pl.BlockSpec((T, HEAD_DIM), lambda b, h, *_: (b, 2 * H_A + h)),
                      pl.BlockSpec(memory_space=pl.ANY),
                      pl.BlockSpec(memory_space=pl.ANY)],
            out_specs=pl.BlockSpec((T, HEAD_DIM), lambda b, h, *_: (b, h)),
            scratch_shapes=[pltpu.VMEM((2, n_slots, MOBA_BLOCK, HEAD_DIM), F32),
                            pltpu.VMEM((2, n_slots, MOBA_BLOCK, HEAD_DIM), F32),
                            pltpu.SemaphoreType.DMA((2, 2))],
        ),
        out_shape=jax.ShapeDtypeStruct((B * T, W_A), F32),
        compiler_params=_cparams(("arbitrary", "arbitrary")),
        name="moba_sample",
    )(sel, page_table, rel_bias, proj_s, proj_s, proj_s, cache_k, cache_v)


def _pad_rows(a, rows):
    return jnp.pad(a, ((0, 0), (0, rows - a.shape[1])) + ((0, 0),) * (a.ndim - 2))


def _mixer_states_in(c0, n0, m0, conv0):
    B = c0.shape[0]
    n0p = _pad_rows(n0, SUBLANES)
    m0p = _pad_rows(jnp.broadcast_to(m0[:, :, None], (B, H_B, LANES)), SUBLANES)
    conv0p = jnp.pad(conv0, ((0, 0), (SUBLANES - (CONV_W - 1), 0), (0, 0)))
    return c0, n0p, m0p, conv0p


def _tile(m, pref):
    return pref if m % pref == 0 else m


def _layer(x_f32, x_bf16, B, T, layer, wts, states, attn_fn, mlstm_chunk, hgrn_chunk, kmean_job=None):
    (w_main, w_gate, bgate, conv_w, conv_b, gn_b, gn_c, lb, w_out, ln1_g, ln1_b, w_up, w_down,
     ln2_g, ln2_b, alpha) = wts
    c0, n0, m0, conv0, s0 = states
    M = B * T
    tm = _tile(M, 2048)
    tn = 512
    proj = _matmul(x_bf16, w_main, layer, tm, tn, 0, N_MAIN // tn)
    gates = _matmul(x_bf16, w_gate, layer, tm, LANES, 0, 1)
    ya, k_new, v_new = attn_fn(proj)
    c0, n0p, m0p, conv0p = _mixer_states_in(c0, n0, m0, conv0)
    y_dtype = BF16 if T % 16 == 0 else F32
    yb, c_new, n_new, m_new = _mlstm(proj, gates, conv0p, conv_w, conv_b, bgate, gn_b, c0, n0p, m0p, B, T,
                                     mlstm_chunk, y_dtype)
    yc, s_new = _hgrn(proj, lb, gn_c, s0, B, T, hgrn_chunk, min(HGRN_SUB, hgrn_chunk), y_dtype)
    tm2 = _tile(M, 512)
    hf, hb = _outproj(ya, yb, yc, x_f32, w_out, layer, ln1_g, ln1_b, alpha, tm2)
    of, ob, *km = _mlp(hb, hf, w_up, w_down, layer, ln2_g, ln2_b, alpha, tm2, 1024, kmean_job)
    conv_new = proj.reshape(B, T, N_MAIN)[:, T - (CONV_W - 1):, 3 * W_A:3 * W_A + 2 * W_B]
    return of, ob, (k_new, v_new, c_new, n_new[:, :H_B, :], m_new[:, :H_B, 0], conv_new, s_new), km


def kernel(x_prompt, x_sample, cache_k, cache_v, page_table, state_b_C, state_b_n, state_b_m, state_b_conv,
           state_c_S, w_in, b_gate, conv_w, conv_b, gn_b, gn_c, lower_bounds, rel_bias, w_out, ln1_g, ln1_b,
           w_up, w_down, ln2_g, ln2_b):
    depth = w_in.shape[0]
    Bp, Tp, D = x_prompt.shape
    Bs, Ts, _ = x_sample.shape
    n_pages = page_table.shape[1]
    past = n_pages * PAGE_SIZE
    alpha = (2 * depth) ** 0.25

    sm = jax.nn.softmax(lower_bounds.astype(F32), axis=0)
    lb_all = jnp.cumsum(sm, axis=0) - sm[0]

    w_main = jnp.concatenate([w_in[:, :, :GATE_COL0], w_in[:, :, GATE_COL0 + 2 * H_B:]], axis=-1).astype(BF16)
    w_gate = jnp.pad(w_in[:, :, GATE_COL0:GATE_COL0 + 2 * H_B], ((0, 0), (0, 0), (0, LANES - 2 * H_B))).astype(BF16)
    bgate = jnp.pad(b_gate, ((0, 0), (0, LANES - 2 * H_B)))[:, None, :]
    w_out_b = w_out.astype(BF16)
    w_up_b = w_up.astype(BF16)
    w_down_b = w_down.astype(BF16)

    bias_tab = _bias_table(rel_bias, Tp // MOBA_BLOCK)
    nb_past = past // MOBA_BLOCK
    kmean = _cache_kmean(cache_k, page_table, 0)

    zeros_p = (jnp.zeros((Bp, H_B, HEAD_DIM, HEAD_DIM), F32), jnp.zeros((Bp, H_B, HEAD_DIM), F32),
               jnp.zeros((Bp, H_B), F32), jnp.zeros((Bp, CONV_W - 1, 2 * W_B), F32),
               jnp.zeros((Bp, H_C, HEAD_DIM, HEAD_DIM), F32))

    xp_f = x_prompt.reshape(Bp * Tp, D)
    xs_f = x_sample.reshape(Bs * Ts, D)
    xp_b = xp_f.astype(BF16)
    xs_b = xs_f.astype(BF16)
    outs = [[] for _ in range(14)]
    mlstm_chunk_p = math.gcd(Tp, 256)
    hgrn_chunk_p = math.gcd(Tp, 256)
    for l in range(depth):
        wts = (w_main, w_gate, bgate[l], conv_w[l], conv_b[l][None, :], gn_b[l][None, :], gn_c[l][None, :],
               lb_all[l][None, :], w_out_b, ln1_g[l][None, :], ln1_b[l][None, :], w_up_b, w_down_b,
               ln2_g[l][None, :], ln2_b[l][None, :], alpha)

        attn_p = lambda proj: _moba_prompt(proj, bias_tab, Bp, Tp)
        job = (cache_k, page_table, l + 1) if l + 1 < depth else None
        kmean_pad = jnp.pad(kmean.reshape(Bs, nb_past, W_A), ((0, 0), (0, LANES - nb_past), (0, 0)))
        xp_f, xp_b, (kp, vp, Cp, nP, mP, cP, SP), km_next = _layer(xp_f, xp_b, Bp, Tp, l, wts, zeros_p, attn_p,
                                                                   mlstm_chunk_p, hgrn_chunk_p, job)
        if km_next:
            kmean = km_next[0]

        def attn_s(proj, l=l, kmean_pad=kmean_pad):
            sel = _sample_select(proj, kmean_pad, Bs, Ts, nb_past)
            ya = _moba_sample(sel, page_table, rel_bias, proj, cache_k, cache_v, Bs, Ts, past, l)
            kv = proj[:, W_A:3 * W_A].reshape(Bs, Ts, 2, H_A, HEAD_DIM)
            return ya, kv[:, :, 0], kv[:, :, 1]

        st_s = (state_b_C[l], state_b_n[l], state_b_m[l], state_b_conv[l], state_c_S[l])
        xs_f, xs_b, (ks, vs, Cs, nS, mS, cS, SS), _ = _layer(xs_f, xs_b, Bs, Ts, l, wts, st_s, attn_s, Ts, Ts)

        for lst, val in zip(outs, (kp, vp, ks, vs, Cp, nP, mP, cP, Cs, nS, mS, cS, SP, SS)):
            lst.append(val)

    return (xp_f.reshape(Bp, Tp, D), xs_f.reshape(Bs, Ts, D)) + tuple(jnp.stack(o) for o in outs)
```

```python
import functools
import math

import numpy as np
import jax
import jax.numpy as jnp
from jax import lax
from jax.experimental import pallas as pl
from jax.experimental.pallas import tpu as pltpu

F32 = jnp.float32
BF16 = jnp.bfloat16

HEAD_DIM = 128
H_A, H_B, H_C = 8, 4, 4
W_A, W_B, W_C = H_A * HEAD_DIM, H_B * HEAD_DIM, H_C * HEAD_DIM
MOBA_BLOCK = 256
MOBA_TOPK = 3
NUM_BUCKETS = 32
MAX_DISTANCE = 2048
CONV_W = 4
EPS = 1e-5
GATE_MASK = -1e30
NEG_BIG = -1e30
LOG2E = math.log2(math.e)
PAGE_SIZE = 128
LANES = 128
SUBLANES = 8
HGRN_SUB = 32
HGRN_SAFE_DECAY = 60.0
MOBA_GROUP = 4
QB = 4
VT_EXTRA = 16
VMEM_LIMIT = 56 * 1024 * 1024

N_MAIN = 3 * W_A + 4 * W_B + 4 * W_C
GATE_COL0 = 3 * W_A + 3 * W_B


def _t5_thresholds():
    max_exact = NUM_BUCKETS // 2
    n = np.arange(1, 4 * MAX_DISTANCE, dtype=np.float32)
    large = max_exact + (np.log(n / np.float32(max_exact)) / np.float32(math.log(MAX_DISTANCE / max_exact))
                         * np.float32(NUM_BUCKETS - max_exact)).astype(np.int32)
    large = np.minimum(large, NUM_BUCKETS - 1)
    thr = []
    for b in range(max_exact + 1, NUM_BUCKETS):
        thr.append(int(np.argmax(large >= b)) + 1)
    return tuple(thr)


T5_THRESHOLDS = _t5_thresholds()


def _cparams(sem):
    return pltpu.CompilerParams(dimension_semantics=sem, vmem_limit_bytes=VMEM_LIMIT)


def _dot(a, b):
    return jnp.dot(a, b, preferred_element_type=F32)


def _dot_nt(a, b):
    return lax.dot_general(a, b, (((1,), (1,)), ((), ())), preferred_element_type=F32)


def _dot_tn(a, b):
    return lax.dot_general(a, b, (((0,), (0,)), ((), ())), preferred_element_type=F32)


def _dot_hi(a, b):
    return jnp.dot(a, b, precision=lax.Precision.HIGHEST, preferred_element_type=F32)


def _dot_nt_hi(a, b):
    return lax.dot_general(a, b, (((1,), (1,)), ((), ())), precision=lax.Precision.HIGHEST,
                           preferred_element_type=F32)


def _sigmoid(x):
    return 1.0 / (1.0 + jnp.exp(-x))


def _layer_norm(z, g, b):
    mu = jnp.mean(z, axis=-1, keepdims=True)
    zc = z - mu
    var = jnp.mean(zc * zc, axis=-1, keepdims=True)
    return zc * lax.rsqrt(var + EPS) * g + b


def _matmul_kernel(x_ref, w_ref, o_ref):
    o_ref[...] = _dot(x_ref[...], w_ref[...])


def _matmul(x, w, layer, tm, tn, col_blk0, n_blk):
    M, K = x.shape
    N = n_blk * tn
    return pl.pallas_call(
        _matmul_kernel,
        grid=(M // tm, n_blk),
        in_specs=[pl.BlockSpec((tm, K), lambda i, j: (i, 0)),
                  pl.BlockSpec((None, K, tn), lambda i, j: (layer, 0, col_blk0 + j))],
        out_specs=pl.BlockSpec((tm, tn), lambda i, j: (i, j)),
        out_shape=jax.ShapeDtypeStruct((M, N), F32),
        compiler_params=_cparams(("arbitrary", "arbitrary")),
        name="in_proj",
    )(x, w)


def _t5_bias_from_dist(dist, rb_ref, h):
    n = jnp.maximum(dist, 0)
    large = jnp.full(n.shape, NUM_BUCKETS // 2, jnp.int32)
    for thr in T5_THRESHOLDS:
        large = large + (n >= thr).astype(jnp.int32)
    bucket = jnp.where(n < NUM_BUCKETS // 2, n, large)
    val = jnp.zeros(n.shape, F32)
    for b in range(NUM_BUCKETS):
        val = jnp.where(bucket == b, rb_ref[b, h], val)
    return val


def _bias_table_kernel(rb_ref, o_ref, *, nb):
    h = pl.program_id(0)
    blk = MOBA_BLOCK
    d_const = -(-(T5_THRESHOLDS[-1] - 1) // blk) + 1
    row = lax.broadcasted_iota(jnp.int32, (blk, blk), 0)
    col = lax.broadcasted_iota(jnp.int32, (blk, blk), 1)
    for e in range(2 * nb - 1):
        d = nb - 1 - e
        if d >= d_const:
            tile = jnp.full((blk, blk), rb_ref[NUM_BUCKETS - 1, h], F32)
        elif d < 0:
            tile = jnp.full((blk, blk), rb_ref[0, h], F32)
        else:
            tile = _t5_bias_from_dist(d * blk + col - row, rb_ref, h)
        o_ref[e * blk:(e + 1) * blk, :] = tile * LOG2E


def _bias_table(rel_bias, nb):
    ne = 2 * nb - 1
    return pl.pallas_call(
        functools.partial(_bias_table_kernel, nb=nb),
        grid=(H_A,),
        in_specs=[pl.BlockSpec(memory_space=pltpu.SMEM)],
        out_specs=pl.BlockSpec((None, ne * MOBA_BLOCK, MOBA_BLOCK), lambda h: (h, 0, 0)),
        out_shape=jax.ShapeDtypeStruct((H_A, ne * MOBA_BLOCK, MOBA_BLOCK), F32),
        compiler_params=_cparams(("arbitrary",)),
        name="t5_bias_table",
    )(rel_bias)


def _topk_select(gate, n_valid, n_cand):
    lane = lax.broadcasted_iota(jnp.int32, gate.shape, 1)
    gm = jnp.where(lane < n_valid, gate, GATE_MASK)
    rank = jnp.zeros(gate.shape, jnp.int32)
    for c in range(n_cand):
        gc = gm[:, c:c + 1]
        ahead = (gc > gm) | ((gc == gm) & (c < lane))
        rank = rank + ahead.astype(jnp.int32)
    return rank, lane


def _moba_prompt_kernel(q_ref, k_ref, v_ref, bias_ref, o_ref, kout_ref, vout_ref,
                        ka_s, vt_s, vtd_s, qa_s, kmean_s, sem, *, nb):
    b = pl.program_id(0)
    h = pl.program_id(1)
    i = pl.program_id(2)
    blk = MOBA_BLOCK
    G = MOBA_GROUP
    nbp = -(-nb // SUBLANES) * SUBLANES

    kv_copies = (pltpu.make_async_copy(k_ref, kout_ref.at[b, :, h, :], sem.at[0]),
                 pltpu.make_async_copy(v_ref, vout_ref.at[b, :, h, :], sem.at[1]))

    @pl.when(i == 0)
    def _():
        for c in kv_copies:
            c.start()
        kmean_s[...] = jnp.zeros(kmean_s.shape, F32)
        lane = lax.broadcasted_iota(jnp.int32, (blk, HEAD_DIM), 1)
        for n in range(nb):
            rows = slice((n % G) * blk, (n % G + 1) * blk)
            kf = k_ref[n * blk:(n + 1) * blk, :]
            ka_s[n // G, rows, 0:HEAD_DIM] = kf.astype(BF16)
            ka_s[n // G, rows, HEAD_DIM:2 * HEAD_DIM] = jnp.where(lane == n, NEG_BIG, 0.0).astype(BF16)
            ones_row = jnp.where(lax.broadcasted_iota(jnp.int32, (VT_EXTRA, blk), 0) == 0, 1.0, 0.0)
            vt = jnp.concatenate([v_ref[n * blk:(n + 1) * blk, :].T, ones_row], axis=0).astype(BF16)
            vt_s[n // G, :, rows] = vt
            vtd_s[n] = vt
            kmean_s[n:n + 1, :] = jnp.mean(kf, axis=0, keepdims=True)
        km = kmean_s[...]
        sub = lax.broadcasted_iota(jnp.int32, (nbp, blk), 0)
        pad = jnp.zeros((HEAD_DIM - nbp, blk), F32)
        for t in range(nb):
            q = q_ref[t * blk:(t + 1) * blk, :]
            if t > MOBA_TOPK:
                gm = jnp.where(sub < t, _dot_nt_hi(km, q), GATE_MASK)
                rank = jnp.zeros((nbp, blk), jnp.int32)
                for c in range(t):
                    gc = gm[c:c + 1, :]
                    rank = rank + ((gc > gm) | ((gc == gm) & (c < sub))).astype(jnp.int32)
                notsel = jnp.where((sub < t) & (rank < MOBA_TOPK), 0.0, 1.0)
            else:
                notsel = jnp.where(sub < t, 0.0, 1.0)
            qt = (q * (HEAD_DIM ** -0.5 * LOG2E)).T
            qa_s[t // QB, :, (t % QB) * blk:(t % QB + 1) * blk] = (
                jnp.concatenate([qt, notsel, pad], axis=0).astype(BF16))

    qaug = qa_s[i]
    rowk = lax.broadcasted_iota(jnp.int32, (blk, blk), 0)
    colq = lax.broadcasted_iota(jnp.int32, (blk, blk), 1)

    def tile(n_grp):
        m_parts, acc_parts = [], []
        for u in range(QB):
            iu = i * QB + u
            kd = ka_s[iu // G, pl.ds(pl.multiple_of((iu % G) * blk, blk), blk), 0:HEAD_DIM]
            sd = _dot(kd, qaug[0:HEAD_DIM, u * blk:(u + 1) * blk]) + bias_ref[(nb - 1) * blk:nb * blk, :]
            sd = jnp.where(colq >= rowk, sd, NEG_BIG)
            mu = jnp.max(sd, axis=0, keepdims=True)
            m_parts.append(mu)
            acc_parts.append(_dot(vtd_s[iu], jnp.exp2(sd - mu).astype(BF16)))
        m = jnp.concatenate(m_parts, axis=1)
        acc = jnp.concatenate(acc_parts, axis=1)
        for g in range(n_grp):
            bias = jnp.concatenate(
                [bias_ref[pl.ds(pl.multiple_of((nb - 1 - (i * QB + u) + G * g) * blk, blk), G * blk), :]
                 for u in range(QB)], axis=1)
            s = _dot(ka_s[g], qaug) + bias
            m_new = jnp.maximum(m, jnp.max(s, axis=0, keepdims=True))
            acc = jnp.exp2(m - m_new) * acc + _dot(vt_s[g], jnp.exp2(s - m_new).astype(BF16))
            m = m_new
        o_ref[...] = (acc[0:HEAD_DIM] / acc[HEAD_DIM:HEAD_DIM + 1]).T.astype(o_ref.dtype)

    steps_per_grp = G // QB
    for grp in range(nb // G):
        pl.when(i // steps_per_grp == grp)(functools.partial(tile, grp + 1))

    @pl.when(i == nb // QB - 1)
    def _():
        for c in kv_copies:
            c.wait()


def _moba_prompt(proj, bias_tab, B, T):
    nb = T // MOBA_BLOCK
    G = MOBA_GROUP
    assert nb % G == 0 and G % QB == 0 and nb >= 2 * G, "key blocks are processed in groups"
    nq = nb // QB
    kv_shape = jax.ShapeDtypeStruct((B, T, H_A, HEAD_DIM), F32)
    return pl.pallas_call(
        functools.partial(_moba_prompt_kernel, nb=nb),
        grid=(B, H_A, nq),
        in_specs=[pl.BlockSpec((T, HEAD_DIM), lambda b, h, i: (b, h)),
                  pl.BlockSpec((T, HEAD_DIM), lambda b, h, i: (b, H_A + h)),
                  pl.BlockSpec((T, HEAD_DIM), lambda b, h, i: (b, 2 * H_A + h)),
                  pl.BlockSpec((None, (2 * nb - 1) * MOBA_BLOCK, MOBA_BLOCK), lambda b, h, i: (h, 0, 0))],
        out_specs=[pl.BlockSpec((QB * MOBA_BLOCK, HEAD_DIM), lambda b, h, i: (b * nq + i, h)),
                   pl.BlockSpec(memory_space=pl.ANY), pl.BlockSpec(memory_space=pl.ANY)],
        out_shape=[jax.ShapeDtypeStruct((B * T, W_A), BF16), kv_shape, kv_shape],
        scratch_shapes=[pltpu.VMEM((nb // G, G * MOBA_BLOCK, 2 * HEAD_DIM), BF16),
                        pltpu.VMEM((nb // G, HEAD_DIM + VT_EXTRA, G * MOBA_BLOCK), BF16),
                        pltpu.VMEM((nb, HEAD_DIM + VT_EXTRA, MOBA_BLOCK), BF16),
                        pltpu.VMEM((nq, 2 * HEAD_DIM, QB * MOBA_BLOCK), BF16),
                        pltpu.VMEM((-(-nb // SUBLANES) * SUBLANES, HEAD_DIM), F32),
                        pltpu.SemaphoreType.DMA((2,))],
        compiler_params=_cparams(("arbitrary", "arbitrary", "arbitrary")),
        name="moba_prompt",
    )(proj, proj, proj, bias_tab)


def _mlstm_kernel(q_ref, k_ref, qp_ref, kp_ref, v_ref, og_ref, g_ref, conv0_ref, cw_ref, cb_ref, bg_ref,
                  gn_ref, c0_ref, n0_ref, m0_ref,
                  y_ref, cout_ref, nout_ref, mout_ref,
                  c_s, n_s, m_s, ext_s, *, L):
    c = pl.program_id(1)
    last = pl.num_programs(1) - 1

    @pl.when(c == 0)
    def _():
        c_s[...] = c0_ref[...]
        n_s[...] = n0_ref[...]
        m_s[...] = m0_ref[...]

    def conv_silu(u_ref, up_ref, col0):
        u = u_ref[...]
        tail = jnp.where(c == 0, conv0_ref[:, col0:col0 + W_B], up_ref[L - SUBLANES:L, :])
        ext_s[0:SUBLANES, :] = tail
        ext_s[SUBLANES:SUBLANES + L, :] = u
        acc = u * cw_ref[CONV_W - 1:CONV_W, col0:col0 + W_B] + cb_ref[:, col0:col0 + W_B]
        for j in range(1, CONV_W):
            xj = ext_s[SUBLANES - j:SUBLANES - j + L, :]
            acc = acc + xj * cw_ref[CONV_W - 1 - j:CONV_W - j, col0:col0 + W_B]
        return acc * _sigmoid(acc)

    qc = conv_silu(q_ref, qp_ref, 0)
    kc = conv_silu(k_ref, kp_ref, W_B) * (HEAD_DIM ** -0.5)
    v = v_ref[...]
    og = og_ref[...]

    g = g_ref[...] + bg_ref[...]
    lf = jnp.minimum(g, 0.0) - jnp.log(1.0 + jnp.exp(-jnp.abs(g)))
    row = lax.broadcasted_iota(jnp.int32, (L, L), 0)
    col = lax.broadcasted_iota(jnp.int32, (L, L), 1)
    causal = row >= col
    fcum = _dot_hi(causal.astype(F32), lf)

    for h in range(H_B):
        hs = slice(h * HEAD_DIM, (h + 1) * HEAD_DIM)
        fcol = fcum[:, H_B + h:H_B + h + 1]
        rcol = g[:, h:h + 1] - fcol
        rrow = jnp.sum(jnp.where(row == col, rcol, 0.0), axis=0, keepdims=True)
        dm = jnp.where(causal, fcol + rrow, -jnp.inf)
        mprev = m_s[h:h + 1, 0:1]
        gcol = fcol + mprev
        mt = jnp.maximum(gcol, jnp.max(dm, axis=1, keepdims=True))
        w = jnp.exp(dm - mt)
        wg = jnp.exp(gcol - mt)
        qh = qc[:, hs]
        kh = kc[:, hs]
        vh = v[:, hs]
        qhb = qh.astype(BF16)
        s = _dot_nt(qhb, kh.astype(BF16)) * w
        num = _dot(s.astype(BF16), vh.astype(BF16)) + wg * _dot(qhb, c_s[h].astype(BF16))
        den = jnp.sum(s, axis=1, keepdims=True) + wg * jnp.sum(qh * n_s[h:h + 1, :], axis=1, keepdims=True)
        hh = num / jnp.maximum(jnp.abs(den), jnp.exp(-mt))
        ml = mt[L - 1:L, :]
        wl = jnp.exp(fcol[L - 1:L, :] + rcol - ml)
        gl = jnp.exp(gcol[L - 1:L, :] - ml)
        kw = kh * wl
        c_s[h] = gl * c_s[h] + _dot_tn(kw.astype(BF16), vh.astype(BF16))
        n_s[h:h + 1, :] = gl * n_s[h:h + 1, :] + jnp.sum(kw, axis=0, keepdims=True)
        m_s[h:h + 1, :] = jnp.broadcast_to(ml, (1, LANES))
        hc = hh - jnp.mean(hh, axis=1, keepdims=True)
        yn = hc * lax.rsqrt(jnp.mean(hc * hc, axis=1, keepdims=True) + EPS) * gn_ref[:, hs]
        y_ref[:, hs] = (_sigmoid(og[:, hs]) * yn).astype(y_ref.dtype)

    @pl.when(c == last)
    def _():
        cout_ref[...] = c_s[...]
        nout_ref[...] = n_s[...]
        mout_ref[...] = m_s[...]


def _mlstm(proj, gates, conv0, conv_w, conv_b, bgate, gn_b, c0, n0, m0, B, T, L, out_dtype):
    nc = T // L
    q_blk = 3 * W_A // W_B
    k_blk, v_blk, o_blk = q_blk + 1, q_blk + 2, q_blk + 3

    def cur(colblk):
        return pl.BlockSpec((L, W_B), lambda b, c: (b * nc + c, colblk))

    def prev(colblk):
        return pl.BlockSpec((L, W_B), lambda b, c: (b * nc + jnp.maximum(c - 1, 0), colblk))

    full2 = lambda shape: pl.BlockSpec(shape, lambda b, c: (0, 0))
    per_b3 = lambda shape: pl.BlockSpec((None,) + shape, lambda b, c: (b, 0, 0))
    return pl.pallas_call(
        functools.partial(_mlstm_kernel, L=L),
        grid=(B, nc),
        in_specs=[cur(q_blk), cur(k_blk), prev(q_blk), prev(k_blk), cur(v_blk), cur(o_blk),
                  pl.BlockSpec((L, LANES), lambda b, c: (b * nc + c, 0)),
                  per_b3((SUBLANES, 2 * W_B)),
                  full2((CONV_W, 2 * W_B)), full2((1, 2 * W_B)), full2((1, LANES)), full2((1, W_B)),
                  pl.BlockSpec((None, H_B, HEAD_DIM, HEAD_DIM), lambda b, c: (b, 0, 0, 0)),
                  per_b3((SUBLANES, HEAD_DIM)), per_b3((SUBLANES, LANES))],
        out_specs=[pl.BlockSpec((L, W_B), lambda b, c: (b * nc + c, 0)),
                   pl.BlockSpec((None, H_B, HEAD_DIM, HEAD_DIM), lambda b, c: (b, 0, 0, 0)),
                   per_b3((SUBLANES, HEAD_DIM)), per_b3((SUBLANES, LANES))],
        out_shape=[jax.ShapeDtypeStruct((B * T, W_B), out_dtype),
                   jax.ShapeDtypeStruct((B, H_B, HEAD_DIM, HEAD_DIM), F32),
                   jax.ShapeDtypeStruct((B, SUBLANES, HEAD_DIM), F32),
                   jax.ShapeDtypeStruct((B, SUBLANES, LANES), F32)],
        scratch_shapes=[pltpu.VMEM((H_B, HEAD_DIM, HEAD_DIM), F32), pltpu.VMEM((SUBLANES, HEAD_DIM), F32),
                        pltpu.VMEM((SUBLANES, LANES), F32), pltpu.VMEM((L + SUBLANES, W_B), F32)],
        compiler_params=_cparams(("arbitrary", "arbitrary")),
        name="mlstm",
    )(proj, proj, proj, proj, proj, proj, gates, conv0, conv_w, conv_b, bgate, gn_b, c0, n0, m0)


def _hgrn_kernel(q_ref, f_ref, i_ref, g_ref, lb_ref, gn_ref, s0_ref, y_ref, sout_ref,
                 st_s, k_s, b_s, v_s, q_s, o_s, *, LC, LS):
    c = pl.program_id(1)
    last = pl.num_programs(1) - 1

    @pl.when(c == 0)
    def _():
        for h in range(H_C):
            st_s[h] = s0_ref[h].T

    lb = lb_ref[...]
    one_m_lb = 1.0 - lb
    row = lax.broadcasted_iota(jnp.int32, (LC, LC), 0)
    col = lax.broadcasted_iota(jnp.int32, (LC, LC), 1)
    same_sub = (row // LS) == (col // LS)
    intra = same_sub & (row >= col)
    row8 = lax.broadcasted_iota(jnp.int32, (SUBLANES, W_C), 0)

    fc = f_ref[...]
    qc = q_ref[...]
    logf = jnp.log(lb + one_m_lb * _sigmoid(fc))
    kk = one_m_lb * _sigmoid(-fc)
    qq = qc * _sigmoid(qc)
    vv = i_ref[...]
    b = _dot_hi(intra.astype(F32), logf)
    k_s[...] = kk
    b_s[...] = b
    v_s[...] = vv
    q_s[...] = qq

    safe = jnp.min(b) > -HGRN_SAFE_DECAY

    @pl.when(safe)
    def _():
        qe = qq * jnp.exp(b)
        kinv = kk * jnp.exp(-b)
        for h in range(H_C):
            hs = slice(h * HEAD_DIM, (h + 1) * HEAD_DIM)
            a = jnp.where(intra, _dot_nt(qe[:, hs].astype(BF16), kinv[:, hs].astype(BF16)), 0.0)
            o_s[:, hs] = _dot(a.astype(BF16), vv[:, hs].astype(BF16))

    @pl.when(jnp.logical_not(safe))
    def _():
        def exact_sub(sc, carry):
            r = pl.multiple_of(sc * LS, LS)
            for rb in range(LS // SUBLANES):
                t0 = rb * SUBLANES
                qb_ = q_s[pl.ds(r + t0, SUBLANES), :]
                bb_ = b_s[pl.ds(r + t0, SUBLANES), :]
                o_h = [jnp.zeros((SUBLANES, HEAD_DIM), F32) for _ in range(H_C)]
                for s in range(t0 + SUBLANES):
                    d = bb_ - b_s[pl.ds(r + s, 1), :]
                    if s >= t0:
                        d = jnp.where(row8 >= (s - t0), d, -jnp.inf)
                    tmp = qb_ * k_s[pl.ds(r + s, 1), :] * jnp.exp(d)
                    vs = v_s[pl.ds(r + s, 1), :]
                    for h in range(H_C):
                        hs = slice(h * HEAD_DIM, (h + 1) * HEAD_DIM)
                        o_h[h] = o_h[h] + jnp.sum(tmp[:, hs], axis=1, keepdims=True) * vs[:, hs]
                for h in range(H_C):
                    o_s[pl.ds(r + t0, SUBLANES), h * HEAD_DIM:(h + 1) * HEAD_DIM] = o_h[h]
            return carry

        lax.fori_loop(0, LC // LS, exact_sub, 0)

    def sub(sc, carry):
        r = pl.multiple_of(sc * LS, LS)
        bs = b_s[pl.ds(r, LS), :]
        qe = q_s[pl.ds(r, LS), :] * jnp.exp(bs)
        bl = bs[LS - 1:LS, :]
        ke = k_s[pl.ds(r, LS), :] * jnp.exp(bl - bs)
        dec = jnp.exp(bl)
        vs = v_s[pl.ds(r, LS), :]
        gg = g_ref[pl.ds(r, LS), :]
        for h in range(H_C):
            hs = slice(h * HEAD_DIM, (h + 1) * HEAD_DIM)
            st = st_s[h]
            o = _dot_nt(qe[:, hs].astype(BF16), st.astype(BF16)) + o_s[pl.ds(r, LS), hs]
            st_s[h] = dec[:, hs] * st + _dot_tn(vs[:, hs].astype(BF16), ke[:, hs].astype(BF16))
            yn = o * lax.rsqrt(jnp.mean(o * o, axis=1, keepdims=True) + EPS) * gn_ref[:, hs]
            gh = gg[:, hs]
            y_ref[pl.ds(r, LS), hs] = (yn * (gh * _sigmoid(gh))).astype(y_ref.dtype)
        return carry

    lax.fori_loop(0, LC // LS, sub, 0, unroll=True)

    @pl.when(c == last)
    def _():
        for h in range(H_C):
            sout_ref[h] = st_s[h].T


def _hgrn(proj, lb, gn_c, s0, B, T, LC, LS, out_dtype):
    nc = T // LC
    base = (3 * W_A + 4 * W_B) // W_C
    blk = lambda k: pl.BlockSpec((LC, W_C), lambda b, c: (b * nc + c, base + k))
    full2 = lambda shape: pl.BlockSpec(shape, lambda b, c: (0, 0))
    st_spec = pl.BlockSpec((None, H_C, HEAD_DIM, HEAD_DIM), lambda b, c: (b, 0, 0, 0))
    return pl.pallas_call(
        functools.partial(_hgrn_kernel, LC=LC, LS=LS),
        grid=(B, nc),
        in_specs=[blk(0), blk(1), blk(2), blk(3), full2((1, W_C)), full2((1, W_C)), st_spec],
        out_specs=[pl.BlockSpec((LC, W_C), lambda b, c: (b * nc + c, 0)), st_spec],
        out_shape=[jax.ShapeDtypeStruct((B * T, W_C), out_dtype),
                   jax.ShapeDtypeStruct((B, H_C, HEAD_DIM, HEAD_DIM), F32)],
        scratch_shapes=[pltpu.VMEM((H_C, HEAD_DIM, HEAD_DIM), F32)] + [pltpu.VMEM((LC, W_C), F32)] * 5,
        compiler_params=_cparams(("arbitrary", "arbitrary")),
        name="hgrn2",
    )(proj, proj, proj, proj, lb, gn_c, s0)


def _outproj_kernel(ya_ref, yb_ref, yc_ref, x_ref, w_ref, g_ref, b_ref, hf_ref, hb_ref, *, alpha):
    half = x_ref.shape[0] // 2
    for r in (slice(0, half), slice(half, 2 * half)):
        cat = jnp.concatenate([ya_ref[r, :].astype(BF16), yb_ref[r, :].astype(BF16), yc_ref[r, :].astype(BF16)],
                              axis=1)
        h = _layer_norm(alpha * x_ref[r, :] + _dot(cat, w_ref[...]), g_ref[...], b_ref[...])
        hf_ref[r, :] = h
        hb_ref[r, :] = h.astype(BF16)


def _outproj(ya, yb, yc, x, w, layer, g, b, alpha, tm):
    M, D = x.shape
    rows = lambda n: pl.BlockSpec((tm, n), lambda i: (i, 0))
    full = lambda shape: pl.BlockSpec(shape, lambda i: (0, 0))
    w_spec = pl.BlockSpec((None,) + w.shape[1:], lambda i: (layer, 0, 0))
    return pl.pallas_call(
        functools.partial(_outproj_kernel, alpha=alpha),
        grid=(M // tm,),
        in_specs=[rows(W_A), rows(W_B), rows(W_C), rows(D), w_spec, full((1, D)), full((1, D))],
        out_specs=[rows(D), rows(D)],
        out_shape=[jax.ShapeDtypeStruct((M, D), F32), jax.ShapeDtypeStruct((M, D), BF16)],
        compiler_params=_cparams(("arbitrary",)),
        name="out_proj_ln1",
    )(ya, yb, yc, x, w, g, b)


def _mlp_body(hb_ref, hf_ref, wu_ref, wd_ref, g_ref, b_ref, of_ref, ob_ref, acc_s, alpha, side_work=None):
    f = pl.program_id(1)

    @pl.when(f == 0)
    def _():
        acc_s[...] = jnp.zeros(acc_s.shape, F32)

    if side_work is not None:
        side_work()
    hb = hf_ref[...].astype(BF16) if hb_ref is None else hb_ref[...]
    u = jnp.maximum(_dot(hb, wu_ref[...]), 0.0)
    acc_s[...] += _dot((u * u).astype(BF16), wd_ref[...])

    @pl.when(f == pl.num_programs(1) - 1)
    def _():
        o = _layer_norm(alpha * hf_ref[...] + acc_s[...], g_ref[...], b_ref[...])
        of_ref[...] = o
        ob_ref[...] = o.astype(BF16)


def _mlp_kernel(hb_ref, hf_ref, wu_ref, wd_ref, g_ref, b_ref, of_ref, ob_ref, acc_s, *, alpha):
    _mlp_body(hb_ref, hf_ref, wu_ref, wd_ref, g_ref, b_ref, of_ref, ob_ref, acc_s, alpha)


def _mlp_kmean_kernel(pt_ref, hf_ref, wu_ref, wd_ref, g_ref, b_ref, ck_ref, of_ref, ob_ref, km_ref,
                      acc_s, pbuf, sem, *, alpha, cache_layer, pages):
    step = pl.program_id(0) * pl.num_programs(1) + pl.program_id(1)
    n_steps = pl.num_programs(0) * pl.num_programs(1)
    n_pages = pt_ref.shape[1]
    cur = step % 2

    def copies(st, slot):
        first = st * pages
        return [pltpu.make_async_copy(ck_ref.at[cache_layer, pt_ref[first // n_pages, first % n_pages + u]],
                                      pbuf.at[slot, u], sem.at[slot]) for u in range(pages)]

    @pl.when(step == 0)
    def _():
        for c in copies(0, 0):
            c.start()

    @pl.when(step + 1 < n_steps)
    def _():
        for c in copies(step + 1, 1 - cur):
            c.start()

    def page_sums():
        for c in copies(step, cur):
            c.wait()
        per_blk = MOBA_BLOCK // PAGE_SIZE
        for u in range(pages // per_blk):
            tot = pbuf[cur, per_blk * u].sum(axis=0)
            for e in range(1, per_blk):
                tot = tot + pbuf[cur, per_blk * u + e].sum(axis=0)
            km_ref[u] = tot * (1.0 / MOBA_BLOCK)

    _mlp_body(None, hf_ref, wu_ref, wd_ref, g_ref, b_ref, of_ref, ob_ref, acc_s, alpha, page_sums)


def _mlp(hb, hf, wu, wd, layer, g, b, alpha, tm, tf, kmean_job=None):
    M, D = hf.shape
    FF = wu.shape[2]
    grid = (M // tm, FF // tf)
    rows = lambda: pl.BlockSpec((tm, D), lambda i, f, *_: (i, 0))
    vec = lambda: pl.BlockSpec((1, D), lambda i, f, *_: (0, 0))
    in_specs = [rows(), rows(), pl.BlockSpec((None, D, tf), lambda i, f, *_: (layer, 0, f)),
                pl.BlockSpec((None, tf, D), lambda i, f, *_: (layer, f, 0)), vec(), vec()]
    out_shape = [jax.ShapeDtypeStruct((M, D), F32), jax.ShapeDtypeStruct((M, D), BF16)]
    acc = pltpu.VMEM((tm, D), F32)
    if kmean_job is None:
        return pl.pallas_call(
            functools.partial(_mlp_kernel, alpha=alpha),
            grid=grid, in_specs=in_specs, out_specs=[rows(), rows()], out_shape=out_shape,
            scratch_shapes=[acc],
            compiler_params=_cparams(("arbitrary", "arbitrary")),
            name="mlp_ln2",
        )(hb, hf, wu, wd, g, b)
    cache_k, page_table, cache_layer = kmean_job
    B, n_pages = page_table.shape
    n_steps = grid[0] * grid[1]
    per_blk = MOBA_BLOCK // PAGE_SIZE
    pages = B * n_pages // n_steps
    assert pages * n_steps == B * n_pages and pages % per_blk == 0 and n_pages % pages == 0
    nf = grid[1]
    steps_per_seq = n_pages // pages
    km_spec = pl.BlockSpec((None, pages // per_blk, H_A, HEAD_DIM),
                           lambda i, f, *_: ((i * nf + f) // steps_per_seq, (i * nf + f) % steps_per_seq, 0, 0))
    return pl.pallas_call(
        functools.partial(_mlp_kmean_kernel, alpha=alpha, cache_layer=cache_layer, pages=pages),
        grid_spec=pltpu.PrefetchScalarGridSpec(
            num_scalar_prefetch=1, grid=grid,
            in_specs=in_specs[1:] + [pl.BlockSpec(memory_space=pl.ANY)],
            out_specs=[rows(), rows(), km_spec],
            scratch_shapes=[acc, pltpu.VMEM((2, pages, PAGE_SIZE, H_A, HEAD_DIM), F32),
                            pltpu.SemaphoreType.DMA((2,))]),
        out_shape=out_shape + [jax.ShapeDtypeStruct((B, n_pages // per_blk, H_A, HEAD_DIM), F32)],
        compiler_params=_cparams(("arbitrary", "arbitrary")),
        name="mlp_ln2_kmean",
    )(page_table, hf, wu, wd, g, b, cache_k)


def _sample_select_kernel(q_ref, km_ref, o_ref, *, nb):
    out = jnp.zeros(o_ref.shape, jnp.int32)
    lane_o = lax.broadcasted_iota(jnp.int32, o_ref.shape, 1)
    for h in range(H_A):
        hs = slice(h * HEAD_DIM, (h + 1) * HEAD_DIM)
        gate = _dot_nt_hi(q_ref[:, hs], km_ref[:, hs])
        rank, lane = _topk_select(gate, nb, nb)
        for slot in range(MOBA_TOPK):
            pick = (lane < nb) & (rank == slot)
            idx = jnp.sum(jnp.where(pick, lane, 0), axis=1, keepdims=True)
            out = jnp.where(lane_o == h * 4 + slot, idx, out)
    o_ref[...] = out


def _sample_select(proj_s, kmean_pad, B, T, nb):
    return pl.pallas_call(
        functools.partial(_sample_select_kernel, nb=nb),
        grid=(B,),
        in_specs=[pl.BlockSpec((T, W_A), lambda b: (b, 0)),
                  pl.BlockSpec((None, LANES, W_A), lambda b: (b, 0, 0))],
        out_specs=pl.BlockSpec((T, LANES), lambda b: (b, 0)),
        out_shape=jax.ShapeDtypeStruct((B * T, LANES), jnp.int32),
        compiler_params=_cparams(("arbitrary",)),
        name="moba_sample_select",
    )(proj_s, kmean_pad)


def _moba_sample_kernel(sel_ref, pt_ref, rb_ref, q_ref, kn_ref, vn_ref, ck_ref, cv_ref, o_ref,
                        kbuf, vbuf, sem, *, T, past, layer):
    b = pl.program_id(0)
    h = pl.program_id(1)
    nh = pl.num_programs(1)
    step = b * nh + h
    n_steps = pl.num_programs(0) * nh
    per_blk = MOBA_BLOCK // PAGE_SIZE

    def copies(bb, hh, buf, qi, slot, e):
        blk = sel_ref[bb * T + qi, hh * 4 + slot]
        page = pt_ref[bb, blk * per_blk + e]
        idx = qi * MOBA_TOPK + slot
        dst = pl.ds(e * PAGE_SIZE, PAGE_SIZE)
        return (pltpu.make_async_copy(ck_ref.at[layer, page, :, hh, :], kbuf.at[buf, idx, dst, :], sem.at[buf, 0]),
                pltpu.make_async_copy(cv_ref.at[layer, page, :, hh, :], vbuf.at[buf, idx, dst, :], sem.at[buf, 1]))

    def for_all_copies(bb, hh, buf, fn):
        for qi in range(T):
            for slot in range(MOBA_TOPK):
                for e in range(per_blk):
                    for c in copies(bb, hh, buf, qi, slot, e):
                        fn(c)

    cur = step % 2

    @pl.when(step == 0)
    def _():
        for_all_copies(b, h, 0, lambda c: c.start())

    @pl.when(step + 1 < n_steps)
    def _():
        nxt = step + 1
        for_all_copies(nxt // nh, nxt % nh, 1 - cur, lambda c: c.start())

    q = q_ref[...]
    qb = q.astype(BF16)
    scale = HEAD_DIM ** -0.5
    rowT = lax.broadcasted_iota(jnp.int32, (T, T), 0)
    colT = lax.broadcasted_iota(jnp.int32, (T, T), 1)
    s_own = _dot_nt(qb, kn_ref[...].astype(BF16)) * scale + _t5_bias_from_dist(rowT - colT, rb_ref, h)
    s_own = jnp.where(rowT >= colT, s_own, NEG_BIG)

    for_all_copies(b, h, cur, lambda c: c.wait())

    rowB = lax.broadcasted_iota(jnp.int32, (T, MOBA_BLOCK), 0)
    colB = lax.broadcasted_iota(jnp.int32, (T, MOBA_BLOCK), 1)
    row1 = lax.broadcasted_iota(jnp.int32, (T, 1), 0)
    s_slot = []
    for slot in range(MOBA_TOPK):
        s = jnp.zeros((T, MOBA_BLOCK), F32)
        blk_col = jnp.zeros((T, 1), jnp.int32)
        for qi in range(T):
            sq = _dot_nt(qb, kbuf[cur, qi * MOBA_TOPK + slot].astype(BF16))
            s = jnp.where(rowB == qi, sq, s)
            blk_col = jnp.where(row1 == qi, sel_ref[b * T + qi, h * 4 + slot], blk_col)
        dist = past + rowB - (blk_col * MOBA_BLOCK + colB)
        s_slot.append(s * scale + _t5_bias_from_dist(dist, rb_ref, h))

    m = jnp.max(s_own, axis=1, keepdims=True)
    for s in s_slot:
        m = jnp.maximum(m, jnp.max(s, axis=1, keepdims=True))
    p_own = jnp.exp(s_own - m)
    l = jnp.sum(p_own, axis=1, keepdims=True)
    acc = _dot(p_own.astype(BF16), vn_ref[...].astype(BF16))
    for slot in range(MOBA_TOPK):
        p = jnp.exp(s_slot[slot] - m)
        l = l + jnp.sum(p, axis=1, keepdims=True)
        for qi in range(T):
            pq = jnp.where(rowB == qi, p, 0.0).astype(BF16)
            acc = acc + _dot(pq, vbuf[cur, qi * MOBA_TOPK + slot].astype(BF16))
    o_ref[...] = acc / l


def _moba_sample(sel, page_table, rel_bias, proj_s, cache_k, cache_v, B, T, past, layer):
    n_slots = T * MOBA_TOPK
    return pl.pallas_call(
        functools.partial(_moba_sample_kernel, T=T, past=past, layer=layer),
        grid_spec=pltpu.PrefetchScalarGridSpec(
            num_scalar_prefetch=3,
            grid=(B, H_A),
            in_specs=[pl.BlockSpec((T, HEAD_DIM), lambda b, h, *_: (b, h)),
                      pl.BlockSpec((T, HEAD_DIM), lambda b, h, *_: (b, H_A + h)),
                      pl.BlockSpec((T, HEAD_DIM), lambda b, h, *_: (b, 2 * H_A + h)),
                      pl.BlockSpec(memory_space=pl.ANY),
                      pl.BlockSpec(memory_space=pl.ANY)],
            out_specs=pl.BlockSpec((T, HEAD_DIM), lambda b, h, *_: (b, h)),
            scratch_shapes=[pltpu.VMEM((2, n_slots, MOBA_BLOCK, HEAD_DIM), F32),
                            pltpu.VMEM((2, n_slots, MOBA_BLOCK, HEAD_DIM), F32),
                            pltpu.SemaphoreType.DMA((2, 2))],
        ),
        out_shape=jax.ShapeDtypeStruct((B * T, W_A), F32),
        compiler_params=_cparams(("arbitrary", "arbitrary")),
        name="moba_sample",
    )(sel, page_table, rel_bias, proj_s, proj_s, proj_s, cache_k, cache_v)


def _pad_rows(a, rows):
    return jnp.pad(a, ((0, 0), (0, rows - a.shape[1])) + ((0, 0),) * (a.ndim - 2))


def _mixer_states_in(c0, n0, m0, conv0):
    B = c0.shape[0]
    n0p = _pad_rows(n0, SUBLANES)
    m0p = _pad_rows(jnp.broadcast_to(m0[:, :, None], (B, H_B, LANES)), SUBLANES)
    conv0p = jnp.pad(conv0, ((0, 0), (SUBLANES - (CONV_W - 1), 0), (0, 0)))
    return c0, n0p, m0p, conv0p


def _tile(m, pref):
    return pref if m % pref == 0 else m


def _layer(x_f32, x_bf16, B, T, layer, wts, states, attn_fn, mlstm_chunk, hgrn_chunk, kmean_job=None):
    (w_main, w_gate, bgate, conv_w, conv_b, gn_b, gn_c, lb, w_out, ln1_g, ln1_b, w_up, w_down,
     ln2_g, ln2_b, alpha) = wts
    c0, n0, m0, conv0, s0 = states
    M = B * T
    tm = _tile(M, 2048)
    tn = 512
    proj = _matmul(x_bf16, w_main, layer, tm, tn, 0, N_MAIN // tn)
    gates = _matmul(x_bf16, w_gate, layer, tm, LANES, 0, 1)
    ya, k_new, v_new = attn_fn(proj)
    c0, n0p, m0p, conv0p = _mixer_states_in(c0, n0, m0, conv0)
    y_dtype = BF16 if T % 16 == 0 else F32
    yb, c_new, n_new, m_new = _mlstm(proj, gates, conv0p, conv_w, conv_b, bgate, gn_b, c0, n0p, m0p, B, T,
                                     mlstm_chunk, y_dtype)
    yc, s_new = _hgrn(proj, lb, gn_c, s0, B, T, hgrn_chunk, min(HGRN_SUB, hgrn_chunk), y_dtype)
    tm2 = _tile(M, 512)
    hf, hb = _outproj(ya, yb, yc, x_f32, w_out, layer, ln1_g, ln1_b, alpha, tm2)
    of, ob, *km = _mlp(hb, hf, w_up, w_down, layer, ln2_g, ln2_b, alpha, tm2, 1024, kmean_job)
    conv_new = proj.reshape(B, T, N_MAIN)[:, T - (CONV_W - 1):, 3 * W_A:3 * W_A + 2 * W_B]
    return of, ob, (k_new, v_new, c_new, n_new[:, :H_B, :], m_new[:, :H_B, 0], conv_new, s_new), km


def kernel(x_prompt, x_sample, cache_k, cache_v, page_table, state_b_C, state_b_n, state_b_m, state_b_conv,
           state_c_S, w_in, b_gate, conv_w, conv_b, gn_b, gn_c, lower_bounds, rel_bias, w_out, ln1_g, ln1_b,
           w_up, w_down, ln2_g, ln2_b):
    depth = w_in.shape[0]
    Bp, Tp, D = x_prompt.shape
    Bs, Ts, _ = x_sample.shape
    n_pages = page_table.shape[1]
    past = n_pages * PAGE_SIZE
    alpha = (2 * depth) ** 0.25

    sm = jax.nn.softmax(lower_bounds.astype(F32), axis=0)
    lb_all = jnp.cumsum(sm, axis=0) - sm[0]

    w_main = jnp.concatenate([w_in[:, :, :GATE_COL0], w_in[:, :, GATE_COL0 + 2 * H_B:]], axis=-1).astype(BF16)
    w_gate = jnp.pad(w_in[:, :, GATE_COL0:GATE_COL0 + 2 * H_B], ((0, 0), (0, 0), (0, LANES - 2 * H_B))).astype(BF16)
    bgate = jnp.pad(b_gate, ((0, 0), (0, LANES - 2 * H_B)))[:, None, :]
    w_out_b = w_out.astype(BF16)
    w_up_b = w_up.astype(BF16)
    w_down_b = w_down.astype(BF16)

    bias_tab = _bias_table(rel_bias, Tp // MOBA_BLOCK)
    nb_past = past // MOBA_BLOCK

    zeros_p = (jnp.zeros((Bp, H_B, HEAD_DIM, HEAD_DIM), F32), jnp.zeros((Bp, H_B, HEAD_DIM), F32),
               jnp.zeros((Bp, H_B), F32), jnp.zeros((Bp, CONV_W - 1, 2 * W_B), F32),
               jnp.zeros((Bp, H_C, HEAD_DIM, HEAD_DIM), F32))

    xp_f = x_prompt.reshape(Bp * Tp, D)
    xs_f = x_sample.reshape(Bs * Ts, D)
    xp_b = xp_f.astype(BF16)
    xs_b = xs_f.astype(BF16)
    outs = [[] for _ in range(14)]
    mlstm_chunk_p = math.gcd(Tp, 256)
    hgrn_chunk_p = math.gcd(Tp, 256)
    for l in range(depth):
        wts = (w_main, w_gate, bgate[l], conv_w[l], conv_b[l][None, :], gn_b[l][None, :], gn_c[l][None, :],
               lb_all[l][None, :], w_out_b, ln1_g[l][None, :], ln1_b[l][None, :], w_up_b, w_down_b,
               ln2_g[l][None, :], ln2_b[l][None, :], alpha)

        attn_p = lambda proj: _moba_prompt(proj, bias_tab, Bp, Tp)
        xp_f, xp_b, (kp, vp, Cp, nP, mP, cP, SP), (kmean,) = _layer(
            xp_f, xp_b, Bp, Tp, l, wts, zeros_p, attn_p, mlstm_chunk_p, hgrn_chunk_p, (cache_k, page_table, l))
        kmean_pad = jnp.pad(kmean.reshape(Bs, nb_past, W_A), ((0, 0), (0, LANES - nb_past), (0, 0)))

        def attn_s(proj, l=l, kmean_pad=kmean_pad):
            sel = _sample_select(proj, kmean_pad, Bs, Ts, nb_past)
            ya = _moba_sample(sel, page_table, rel_bias, proj, cache_k, cache_v, Bs, Ts, past, l)
            kv = proj[:, W_A:3 * W_A].reshape(Bs, Ts, 2, H_A, HEAD_DIM)
            return ya, kv[:, :, 0], kv[:, :, 1]

        st_s = (state_b_C[l], state_b_n[l], state_b_m[l], state_b_conv[l], state_c_S[l])
        xs_f, xs_b, (ks, vs, Cs, nS, mS, cS, SS), _ = _layer(xs_f, xs_b, Bs, Ts, l, wts, st_s, attn_s, Ts, Ts)

        for lst, val in zip(outs, (kp, vp, ks, vs, Cp, nP, mP, cP, Cs, nS, mS, cS, SP, SS)):
            lst.append(val)

    return (xp_f.reshape(Bp, Tp, D), xs_f.reshape(Bs, Ts, D)) + tuple(jnp.stack(o) for o in outs)
```

```python
import functools
import math

import numpy as np
import jax
import jax.numpy as jnp
from jax import lax
from jax.experimental import pallas as pl
from jax.experimental.pallas import tpu as pltpu

F32 = jnp.float32
BF16 = jnp.bfloat16

HEAD_DIM = 128
H_A, H_B, H_C = 8, 4, 4
W_A, W_B, W_C = H_A * HEAD_DIM, H_B * HEAD_DIM, H_C * HEAD_DIM
MOBA_BLOCK = 256
MOBA_TOPK = 3
NUM_BUCKETS = 32
MAX_DISTANCE = 2048
CONV_W = 4
EPS = 1e-5
GATE_MASK = -1e30
NEG_BIG = -1e30
LOG2E = math.log2(math.e)
PAGE_SIZE = 128
LANES = 128
SUBLANES = 8
HGRN_SUB = 32
HGRN_SAFE_DECAY = 60.0
MOBA_GROUP = 4
QB = 4
VT_EXTRA = 16
VMEM_LIMIT = 56 * 1024 * 1024

N_MAIN = 3 * W_A + 4 * W_B + 4 * W_C
GATE_COL0 = 3 * W_A + 3 * W_B


def _t5_thresholds():
    max_exact = NUM_BUCKETS // 2
    n = np.arange(1, 4 * MAX_DISTANCE, dtype=np.float32)
    large = max_exact + (np.log(n / np.float32(max_exact)) / np.float32(math.log(MAX_DISTANCE / max_exact))
                         * np.float32(NUM_BUCKETS - max_exact)).astype(np.int32)
    large = np.minimum(large, NUM_BUCKETS - 1)
    thr = []
    for b in range(max_exact + 1, NUM_BUCKETS):
        thr.append(int(np.argmax(large >= b)) + 1)
    return tuple(thr)


T5_THRESHOLDS = _t5_thresholds()


def _cparams(sem):
    return pltpu.CompilerParams(dimension_semantics=sem, vmem_limit_bytes=VMEM_LIMIT)


def _dot(a, b):
    return jnp.dot(a, b, preferred_element_type=F32)


def _dot_nt(a, b):
    return lax.dot_general(a, b, (((1,), (1,)), ((), ())), preferred_element_type=F32)


def _dot_tn(a, b):
    return lax.dot_general(a, b, (((0,), (0,)), ((), ())), preferred_element_type=F32)


def _dot_hi(a, b):
    return jnp.dot(a, b, precision=lax.Precision.HIGHEST, preferred_element_type=F32)


def _dot_nt_hi(a, b):
    return lax.dot_general(a, b, (((1,), (1,)), ((), ())), precision=lax.Precision.HIGHEST,
                           preferred_element_type=F32)


def _sigmoid(x):
    return 1.0 / (1.0 + jnp.exp(-x))


def _layer_norm(z, g, b):
    mu = jnp.mean(z, axis=-1, keepdims=True)
    zc = z - mu
    var = jnp.mean(zc * zc, axis=-1, keepdims=True)
    return zc * lax.rsqrt(var + EPS) * g + b


def _matmul_kernel(x_ref, w_ref, o_ref):
    o_ref[...] = _dot(x_ref[...], w_ref[...])


def _matmul(x, w, layer, tm, tn, col_blk0, n_blk):
    M, K = x.shape
    N = n_blk * tn
    return pl.pallas_call(
        _matmul_kernel,
        grid=(M // tm, n_blk),
        in_specs=[pl.BlockSpec((tm, K), lambda i, j: (i, 0)),
                  pl.BlockSpec((None, K, tn), lambda i, j: (layer, 0, col_blk0 + j))],
        out_specs=pl.BlockSpec((tm, tn), lambda i, j: (i, j)),
        out_shape=jax.ShapeDtypeStruct((M, N), F32),
        compiler_params=_cparams(("arbitrary", "arbitrary")),
        name="in_proj",
    )(x, w)


def _t5_bias_from_dist(dist, rb_ref, h):
    n = jnp.maximum(dist, 0)
    large = jnp.full(n.shape, NUM_BUCKETS // 2, jnp.int32)
    for thr in T5_THRESHOLDS:
        large = large + (n >= thr).astype(jnp.int32)
    bucket = jnp.where(n < NUM_BUCKETS // 2, n, large)
    val = jnp.zeros(n.shape, F32)
    for b in range(NUM_BUCKETS):
        val = jnp.where(bucket == b, rb_ref[b, h], val)
    return val


def _bias_table_kernel(rb_ref, o_ref, *, nb):
    h = pl.program_id(0)
    blk = MOBA_BLOCK
    d_const = -(-(T5_THRESHOLDS[-1] - 1) // blk) + 1
    row = lax.broadcasted_iota(jnp.int32, (blk, blk), 0)
    col = lax.broadcasted_iota(jnp.int32, (blk, blk), 1)
    for e in range(2 * nb - 1):
        d = nb - 1 - e
        if d >= d_const:
            tile = jnp.full((blk, blk), rb_ref[NUM_BUCKETS - 1, h], F32)
        elif d < 0:
            tile = jnp.full((blk, blk), rb_ref[0, h], F32)
        else:
            tile = _t5_bias_from_dist(d * blk + col - row, rb_ref, h)
        o_ref[e * blk:(e + 1) * blk, :] = tile * LOG2E


def _bias_table(rel_bias, nb):
    ne = 2 * nb - 1
    return pl.pallas_call(
        functools.partial(_bias_table_kernel, nb=nb),
        grid=(H_A,),
        in_specs=[pl.BlockSpec(memory_space=pltpu.SMEM)],
        out_specs=pl.BlockSpec((None, ne * MOBA_BLOCK, MOBA_BLOCK), lambda h: (h, 0, 0)),
        out_shape=jax.ShapeDtypeStruct((H_A, ne * MOBA_BLOCK, MOBA_BLOCK), F32),
        compiler_params=_cparams(("arbitrary",)),
        name="t5_bias_table",
    )(rel_bias)


def _topk_select(gate, n_valid, n_cand):
    lane = lax.broadcasted_iota(jnp.int32, gate.shape, 1)
    gm = jnp.where(lane < n_valid, gate, GATE_MASK)
    rank = jnp.zeros(gate.shape, jnp.int32)
    for c in range(n_cand):
        gc = gm[:, c:c + 1]
        ahead = (gc > gm) | ((gc == gm) & (c < lane))
        rank = rank + ahead.astype(jnp.int32)
    return rank, lane


def _moba_prompt_kernel(q_ref, k_ref, v_ref, bias_ref, o_ref, kout_ref, vout_ref,
                        ka_s, vt_s, vtd_s, qa_s, kmean_s, sem, *, nb):
    b = pl.program_id(0)
    h = pl.program_id(1)
    i = pl.program_id(2)
    blk = MOBA_BLOCK
    G = MOBA_GROUP
    nbp = -(-nb // SUBLANES) * SUBLANES

    kv_copies = (pltpu.make_async_copy(k_ref, kout_ref.at[b, :, h, :], sem.at[0]),
                 pltpu.make_async_copy(v_ref, vout_ref.at[b, :, h, :], sem.at[1]))

    @pl.when(i == 0)
    def _():
        for c in kv_copies:
            c.start()
        kmean_s[...] = jnp.zeros(kmean_s.shape, F32)
        lane = lax.broadcasted_iota(jnp.int32, (blk, HEAD_DIM), 1)
        for n in range(nb):
            rows = slice((n % G) * blk, (n % G + 1) * blk)
            kf = k_ref[n * blk:(n + 1) * blk, :]
            ka_s[n // G, rows, 0:HEAD_DIM] = kf.astype(BF16)
            ka_s[n // G, rows, HEAD_DIM:2 * HEAD_DIM] = jnp.where(lane == n, NEG_BIG, 0.0).astype(BF16)
            ones_row = jnp.where(lax.broadcasted_iota(jnp.int32, (VT_EXTRA, blk), 0) == 0, 1.0, 0.0)
            vt = jnp.concatenate([v_ref[n * blk:(n + 1) * blk, :].T, ones_row], axis=0).astype(BF16)
            vt_s[n // G, :, rows] = vt
            vtd_s[n] = vt
            kmean_s[n:n + 1, :] = jnp.mean(kf, axis=0, keepdims=True)
        km = kmean_s[...]
        sub = lax.broadcasted_iota(jnp.int32, (nbp, blk), 0)
        pad = jnp.zeros((HEAD_DIM - nbp, blk), F32)
        for t in range(nb):
            q = q_ref[t * blk:(t + 1) * blk, :]
            if t > MOBA_TOPK:
                gm = jnp.where(sub < t, _dot_nt_hi(km, q), GATE_MASK)
                rank = jnp.zeros((nbp, blk), jnp.int32)
                for c in range(t):
                    gc = gm[c:c + 1, :]
                    rank = rank + ((gc > gm) | ((gc == gm) & (c < sub))).astype(jnp.int32)
                notsel = jnp.where((sub < t) & (rank < MOBA_TOPK), 0.0, 1.0)
            else:
                notsel = jnp.where(sub < t, 0.0, 1.0)
            qt = (q * (HEAD_DIM ** -0.5 * LOG2E)).T
            qa_s[t // QB, :, (t % QB) * blk:(t % QB + 1) * blk] = (
                jnp.concatenate([qt, notsel, pad], axis=0).astype(BF16))

    qaug = qa_s[i]
    rowk = lax.broadcasted_iota(jnp.int32, (blk, blk), 0)
    colq = lax.broadcasted_iota(jnp.int32, (blk, blk), 1)

    def tile(n_grp):
        m_parts, acc_parts = [], []
        for u in range(QB):
            iu = i * QB + u
            kd = ka_s[iu // G, pl.ds(pl.multiple_of((iu % G) * blk, blk), blk), 0:HEAD_DIM]
            sd = _dot(kd, qaug[0:HEAD_DIM, u * blk:(u + 1) * blk]) + bias_ref[(nb - 1) * blk:nb * blk, :]
            sd = jnp.where(colq >= rowk, sd, NEG_BIG)
            mu = jnp.max(sd, axis=0, keepdims=True)
            m_parts.append(mu)
            acc_parts.append(_dot(vtd_s[iu], jnp.exp2(sd - mu).astype(BF16)))
        m = jnp.concatenate(m_parts, axis=1)
        acc = jnp.concatenate(acc_parts, axis=1)
        for g in range(n_grp - 1):
            bias = jnp.concatenate(
                [bias_ref[pl.ds(pl.multiple_of((nb - 1 - (i * QB + u) + G * g) * blk, blk), G * blk), :]
                 for u in range(QB)], axis=1)
            s = _dot(ka_s[g], qaug) + bias
            m_new = jnp.maximum(m, jnp.max(s, axis=0, keepdims=True))
            acc = jnp.exp2(m - m_new) * acc + _dot(vt_s[g], jnp.exp2(s - m_new).astype(BF16))
            m = m_new
        for a in range(G - 1):
            c0 = (a + 1) * blk
            bias = jnp.concatenate([bias_ref[(nb - 1 - (u - a)) * blk:(nb - (u - a)) * blk, :]
                                    for u in range(a + 1, QB)], axis=1)
            s = _dot(ka_s[i, a * blk:(a + 1) * blk, :], qaug[:, c0:]) + bias
            m_t = m[:, c0:]
            m_new = jnp.maximum(m_t, jnp.max(s, axis=0, keepdims=True))
            acc_t = jnp.exp2(m_t - m_new) * acc[:, c0:] + _dot(vtd_s[i * G + a], jnp.exp2(s - m_new).astype(BF16))
            m = jnp.concatenate([m[:, :c0], m_new], axis=1)
            acc = jnp.concatenate([acc[:, :c0], acc_t], axis=1)
        o_ref[...] = (acc[0:HEAD_DIM] / acc[HEAD_DIM:HEAD_DIM + 1]).T.astype(o_ref.dtype)

    for grp in range(nb // G):
        pl.when(i == grp)(functools.partial(tile, grp + 1))

    @pl.when(i == nb // QB - 1)
    def _():
        for c in kv_copies:
            c.wait()


def _moba_prompt(proj, bias_tab, B, T):
    nb = T // MOBA_BLOCK
    G = MOBA_GROUP
    assert nb % G == 0 and G == QB and nb >= 2 * G, "a step's query blocks are exactly one key group"
    nq = nb // QB
    kv_shape = jax.ShapeDtypeStruct((B, T, H_A, HEAD_DIM), F32)
    return pl.pallas_call(
        functools.partial(_moba_prompt_kernel, nb=nb),
        grid=(B, H_A, nq),
        in_specs=[pl.BlockSpec((T, HEAD_DIM), lambda b, h, i: (b, h)),
                  pl.BlockSpec((T, HEAD_DIM), lambda b, h, i: (b, H_A + h)),
                  pl.BlockSpec((T, HEAD_DIM), lambda b, h, i: (b, 2 * H_A + h)),
                  pl.BlockSpec((None, (2 * nb - 1) * MOBA_BLOCK, MOBA_BLOCK), lambda b, h, i: (h, 0, 0))],
        out_specs=[pl.BlockSpec((QB * MOBA_BLOCK, HEAD_DIM), lambda b, h, i: (b * nq + i, h)),
                   pl.BlockSpec(memory_space=pl.ANY), pl.BlockSpec(memory_space=pl.ANY)],
        out_shape=[jax.ShapeDtypeStruct((B * T, W_A), BF16), kv_shape, kv_shape],
        scratch_shapes=[pltpu.VMEM((nb // G, G * MOBA_BLOCK, 2 * HEAD_DIM), BF16),
                        pltpu.VMEM((nb // G, HEAD_DIM + VT_EXTRA, G * MOBA_BLOCK), BF16),
                        pltpu.VMEM((nb, HEAD_DIM + VT_EXTRA, MOBA_BLOCK), BF16),
                        pltpu.VMEM((nq, 2 * HEAD_DIM, QB * MOBA_BLOCK), BF16),
                        pltpu.VMEM((-(-nb // SUBLANES) * SUBLANES, HEAD_DIM), F32),
                        pltpu.SemaphoreType.DMA((2,))],
        compiler_params=_cparams(("arbitrary", "arbitrary", "arbitrary")),
        name="moba_prompt",
    )(proj, proj, proj, bias_tab)


def _mlstm_kernel(q_ref, k_ref, qp_ref, kp_ref, v_ref, og_ref, g_ref, conv0_ref, cw_ref, cb_ref, bg_ref,
                  gn_ref, c0_ref, n0_ref, m0_ref,
                  y_ref, cout_ref, nout_ref, mout_ref,
                  c_s, n_s, m_s, ext_s, *, L):
    c = pl.program_id(1)
    last = pl.num_programs(1) - 1

    @pl.when(c == 0)
    def _():
        c_s[...] = c0_ref[...]
        n_s[...] = n0_ref[...]
        m_s[...] = m0_ref[...]

    def conv_silu(u_ref, up_ref, col0):
        u = u_ref[...]
        tail = jnp.where(c == 0, conv0_ref[:, col0:col0 + W_B], up_ref[L - SUBLANES:L, :])
        ext_s[0:SUBLANES, :] = tail
        ext_s[SUBLANES:SUBLANES + L, :] = u
        acc = u * cw_ref[CONV_W - 1:CONV_W, col0:col0 + W_B] + cb_ref[:, col0:col0 + W_B]
        for j in range(1, CONV_W):
            xj = ext_s[SUBLANES - j:SUBLANES - j + L, :]
            acc = acc + xj * cw_ref[CONV_W - 1 - j:CONV_W - j, col0:col0 + W_B]
        return acc * _sigmoid(acc)

    qc = conv_silu(q_ref, qp_ref, 0)
    kc = conv_silu(k_ref, kp_ref, W_B) * (HEAD_DIM ** -0.5)
    v = v_ref[...]
    og = og_ref[...]

    g = g_ref[...] + bg_ref[...]
    lf = jnp.minimum(g, 0.0) - jnp.log(1.0 + jnp.exp(-jnp.abs(g)))
    row = lax.broadcasted_iota(jnp.int32, (L, L), 0)
    col = lax.broadcasted_iota(jnp.int32, (L, L), 1)
    causal = row >= col
    fcum = _dot_hi(causal.astype(F32), lf)

    for h in range(H_B):
        hs = slice(h * HEAD_DIM, (h + 1) * HEAD_DIM)
        fcol = fcum[:, H_B + h:H_B + h + 1]
        rcol = g[:, h:h + 1] - fcol
        rrow = jnp.sum(jnp.where(row == col, rcol, 0.0), axis=0, keepdims=True)
        dm = jnp.where(causal, fcol + rrow, -jnp.inf)
        mprev = m_s[h:h + 1, 0:1]
        gcol = fcol + mprev
        mt = jnp.maximum(gcol, jnp.max(dm, axis=1, keepdims=True))
        w = jnp.exp(dm - mt)
        wg = jnp.exp(gcol - mt)
        qh = qc[:, hs]
        kh = kc[:, hs]
        vh = v[:, hs]
        qhb = qh.astype(BF16)
        s = _dot_nt(qhb, kh.astype(BF16)) * w
        num = _dot(s.astype(BF16), vh.astype(BF16)) + wg * _dot(qhb, c_s[h].astype(BF16))
        den = jnp.sum(s, axis=1, keepdims=True) + wg * jnp.sum(qh * n_s[h:h + 1, :], axis=1, keepdims=True)
        hh = num / jnp.maximum(jnp.abs(den), jnp.exp(-mt))
        ml = mt[L - 1:L, :]
        wl = jnp.exp(fcol[L - 1:L, :] + rcol - ml)
        gl = jnp.exp(gcol[L - 1:L, :] - ml)
        kw = kh * wl
        c_s[h] = gl * c_s[h] + _dot_tn(kw.astype(BF16), vh.astype(BF16))
        n_s[h:h + 1, :] = gl * n_s[h:h + 1, :] + jnp.sum(kw, axis=0, keepdims=True)
        m_s[h:h + 1, :] = jnp.broadcast_to(ml, (1, LANES))
        hc = hh - jnp.mean(hh, axis=1, keepdims=True)
        yn = hc * lax.rsqrt(jnp.mean(hc * hc, axis=1, keepdims=True) + EPS) * gn_ref[:, hs]
        y_ref[:, hs] = (_sigmoid(og[:, hs]) * yn).astype(y_ref.dtype)

    @pl.when(c == last)
    def _():
        cout_ref[...] = c_s[...]
        nout_ref[...] = n_s[...]
        mout_ref[...] = m_s[...]


def _mlstm(proj, gates, conv0, conv_w, conv_b, bgate, gn_b, c0, n0, m0, B, T, L, out_dtype):
    nc = T // L
    q_blk = 3 * W_A // W_B
    k_blk, v_blk, o_blk = q_blk + 1, q_blk + 2, q_blk + 3

    def cur(colblk):
        return pl.BlockSpec((L, W_B), lambda b, c: (b * nc + c, colblk))

    def prev(colblk):
        return pl.BlockSpec((L, W_B), lambda b, c: (b * nc + jnp.maximum(c - 1, 0), colblk))

    full2 = lambda shape: pl.BlockSpec(shape, lambda b, c: (0, 0))
    per_b3 = lambda shape: pl.BlockSpec((None,) + shape, lambda b, c: (b, 0, 0))
    return pl.pallas_call(
        functools.partial(_mlstm_kernel, L=L),
        grid=(B, nc),
        in_specs=[cur(q_blk), cur(k_blk), prev(q_blk), prev(k_blk), cur(v_blk), cur(o_blk),
                  pl.BlockSpec((L, LANES), lambda b, c: (b * nc + c, 0)),
                  per_b3((SUBLANES, 2 * W_B)),
                  full2((CONV_W, 2 * W_B)), full2((1, 2 * W_B)), full2((1, LANES)), full2((1, W_B)),
                  pl.BlockSpec((None, H_B, HEAD_DIM, HEAD_DIM), lambda b, c: (b, 0, 0, 0)),
                  per_b3((SUBLANES, HEAD_DIM)), per_b3((SUBLANES, LANES))],
        out_specs=[pl.BlockSpec((L, W_B), lambda b, c: (b * nc + c, 0)),
                   pl.BlockSpec((None, H_B, HEAD_DIM, HEAD_DIM), lambda b, c: (b, 0, 0, 0)),
                   per_b3((SUBLANES, HEAD_DIM)), per_b3((SUBLANES, LANES))],
        out_shape=[jax.ShapeDtypeStruct((B * T, W_B), out_dtype),
                   jax.ShapeDtypeStruct((B, H_B, HEAD_DIM, HEAD_DIM), F32),
                   jax.ShapeDtypeStruct((B, SUBLANES, HEAD_DIM), F32),
                   jax.ShapeDtypeStruct((B, SUBLANES, LANES), F32)],
        scratch_shapes=[pltpu.VMEM((H_B, HEAD_DIM, HEAD_DIM), F32), pltpu.VMEM((SUBLANES, HEAD_DIM), F32),
                        pltpu.VMEM((SUBLANES, LANES), F32), pltpu.VMEM((L + SUBLANES, W_B), F32)],
        compiler_params=_cparams(("arbitrary", "arbitrary")),
        name="mlstm",
    )(proj, proj, proj, proj, proj, proj, gates, conv0, conv_w, conv_b, bgate, gn_b, c0, n0, m0)


def _hgrn_kernel(q_ref, f_ref, i_ref, g_ref, lb_ref, gn_ref, s0_ref, y_ref, sout_ref,
                 st_s, k_s, b_s, v_s, q_s, o_s, *, LC, LS):
    c = pl.program_id(1)
    last = pl.num_programs(1) - 1

    @pl.when(c == 0)
    def _():
        for h in range(H_C):
            st_s[h] = s0_ref[h].T

    lb = lb_ref[...]
    one_m_lb = 1.0 - lb
    row = lax.broadcasted_iota(jnp.int32, (LC, LC), 0)
    col = lax.broadcasted_iota(jnp.int32, (LC, LC), 1)
    same_sub = (row // LS) == (col // LS)
    intra = same_sub & (row >= col)
    row8 = lax.broadcasted_iota(jnp.int32, (SUBLANES, W_C), 0)

    fc = f_ref[...]
    qc = q_ref[...]
    logf = jnp.log(lb + one_m_lb * _sigmoid(fc))
    kk = one_m_lb * _sigmoid(-fc)
    qq = qc * _sigmoid(qc)
    vv = i_ref[...]
    b = _dot_hi(intra.astype(F32), logf)
    k_s[...] = kk
    b_s[...] = b
    v_s[...] = vv
    q_s[...] = qq

    safe = jnp.min(b) > -HGRN_SAFE_DECAY

    @pl.when(safe)
    def _():
        qe = qq * jnp.exp(b)
        kinv = kk * jnp.exp(-b)
        for h in range(H_C):
            hs = slice(h * HEAD_DIM, (h + 1) * HEAD_DIM)
            a = jnp.where(intra, _dot_nt(qe[:, hs].astype(BF16), kinv[:, hs].astype(BF16)), 0.0)
            o_s[:, hs] = _dot(a.astype(BF16), vv[:, hs].astype(BF16))

    @pl.when(jnp.logical_not(safe))
    def _():
        def exact_sub(sc, carry):
            r = pl.multiple_of(sc * LS, LS)
            for rb in range(LS // SUBLANES):
                t0 = rb * SUBLANES
                qb_ = q_s[pl.ds(r + t0, SUBLANES), :]
                bb_ = b_s[pl.ds(r + t0, SUBLANES), :]
                o_h = [jnp.zeros((SUBLANES, HEAD_DIM), F32) for _ in range(H_C)]
                for s in range(t0 + SUBLANES):
                    d = bb_ - b_s[pl.ds(r + s, 1), :]
                    if s >= t0:
                        d = jnp.where(row8 >= (s - t0), d, -jnp.inf)
                    tmp = qb_ * k_s[pl.ds(r + s, 1), :] * jnp.exp(d)
                    vs = v_s[pl.ds(r + s, 1), :]
                    for h in range(H_C):
                        hs = slice(h * HEAD_DIM, (h + 1) * HEAD_DIM)
                        o_h[h] = o_h[h] + jnp.sum(tmp[:, hs], axis=1, keepdims=True) * vs[:, hs]
                for h in range(H_C):
                    o_s[pl.ds(r + t0, SUBLANES), h * HEAD_DIM:(h + 1) * HEAD_DIM] = o_h[h]
            return carry

        lax.fori_loop(0, LC // LS, exact_sub, 0)

    def sub(sc, carry):
        r = pl.multiple_of(sc * LS, LS)
        bs = b_s[pl.ds(r, LS), :]
        qe = q_s[pl.ds(r, LS), :] * jnp.exp(bs)
        bl = bs[LS - 1:LS, :]
        ke = k_s[pl.ds(r, LS), :] * jnp.exp(bl - bs)
        dec = jnp.exp(bl)
        vs = v_s[pl.ds(r, LS), :]
        gg = g_ref[pl.ds(r, LS), :]
        for h in range(H_C):
            hs = slice(h * HEAD_DIM, (h + 1) * HEAD_DIM)
            st = st_s[h]
            o = _dot_nt(qe[:, hs].astype(BF16), st.astype(BF16)) + o_s[pl.ds(r, LS), hs]
            st_s[h] = dec[:, hs] * st + _dot_tn(vs[:, hs].astype(BF16), ke[:, hs].astype(BF16))
            yn = o * lax.rsqrt(jnp.mean(o * o, axis=1, keepdims=True) + EPS) * gn_ref[:, hs]
            gh = gg[:, hs]
            y_ref[pl.ds(r, LS), hs] = (yn * (gh * _sigmoid(gh))).astype(y_ref.dtype)
        return carry

    lax.fori_loop(0, LC // LS, sub, 0, unroll=True)

    @pl.when(c == last)
    def _():
        for h in range(H_C):
            sout_ref[h] = st_s[h].T


def _hgrn(proj, lb, gn_c, s0, B, T, LC, LS, out_dtype):
    nc = T // LC
    base = (3 * W_A + 4 * W_B) // W_C
    blk = lambda k: pl.BlockSpec((LC, W_C), lambda b, c: (b * nc + c, base + k))
    full2 = lambda shape: pl.BlockSpec(shape, lambda b, c: (0, 0))
    st_spec = pl.BlockSpec((None, H_C, HEAD_DIM, HEAD_DIM), lambda b, c: (b, 0, 0, 0))
    return pl.pallas_call(
        functools.partial(_hgrn_kernel, LC=LC, LS=LS),
        grid=(B, nc),
        in_specs=[blk(0), blk(1), blk(2), blk(3), full2((1, W_C)), full2((1, W_C)), st_spec],
        out_specs=[pl.BlockSpec((LC, W_C), lambda b, c: (b * nc + c, 0)), st_spec],
        out_shape=[jax.ShapeDtypeStruct((B * T, W_C), out_dtype),
                   jax.ShapeDtypeStruct((B, H_C, HEAD_DIM, HEAD_DIM), F32)],
        scratch_shapes=[pltpu.VMEM((H_C, HEAD_DIM, HEAD_DIM), F32)] + [pltpu.VMEM((LC, W_C), F32)] * 5,
        compiler_params=_cparams(("arbitrary", "arbitrary")),
        name="hgrn2",
    )(proj, proj, proj, proj, lb, gn_c, s0)


def _outproj_kernel(ya_ref, yb_ref, yc_ref, x_ref, w_ref, g_ref, b_ref, hf_ref, hb_ref, *, alpha):
    half = x_ref.shape[0] // 2
    for r in (slice(0, half), slice(half, 2 * half)):
        cat = jnp.concatenate([ya_ref[r, :].astype(BF16), yb_ref[r, :].astype(BF16), yc_ref[r, :].astype(BF16)],
                              axis=1)
        h = _layer_norm(alpha * x_ref[r, :] + _dot(cat, w_ref[...]), g_ref[...], b_ref[...])
        hf_ref[r, :] = h
        hb_ref[r, :] = h.astype(BF16)


def _outproj(ya, yb, yc, x, w, layer, g, b, alpha, tm):
    M, D = x.shape
    rows = lambda n: pl.BlockSpec((tm, n), lambda i: (i, 0))
    full = lambda shape: pl.BlockSpec(shape, lambda i: (0, 0))
    w_spec = pl.BlockSpec((None,) + w.shape[1:], lambda i: (layer, 0, 0))
    return pl.pallas_call(
        functools.partial(_outproj_kernel, alpha=alpha),
        grid=(M // tm,),
        in_specs=[rows(W_A), rows(W_B), rows(W_C), rows(D), w_spec, full((1, D)), full((1, D))],
        out_specs=[rows(D), rows(D)],
        out_shape=[jax.ShapeDtypeStruct((M, D), F32), jax.ShapeDtypeStruct((M, D), BF16)],
        compiler_params=_cparams(("arbitrary",)),
        name="out_proj_ln1",
    )(ya, yb, yc, x, w, g, b)


def _mlp_body(hb_ref, hf_ref, wu_ref, wd_ref, g_ref, b_ref, of_ref, ob_ref, acc_s, alpha, side_work=None):
    f = pl.program_id(1)

    @pl.when(f == 0)
    def _():
        acc_s[...] = jnp.zeros(acc_s.shape, F32)

    if side_work is not None:
        side_work()
    hb = hf_ref[...].astype(BF16) if hb_ref is None else hb_ref[...]
    u = jnp.maximum(_dot(hb, wu_ref[...]), 0.0)
    acc_s[...] += _dot((u * u).astype(BF16), wd_ref[...])

    @pl.when(f == pl.num_programs(1) - 1)
    def _():
        o = _layer_norm(alpha * hf_ref[...] + acc_s[...], g_ref[...], b_ref[...])
        of_ref[...] = o
        ob_ref[...] = o.astype(BF16)


def _mlp_kernel(hb_ref, hf_ref, wu_ref, wd_ref, g_ref, b_ref, of_ref, ob_ref, acc_s, *, alpha):
    _mlp_body(hb_ref, hf_ref, wu_ref, wd_ref, g_ref, b_ref, of_ref, ob_ref, acc_s, alpha)


def _mlp_kmean_kernel(pt_ref, hf_ref, wu_ref, wd_ref, g_ref, b_ref, ck_ref, of_ref, ob_ref, km_ref,
                      acc_s, pbuf, sem, *, alpha, cache_layer, pages):
    step = pl.program_id(0) * pl.num_programs(1) + pl.program_id(1)
    n_steps = pl.num_programs(0) * pl.num_programs(1)
    n_pages = pt_ref.shape[1]
    cur = step % 2

    def copies(st, slot):
        first = st * pages
        return [pltpu.make_async_copy(ck_ref.at[cache_layer, pt_ref[first // n_pages, first % n_pages + u]],
                                      pbuf.at[slot, u], sem.at[slot]) for u in range(pages)]

    @pl.when(step == 0)
    def _():
        for c in copies(0, 0):
            c.start()

    @pl.when(step + 1 < n_steps)
    def _():
        for c in copies(step + 1, 1 - cur):
            c.start()

    def page_sums():
        for c in copies(step, cur):
            c.wait()
        per_blk = MOBA_BLOCK // PAGE_SIZE
        for u in range(pages // per_blk):
            tot = pbuf[cur, per_blk * u].sum(axis=0)
            for e in range(1, per_blk):
                tot = tot + pbuf[cur, per_blk * u + e].sum(axis=0)
            km_ref[u] = tot * (1.0 / MOBA_BLOCK)

    _mlp_body(None, hf_ref, wu_ref, wd_ref, g_ref, b_ref, of_ref, ob_ref, acc_s, alpha, page_sums)


def _mlp(hb, hf, wu, wd, layer, g, b, alpha, tm, tf, kmean_job=None):
    M, D = hf.shape
    FF = wu.shape[2]
    grid = (M // tm, FF // tf)
    rows = lambda: pl.BlockSpec((tm, D), lambda i, f, *_: (i, 0))
    vec = lambda: pl.BlockSpec((1, D), lambda i, f, *_: (0, 0))
    in_specs = [rows(), rows(), pl.BlockSpec((None, D, tf), lambda i, f, *_: (layer, 0, f)),
                pl.BlockSpec((None, tf, D), lambda i, f, *_: (layer, f, 0)), vec(), vec()]
    out_shape = [jax.ShapeDtypeStruct((M, D), F32), jax.ShapeDtypeStruct((M, D), BF16)]
    acc = pltpu.VMEM((tm, D), F32)
    if kmean_job is None:
        return pl.pallas_call(
            functools.partial(_mlp_kernel, alpha=alpha),
            grid=grid, in_specs=in_specs, out_specs=[rows(), rows()], out_shape=out_shape,
            scratch_shapes=[acc],
            compiler_params=_cparams(("arbitrary", "arbitrary")),
            name="mlp_ln2",
        )(hb, hf, wu, wd, g, b)
    cache_k, page_table, cache_layer = kmean_job
    B, n_pages = page_table.shape
    n_steps = grid[0] * grid[1]
    per_blk = MOBA_BLOCK // PAGE_SIZE
    pages = B * n_pages // n_steps
    assert pages * n_steps == B * n_pages and pages % per_blk == 0 and n_pages % pages == 0
    nf = grid[1]
    steps_per_seq = n_pages // pages
    km_spec = pl.BlockSpec((None, pages // per_blk, H_A, HEAD_DIM),
                           lambda i, f, *_: ((i * nf + f) // steps_per_seq, (i * nf + f) % steps_per_seq, 0, 0))
    return pl.pallas_call(
        functools.partial(_mlp_kmean_kernel, alpha=alpha, cache_layer=cache_layer, pages=pages),
        grid_spec=pltpu.PrefetchScalarGridSpec(
            num_scalar_prefetch=1, grid=grid,
            in_specs=in_specs[1:] + [pl.BlockSpec(memory_space=pl.ANY)],
            out_specs=[rows(), rows(), km_spec],
            scratch_shapes=[acc, pltpu.VMEM((2, pages, PAGE_SIZE, H_A, HEAD_DIM), F32),
                            pltpu.SemaphoreType.DMA((2,))]),
        out_shape=out_shape + [jax.ShapeDtypeStruct((B, n_pages // per_blk, H_A, HEAD_DIM), F32)],
        compiler_params=_cparams(("arbitrary", "arbitrary")),
        name="mlp_ln2_kmean",
    )(page_table, hf, wu, wd, g, b, cache_k)


def _sample_select_kernel(q_ref, km_ref, o_ref, *, nb):
    out = jnp.zeros(o_ref.shape, jnp.int32)
    lane_o = lax.broadcasted_iota(jnp.int32, o_ref.shape, 1)
    for h in range(H_A):
        hs = slice(h * HEAD_DIM, (h + 1) * HEAD_DIM)
        gate = _dot_nt_hi(q_ref[:, hs], km_ref[:, hs])
        rank, lane = _topk_select(gate, nb, nb)
        for slot in range(MOBA_TOPK):
            pick = (lane < nb) & (rank == slot)
            idx = jnp.sum(jnp.where(pick, lane, 0), axis=1, keepdims=True)
            out = jnp.where(lane_o == h * 4 + slot, idx, out)
    o_ref[...] = out


def _sample_select(proj_s, kmean_pad, B, T, nb):
    return pl.pallas_call(
        functools.partial(_sample_select_kernel, nb=nb),
        grid=(B,),
        in_specs=[pl.BlockSpec((T, W_A), lambda b: (b, 0)),
                  pl.BlockSpec((None, LANES, W_A), lambda b: (b, 0, 0))],
        out_specs=pl.BlockSpec((T, LANES), lambda b: (b, 0)),
        out_shape=jax.ShapeDtypeStruct((B * T, LANES), jnp.int32),
        compiler_params=_cparams(("arbitrary",)),
        name="moba_sample_select",
    )(proj_s, kmean_pad)


def _moba_sample_kernel(sel_ref, pt_ref, rb_ref, q_ref, kn_ref, vn_ref, ck_ref, cv_ref, o_ref,
                        kbuf, vbuf, sem, *, T, past, layer):
    b = pl.program_id(0)
    h = pl.program_id(1)
    nh = pl.num_programs(1)
    step = b * nh + h
    n_steps = pl.num_programs(0) * nh
    per_blk = MOBA_BLOCK // PAGE_SIZE

    def copies(bb, hh, buf, qi, slot, e):
        blk = sel_ref[bb * T + qi, hh * 4 + slot]
        page = pt_ref[bb, blk * per_blk + e]
        idx = qi * MOBA_TOPK + slot
        dst = pl.ds(e * PAGE_SIZE, PAGE_SIZE)
        return (pltpu.make_async_copy(ck_ref.at[layer, page, :, hh, :], kbuf.at[buf, idx, dst, :], sem.at[buf, 0]),
                pltpu.make_async_copy(cv_ref.at[layer, page, :, hh, :], vbuf.at[buf, idx, dst, :], sem.at[buf, 1]))

    def for_all_copies(bb, hh, buf, fn):
        for qi in range(T):
            for slot in range(MOBA_TOPK):
                for e in range(per_blk):
                    for c in copies(bb, hh, buf, qi, slot, e):
                        fn(c)

    cur = step % 2

    @pl.when(step == 0)
    def _():
        for_all_copies(b, h, 0, lambda c: c.start())

    @pl.when(step + 1 < n_steps)
    def _():
        nxt = step + 1
        for_all_copies(nxt // nh, nxt % nh, 1 - cur, lambda c: c.start())

    q = q_ref[...]
    qb = q.astype(BF16)
    scale = HEAD_DIM ** -0.5
    rowT = lax.broadcasted_iota(jnp.int32, (T, T), 0)
    colT = lax.broadcasted_iota(jnp.int32, (T, T), 1)
    s_own = _dot_nt(qb, kn_ref[...].astype(BF16)) * scale + _t5_bias_from_dist(rowT - colT, rb_ref, h)
    s_own = jnp.where(rowT >= colT, s_own, NEG_BIG)

    for_all_copies(b, h, cur, lambda c: c.wait())

    rowB = lax.broadcasted_iota(jnp.int32, (T, MOBA_BLOCK), 0)
    colB = lax.broadcasted_iota(jnp.int32, (T, MOBA_BLOCK), 1)
    row1 = lax.broadcasted_iota(jnp.int32, (T, 1), 0)
    s_slot = []
    for slot in range(MOBA_TOPK):
        s = jnp.zeros((T, MOBA_BLOCK), F32)
        blk_col = jnp.zeros((T, 1), jnp.int32)
        for qi in range(T):
            sq = _dot_nt(qb, kbuf[cur, qi * MOBA_TOPK + slot].astype(BF16))
            s = jnp.where(rowB == qi, sq, s)
            blk_col = jnp.where(row1 == qi, sel_ref[b * T + qi, h * 4 + slot], blk_col)
        dist = past + rowB - (blk_col * MOBA_BLOCK + colB)
        s_slot.append(s * scale + _t5_bias_from_dist(dist, rb_ref, h))

    m = jnp.max(s_own, axis=1, keepdims=True)
    for s in s_slot:
        m = jnp.maximum(m, jnp.max(s, axis=1, keepdims=True))
    p_own = jnp.exp(s_own - m)
    l = jnp.sum(p_own, axis=1, keepdims=True)
    acc = _dot(p_own.astype(BF16), vn_ref[...].astype(BF16))
    for slot in range(MOBA_TOPK):
        p = jnp.exp(s_slot[slot] - m)
        l = l + jnp.sum(p, axis=1, keepdims=True)
        for qi in range(T):
            pq = jnp.where(rowB == qi, p, 0.0).astype(BF16)
            acc = acc + _dot(pq, vbuf[cur, qi * MOBA_TOPK + slot].astype(BF16))
    o_ref[...] = acc / l


def _moba_sample(sel, page_table, rel_bias, proj_s, cache_k, cache_v, B, T, past, layer):
    n_slots = T * MOBA_TOPK
    return pl.pallas_call(
        functools.partial(_moba_sample_kernel, T=T, past=past, layer=layer),
        grid_spec=pltpu.PrefetchScalarGridSpec(
            num_scalar_prefetch=3,
            grid=(B, H_A),
            in_specs=[pl.BlockSpec((T, HEAD_DIM), lambda b, h, *_: (b, h)),
                      pl.BlockSpec((T, HEAD_DIM), lambda b, h, *_: (b, H_A + h)),
                      pl.BlockSpec((T, HEAD_DIM), lambda b, h, *_: (b, 2 * H_A + h)),
                      pl.BlockSpec(memory_space=pl.ANY),
                      pl.BlockSpec(memory_space=pl.ANY)],
            out_specs=pl.BlockSpec((T, HEAD_DIM), lambda b, h, *_: (b, h)),
            scratch_shapes=[pltpu.VMEM((2, n_slots, MOBA_BLOCK, HEAD_DIM), F32),
                            pltpu.VMEM((2, n_slots, MOBA_BLOCK, HEAD_DIM), F32),
                            pltpu.SemaphoreType.DMA((2, 2))],
        ),
        out_shape=jax.ShapeDtypeStruct((B * T, W_A), F32),
        compiler_params=_cparams(("arbitrary", "arbitrary")),
        name="moba_sample",
    )(sel, page_table, rel_bias, proj_s, proj_s, proj_s, cache_k, cache_v)


def _pad_rows(a, rows):
    return jnp.pad(a, ((0, 0), (0, rows - a.shape[1])) + ((0, 0),) * (a.ndim - 2))


def _mixer_states_in(c0, n0, m0, conv0):
    B = c0.shape[0]
    n0p = _pad_rows(n0, SUBLANES)
    m0p = _pad_rows(jnp.broadcast_to(m0[:, :, None], (B, H_B, LANES)), SUBLANES)
    conv0p = jnp.pad(conv0, ((0, 0), (SUBLANES - (CONV_W - 1), 0), (0, 0)))
    return c0, n0p, m0p, conv0p


def _tile(m, pref):
    return pref if m % pref == 0 else m


def _layer(x_f32, x_bf16, B, T, layer, wts, states, attn_fn, mlstm_chunk, hgrn_chunk, kmean_job=None):
    (w_main, w_gate, bgate, conv_w, conv_b, gn_b, gn_c, lb, w_out, ln1_g, ln1_b, w_up, w_down,
     ln2_g, ln2_b, alpha) = wts
    c0, n0, m0, conv0, s0 = states
    M = B * T
    tm = _tile(M, 2048)
    tn = 512
    proj = _matmul(x_bf16, w_main, layer, tm, tn, 0, N_MAIN // tn)
    gates = _matmul(x_bf16, w_gate, layer, tm, LANES, 0, 1)
    ya, k_new, v_new = attn_fn(proj)
    c0, n0p, m0p, conv0p = _mixer_states_in(c0, n0, m0, conv0)
    y_dtype = BF16 if T % 16 == 0 else F32
    yb, c_new, n_new, m_new = _mlstm(proj, gates, conv0p, conv_w, conv_b, bgate, gn_b, c0, n0p, m0p, B, T,
                                     mlstm_chunk, y_dtype)
    yc, s_new = _hgrn(proj, lb, gn_c, s0, B, T, hgrn_chunk, min(HGRN_SUB, hgrn_chunk), y_dtype)
    tm2 = _tile(M, 512)
    hf, hb = _outproj(ya, yb, yc, x_f32, w_out, layer, ln1_g, ln1_b, alpha, tm2)
    of, ob, *km = _mlp(hb, hf, w_up, w_down, layer, ln2_g, ln2_b, alpha, tm2, 1024, kmean_job)
    conv_new = proj.reshape(B, T, N_MAIN)[:, T - (CONV_W - 1):, 3 * W_A:3 * W_A + 2 * W_B]
    return of, ob, (k_new, v_new, c_new, n_new[:, :H_B, :], m_new[:, :H_B, 0], conv_new, s_new), km


def kernel(x_prompt, x_sample, cache_k, cache_v, page_table, state_b_C, state_b_n, state_b_m, state_b_conv,
           state_c_S, w_in, b_gate, conv_w, conv_b, gn_b, gn_c, lower_bounds, rel_bias, w_out, ln1_g, ln1_b,
           w_up, w_down, ln2_g, ln2_b):
    depth = w_in.shape[0]
    Bp, Tp, D = x_prompt.shape
    Bs, Ts, _ = x_sample.shape
    n_pages = page_table.shape[1]
    past = n_pages * PAGE_SIZE
    alpha = (2 * depth) ** 0.25

    sm = jax.nn.softmax(lower_bounds.astype(F32), axis=0)
    lb_all = jnp.cumsum(sm, axis=0) - sm[0]

    w_main = jnp.concatenate([w_in[:, :, :GATE_COL0], w_in[:, :, GATE_COL0 + 2 * H_B:]], axis=-1).astype(BF16)
    w_gate = jnp.pad(w_in[:, :, GATE_COL0:GATE_COL0 + 2 * H_B], ((0, 0), (0, 0), (0, LANES - 2 * H_B))).astype(BF16)
    bgate = jnp.pad(b_gate, ((0, 0), (0, LANES - 2 * H_B)))[:, None, :]
    w_out_b = w_out.astype(BF16)
    w_up_b = w_up.astype(BF16)
    w_down_b = w_down.astype(BF16)

    bias_tab = _bias_table(rel_bias, Tp // MOBA_BLOCK)
    nb_past = past // MOBA_BLOCK

    zeros_p = (jnp.zeros((Bp, H_B, HEAD_DIM, HEAD_DIM), F32), jnp.zeros((Bp, H_B, HEAD_DIM), F32),
               jnp.zeros((Bp, H_B), F32), jnp.zeros((Bp, CONV_W - 1, 2 * W_B), F32),
               jnp.zeros((Bp, H_C, HEAD_DIM, HEAD_DIM), F32))

    xp_f = x_prompt.reshape(Bp * Tp, D)
    xs_f = x_sample.reshape(Bs * Ts, D)
    xp_b = xp_f.astype(BF16)
    xs_b = xs_f.astype(BF16)
    outs = [[] for _ in range(14)]
    mlstm_chunk_p = math.gcd(Tp, 256)
    hgrn_chunk_p = math.gcd(Tp, 256)
    for l in range(depth):
        wts = (w_main, w_gate, bgate[l], conv_w[l], conv_b[l][None, :], gn_b[l][None, :], gn_c[l][None, :],
               lb_all[l][None, :], w_out_b, ln1_g[l][None, :], ln1_b[l][None, :], w_up_b, w_down_b,
               ln2_g[l][None, :], ln2_b[l][None, :], alpha)

        attn_p = lambda proj: _moba_prompt(proj, bias_tab, Bp, Tp)
        xp_f, xp_b, (kp, vp, Cp, nP, mP, cP, SP), (kmean,) = _layer(
            xp_f, xp_b, Bp, Tp, l, wts, zeros_p, attn_p, mlstm_chunk_p, hgrn_chunk_p, (cache_k, page_table, l))
        kmean_pad = jnp.pad(kmean.reshape(Bs, nb_past, W_A), ((0, 0), (0, LANES - nb_past), (0, 0)))

        def attn_s(proj, l=l, kmean_pad=kmean_pad):
            sel = _sample_select(proj, kmean_pad, Bs, Ts, nb_past)
            ya = _moba_sample(sel, page_table, rel_bias, proj, cache_k, cache_v, Bs, Ts, past, l)
            kv = proj[:, W_A:3 * W_A].reshape(Bs, Ts, 2, H_A, HEAD_DIM)
            return ya, kv[:, :, 0], kv[:, :, 1]

        st_s = (state_b_C[l], state_b_n[l], state_b_m[l], state_b_conv[l], state_c_S[l])
        xs_f, xs_b, (ks, vs, Cs, nS, mS, cS, SS), _ = _layer(xs_f, xs_b, Bs, Ts, l, wts, st_s, attn_s, Ts, Ts)

        for lst, val in zip(outs, (kp, vp, ks, vs, Cp, nP, mP, cP, Cs, nS, mS, cS, SP, SS)):
            lst.append(val)

    return (xp_f.reshape(Bp, Tp, D), xs_f.reshape(Bs, Ts, D)) + tuple(jnp.stack(o) for o in outs)
```

```python
import functools
import math

import numpy as np
import jax
import jax.numpy as jnp
from jax import lax
from jax.experimental import pallas as pl
from jax.experimental.pallas import tpu as pltpu

F32 = jnp.float32
BF16 = jnp.bfloat16

HEAD_DIM = 128
H_A, H_B, H_C = 8, 4, 4
W_A, W_B, W_C = H_A * HEAD_DIM, H_B * HEAD_DIM, H_C * HEAD_DIM
MOBA_BLOCK = 256
MOBA_TOPK = 3
NUM_BUCKETS = 32
MAX_DISTANCE = 2048
CONV_W = 4
EPS = 1e-5
GATE_MASK = -1e30
NEG_BIG = -1e30
LOG2E = math.log2(math.e)
PAGE_SIZE = 128
LANES = 128
SUBLANES = 8
HGRN_SUB = 32
HGRN_SAFE_DECAY = 60.0
MOBA_GROUP = 4
QB = 4
VT_EXTRA = 16
VMEM_LIMIT = 56 * 1024 * 1024

N_MAIN = 3 * W_A + 4 * W_B + 4 * W_C
GATE_COL0 = 3 * W_A + 3 * W_B


def _t5_thresholds():
    max_exact = NUM_BUCKETS // 2
    n = np.arange(1, 4 * MAX_DISTANCE, dtype=np.float32)
    large = max_exact + (np.log(n / np.float32(max_exact)) / np.float32(math.log(MAX_DISTANCE / max_exact))
                         * np.float32(NUM_BUCKETS - max_exact)).astype(np.int32)
    large = np.minimum(large, NUM_BUCKETS - 1)
    thr = []
    for b in range(max_exact + 1, NUM_BUCKETS):
        thr.append(int(np.argmax(large >= b)) + 1)
    return tuple(thr)


T5_THRESHOLDS = _t5_thresholds()


def _cparams(sem):
    return pltpu.CompilerParams(dimension_semantics=sem, vmem_limit_bytes=VMEM_LIMIT)


def _dot(a, b):
    return jnp.dot(a, b, preferred_element_type=F32)


def _dot_nt(a, b):
    return lax.dot_general(a, b, (((1,), (1,)), ((), ())), preferred_element_type=F32)


def _dot_tn(a, b):
    return lax.dot_general(a, b, (((0,), (0,)), ((), ())), preferred_element_type=F32)


def _dot_hi(a, b):
    return jnp.dot(a, b, precision=lax.Precision.HIGHEST, preferred_element_type=F32)


def _dot_nt_hi(a, b):
    return lax.dot_general(a, b, (((1,), (1,)), ((), ())), precision=lax.Precision.HIGHEST,
                           preferred_element_type=F32)


def _sigmoid(x):
    return 1.0 / (1.0 + jnp.exp(-x))


def _layer_norm(z, g, b):
    mu = jnp.mean(z, axis=-1, keepdims=True)
    zc = z - mu
    var = jnp.mean(zc * zc, axis=-1, keepdims=True)
    return zc * lax.rsqrt(var + EPS) * g + b


def _matmul_kernel(x_ref, w_ref, o_ref):
    o_ref[...] = _dot(x_ref[...], w_ref[...])


def _matmul(x, w, layer, tm, tn, col_blk0, n_blk):
    M, K = x.shape
    N = n_blk * tn
    return pl.pallas_call(
        _matmul_kernel,
        grid=(M // tm, n_blk),
        in_specs=[pl.BlockSpec((tm, K), lambda i, j: (i, 0)),
                  pl.BlockSpec((None, K, tn), lambda i, j: (layer, 0, col_blk0 + j))],
        out_specs=pl.BlockSpec((tm, tn), lambda i, j: (i, j)),
        out_shape=jax.ShapeDtypeStruct((M, N), F32),
        compiler_params=_cparams(("arbitrary", "arbitrary")),
        name="in_proj",
    )(x, w)


def _t5_bias_from_dist(dist, rb_ref, h):
    n = jnp.maximum(dist, 0)
    large = jnp.full(n.shape, NUM_BUCKETS // 2, jnp.int32)
    for thr in T5_THRESHOLDS:
        large = large + (n >= thr).astype(jnp.int32)
    bucket = jnp.where(n < NUM_BUCKETS // 2, n, large)
    val = jnp.zeros(n.shape, F32)
    for b in range(NUM_BUCKETS):
        val = jnp.where(bucket == b, rb_ref[b, h], val)
    return val


def _bias_table_kernel(rb_ref, o_ref, *, nb):
    h = pl.program_id(0)
    blk = MOBA_BLOCK
    d_const = -(-(T5_THRESHOLDS[-1] - 1) // blk) + 1
    row = lax.broadcasted_iota(jnp.int32, (blk, blk), 0)
    col = lax.broadcasted_iota(jnp.int32, (blk, blk), 1)
    for e in range(2 * nb - 1):
        d = nb - 1 - e
        if d >= d_const:
            tile = jnp.full((blk, blk), rb_ref[NUM_BUCKETS - 1, h], F32)
        elif d < 0:
            tile = jnp.full((blk, blk), rb_ref[0, h], F32)
        else:
            tile = _t5_bias_from_dist(d * blk + col - row, rb_ref, h)
        o_ref[e * blk:(e + 1) * blk, :] = tile * LOG2E


def _bias_table(rel_bias, nb):
    ne = 2 * nb - 1
    return pl.pallas_call(
        functools.partial(_bias_table_kernel, nb=nb),
        grid=(H_A,),
        in_specs=[pl.BlockSpec(memory_space=pltpu.SMEM)],
        out_specs=pl.BlockSpec((None, ne * MOBA_BLOCK, MOBA_BLOCK), lambda h: (h, 0, 0)),
        out_shape=jax.ShapeDtypeStruct((H_A, ne * MOBA_BLOCK, MOBA_BLOCK), F32),
        compiler_params=_cparams(("arbitrary",)),
        name="t5_bias_table",
    )(rel_bias)


def _topk_select(gate, n_valid, n_cand):
    lane = lax.broadcasted_iota(jnp.int32, gate.shape, 1)
    gm = jnp.where(lane < n_valid, gate, GATE_MASK)
    rank = jnp.zeros(gate.shape, jnp.int32)
    for c in range(n_cand):
        gc = gm[:, c:c + 1]
        ahead = (gc > gm) | ((gc == gm) & (c < lane))
        rank = rank + ahead.astype(jnp.int32)
    return rank, lane


def _moba_prompt_kernel(q_ref, k_ref, v_ref, bias_ref, o_ref, kout_ref, vout_ref,
                        ka_s, vt_s, vtd_s, qa_s, kmean_s, sem, *, nb):
    b = pl.program_id(0)
    h = pl.program_id(1)
    i = pl.program_id(2)
    blk = MOBA_BLOCK
    G = MOBA_GROUP
    nbp = -(-nb // SUBLANES) * SUBLANES

    kv_copies = (pltpu.make_async_copy(k_ref, kout_ref.at[b, :, h, :], sem.at[0]),
                 pltpu.make_async_copy(v_ref, vout_ref.at[b, :, h, :], sem.at[1]))

    @pl.when(i == 0)
    def _():
        for c in kv_copies:
            c.start()
        kmean_s[...] = jnp.zeros(kmean_s.shape, F32)
        lane = lax.broadcasted_iota(jnp.int32, (blk, HEAD_DIM), 1)
        for n in range(nb):
            rows = slice((n % G) * blk, (n % G + 1) * blk)
            kf = k_ref[n * blk:(n + 1) * blk, :]
            ka_s[n // G, rows, 0:HEAD_DIM] = kf.astype(BF16)
            ka_s[n // G, rows, HEAD_DIM:2 * HEAD_DIM] = jnp.where(lane == n, NEG_BIG, 0.0).astype(BF16)
            ones_row = jnp.where(lax.broadcasted_iota(jnp.int32, (VT_EXTRA, blk), 0) == 0, 1.0, 0.0)
            vt = jnp.concatenate([v_ref[n * blk:(n + 1) * blk, :].T, ones_row], axis=0).astype(BF16)
            vt_s[n // G, :, rows] = vt
            vtd_s[n] = vt
            kmean_s[n:n + 1, :] = jnp.mean(kf, axis=0, keepdims=True)
        km = kmean_s[...]
        sub = lax.broadcasted_iota(jnp.int32, (nbp, blk), 0)
        pad = jnp.zeros((HEAD_DIM - nbp, blk), F32)
        for t in range(nb):
            q = q_ref[t * blk:(t + 1) * blk, :]
            if t > MOBA_TOPK:
                gm = jnp.where(sub < t, _dot_nt_hi(km, q), GATE_MASK)
                rank = jnp.zeros((nbp, blk), jnp.int32)
                for c in range(t):
                    gc = gm[c:c + 1, :]
                    rank = rank + ((gc > gm) | ((gc == gm) & (c < sub))).astype(jnp.int32)
                notsel = jnp.where((sub < t) & (rank < MOBA_TOPK), 0.0, 1.0)
            else:
                notsel = jnp.where(sub < t, 0.0, 1.0)
            qt = (q * (HEAD_DIM ** -0.5 * LOG2E)).T
            qa_s[t // QB, :, (t % QB) * blk:(t % QB + 1) * blk] = (
                jnp.concatenate([qt, notsel, pad], axis=0).astype(BF16))

    qaug = qa_s[i]
    rowk = lax.broadcasted_iota(jnp.int32, (blk, blk), 0)
    colq = lax.broadcasted_iota(jnp.int32, (blk, blk), 1)

    def tile(n_grp):
        m_parts, acc_parts = [], []
        for u in range(QB):
            iu = i * QB + u
            kd = ka_s[iu // G, pl.ds(pl.multiple_of((iu % G) * blk, blk), blk), 0:HEAD_DIM]
            sd = _dot(kd, qaug[0:HEAD_DIM, u * blk:(u + 1) * blk]) + bias_ref[(nb - 1) * blk:nb * blk, :]
            sd = jnp.where(colq >= rowk, sd, NEG_BIG)
            mu = jnp.max(sd, axis=0, keepdims=True)
            m_parts.append(mu)
            acc_parts.append(_dot(vtd_s[iu], jnp.exp2(sd - mu).astype(BF16)))
        m = jnp.concatenate(m_parts, axis=1)
        acc = jnp.concatenate(acc_parts, axis=1)
        for g in range(n_grp - 1):
            bias = jnp.concatenate(
                [bias_ref[pl.ds(pl.multiple_of((nb - 1 - (i * QB + u) + G * g) * blk, blk), G * blk), :]
                 for u in range(QB)], axis=1)
            s = _dot(ka_s[g], qaug) + bias
            m_new = jnp.maximum(m, jnp.max(s, axis=0, keepdims=True))
            acc = jnp.exp2(m - m_new) * acc + _dot(vt_s[g], jnp.exp2(s - m_new).astype(BF16))
            m = m_new
        for a in range(G - 1):
            c0 = (a + 1) * blk
            bias = jnp.concatenate([bias_ref[(nb - 1 - (u - a)) * blk:(nb - (u - a)) * blk, :]
                                    for u in range(a + 1, QB)], axis=1)
            s = _dot(ka_s[i, a * blk:(a + 1) * blk, :], qaug[:, c0:]) + bias
            m_t = m[:, c0:]
            m_new = jnp.maximum(m_t, jnp.max(s, axis=0, keepdims=True))
            acc_t = jnp.exp2(m_t - m_new) * acc[:, c0:] + _dot(vtd_s[i * G + a], jnp.exp2(s - m_new).astype(BF16))
            m = jnp.concatenate([m[:, :c0], m_new], axis=1)
            acc = jnp.concatenate([acc[:, :c0], acc_t], axis=1)
        o_ref[...] = (acc[0:HEAD_DIM] / acc[HEAD_DIM:HEAD_DIM + 1]).T.astype(o_ref.dtype)

    for grp in range(nb // G):
        pl.when(i == grp)(functools.partial(tile, grp + 1))

    @pl.when(i == nb // QB - 1)
    def _():
        for c in kv_copies:
            c.wait()


def _moba_prompt(proj, bias_tab, B, T):
    nb = T // MOBA_BLOCK
    G = MOBA_GROUP
    assert nb % G == 0 and G == QB and nb >= 2 * G, "a step's query blocks are exactly one key group"
    nq = nb // QB
    kv_shape = jax.ShapeDtypeStruct((B, T, H_A, HEAD_DIM), F32)
    return pl.pallas_call(
        functools.partial(_moba_prompt_kernel, nb=nb),
        grid=(B, H_A, nq),
        in_specs=[pl.BlockSpec((T, HEAD_DIM), lambda b, h, i: (b, h)),
                  pl.BlockSpec((T, HEAD_DIM), lambda b, h, i: (b, H_A + h)),
                  pl.BlockSpec((T, HEAD_DIM), lambda b, h, i: (b, 2 * H_A + h)),
                  pl.BlockSpec((None, (2 * nb - 1) * MOBA_BLOCK, MOBA_BLOCK), lambda b, h, i: (h, 0, 0))],
        out_specs=[pl.BlockSpec((QB * MOBA_BLOCK, HEAD_DIM), lambda b, h, i: (b * nq + i, h)),
                   pl.BlockSpec(memory_space=pl.ANY), pl.BlockSpec(memory_space=pl.ANY)],
        out_shape=[jax.ShapeDtypeStruct((B * T, W_A), BF16), kv_shape, kv_shape],
        scratch_shapes=[pltpu.VMEM((nb // G, G * MOBA_BLOCK, 2 * HEAD_DIM), BF16),
                        pltpu.VMEM((nb // G, HEAD_DIM + VT_EXTRA, G * MOBA_BLOCK), BF16),
                        pltpu.VMEM((nb, HEAD_DIM + VT_EXTRA, MOBA_BLOCK), BF16),
                        pltpu.VMEM((nq, 2 * HEAD_DIM, QB * MOBA_BLOCK), BF16),
                        pltpu.VMEM((-(-nb // SUBLANES) * SUBLANES, HEAD_DIM), F32),
                        pltpu.SemaphoreType.DMA((2,))],
        compiler_params=_cparams(("arbitrary", "arbitrary", "arbitrary")),
        name="moba_prompt",
    )(proj, proj, proj, bias_tab)


def _mlstm_kernel(q_ref, k_ref, qp_ref, kp_ref, v_ref, og_ref, x_ref, wg_ref, conv0_ref, cw_ref, cb_ref, bg_ref,
                  gn_ref, c0_ref, n0_ref, m0_ref,
                  y_ref, cout_ref, nout_ref, mout_ref,
                  c_s, n_s, m_s, ext_s, *, L):
    c = pl.program_id(1)
    last = pl.num_programs(1) - 1

    @pl.when(c == 0)
    def _():
        c_s[...] = c0_ref[...]
        n_s[...] = n0_ref[...]
        m_s[...] = m0_ref[...]

    def conv_silu(u_ref, up_ref, col0):
        u = u_ref[...]
        tail = jnp.where(c == 0, conv0_ref[:, col0:col0 + W_B], up_ref[L - SUBLANES:L, :])
        ext_s[0:SUBLANES, :] = tail
        ext_s[SUBLANES:SUBLANES + L, :] = u
        acc = u * cw_ref[CONV_W - 1:CONV_W, col0:col0 + W_B] + cb_ref[:, col0:col0 + W_B]
        for j in range(1, CONV_W):
            xj = ext_s[SUBLANES - j:SUBLANES - j + L, :]
            acc = acc + xj * cw_ref[CONV_W - 1 - j:CONV_W - j, col0:col0 + W_B]
        return acc * _sigmoid(acc)

    qc = conv_silu(q_ref, qp_ref, 0)
    kc = conv_silu(k_ref, kp_ref, W_B) * (HEAD_DIM ** -0.5)
    v = v_ref[...]
    og = og_ref[...]

    g = _dot(x_ref[...].astype(BF16), wg_ref[...]) + bg_ref[...]
    lf = jnp.minimum(g, 0.0) - jnp.log(1.0 + jnp.exp(-jnp.abs(g)))
    row = lax.broadcasted_iota(jnp.int32, (L, L), 0)
    col = lax.broadcasted_iota(jnp.int32, (L, L), 1)
    causal = row >= col
    fcum = _dot_hi(causal.astype(F32), lf)

    for h in range(H_B):
        hs = slice(h * HEAD_DIM, (h + 1) * HEAD_DIM)
        fcol = fcum[:, H_B + h:H_B + h + 1]
        rcol = g[:, h:h + 1] - fcol
        rrow = jnp.sum(jnp.where(row == col, rcol, 0.0), axis=0, keepdims=True)
        dm = jnp.where(causal, fcol + rrow, -jnp.inf)
        mprev = m_s[h:h + 1, 0:1]
        gcol = fcol + mprev
        mt = jnp.maximum(gcol, jnp.max(dm, axis=1, keepdims=True))
        w = jnp.exp(dm - mt)
        wg = jnp.exp(gcol - mt)
        qh = qc[:, hs]
        kh = kc[:, hs]
        vh = v[:, hs]
        qhb = qh.astype(BF16)
        s = _dot_nt(qhb, kh.astype(BF16)) * w
        num = _dot(s.astype(BF16), vh.astype(BF16)) + wg * _dot(qhb, c_s[h].astype(BF16))
        den = jnp.sum(s, axis=1, keepdims=True) + wg * jnp.sum(qh * n_s[h:h + 1, :], axis=1, keepdims=True)
        hh = num / jnp.maximum(jnp.abs(den), jnp.exp(-mt))
        ml = mt[L - 1:L, :]
        wl = jnp.exp(fcol[L - 1:L, :] + rcol - ml)
        gl = jnp.exp(gcol[L - 1:L, :] - ml)
        kw = kh * wl
        c_s[h] = gl * c_s[h] + _dot_tn(kw.astype(BF16), vh.astype(BF16))
        n_s[h:h + 1, :] = gl * n_s[h:h + 1, :] + jnp.sum(kw, axis=0, keepdims=True)
        m_s[h:h + 1, :] = jnp.broadcast_to(ml, (1, LANES))
        hc = hh - jnp.mean(hh, axis=1, keepdims=True)
        yn = hc * lax.rsqrt(jnp.mean(hc * hc, axis=1, keepdims=True) + EPS) * gn_ref[:, hs]
        y_ref[:, hs] = (_sigmoid(og[:, hs]) * yn).astype(y_ref.dtype)

    @pl.when(c == last)
    def _():
        cout_ref[...] = c_s[...]
        nout_ref[...] = n_s[...]
        mout_ref[...] = m_s[...]


def _mlstm(proj, x, w_gate, layer, conv0, conv_w, conv_b, bgate, gn_b, c0, n0, m0, B, T, L, out_dtype):
    nc = T // L
    q_blk = 3 * W_A // W_B
    k_blk, v_blk, o_blk = q_blk + 1, q_blk + 2, q_blk + 3

    def cur(colblk):
        return pl.BlockSpec((L, W_B), lambda b, c: (b * nc + c, colblk))

    def prev(colblk):
        return pl.BlockSpec((L, W_B), lambda b, c: (b * nc + jnp.maximum(c - 1, 0), colblk))

    full2 = lambda shape: pl.BlockSpec(shape, lambda b, c: (0, 0))
    per_b3 = lambda shape: pl.BlockSpec((None,) + shape, lambda b, c: (b, 0, 0))
    return pl.pallas_call(
        functools.partial(_mlstm_kernel, L=L),
        grid=(B, nc),
        in_specs=[cur(q_blk), cur(k_blk), prev(q_blk), prev(k_blk), cur(v_blk), cur(o_blk),
                  pl.BlockSpec((L, x.shape[1]), lambda b, c: (b * nc + c, 0)),
                  pl.BlockSpec((None,) + w_gate.shape[1:], lambda b, c: (layer, 0, 0)),
                  per_b3((SUBLANES, 2 * W_B)),
                  full2((CONV_W, 2 * W_B)), full2((1, 2 * W_B)), full2((1, LANES)), full2((1, W_B)),
                  pl.BlockSpec((None, H_B, HEAD_DIM, HEAD_DIM), lambda b, c: (b, 0, 0, 0)),
                  per_b3((SUBLANES, HEAD_DIM)), per_b3((SUBLANES, LANES))],
        out_specs=[pl.BlockSpec((L, W_B), lambda b, c: (b * nc + c, 0)),
                   pl.BlockSpec((None, H_B, HEAD_DIM, HEAD_DIM), lambda b, c: (b, 0, 0, 0)),
                   per_b3((SUBLANES, HEAD_DIM)), per_b3((SUBLANES, LANES))],
        out_shape=[jax.ShapeDtypeStruct((B * T, W_B), out_dtype),
                   jax.ShapeDtypeStruct((B, H_B, HEAD_DIM, HEAD_DIM), F32),
                   jax.ShapeDtypeStruct((B, SUBLANES, HEAD_DIM), F32),
                   jax.ShapeDtypeStruct((B, SUBLANES, LANES), F32)],
        scratch_shapes=[pltpu.VMEM((H_B, HEAD_DIM, HEAD_DIM), F32), pltpu.VMEM((SUBLANES, HEAD_DIM), F32),
                        pltpu.VMEM((SUBLANES, LANES), F32), pltpu.VMEM((L + SUBLANES, W_B), F32)],
        compiler_params=_cparams(("arbitrary", "arbitrary")),
        name="mlstm",
    )(proj, proj, proj, proj, proj, proj, x, w_gate, conv0, conv_w, conv_b, bgate, gn_b, c0, n0, m0)


def _hgrn_kernel(q_ref, f_ref, i_ref, g_ref, lb_ref, gn_ref, s0_ref, y_ref, sout_ref,
                 st_s, k_s, b_s, v_s, q_s, o_s, *, LC, LS):
    c = pl.program_id(1)
    last = pl.num_programs(1) - 1

    @pl.when(c == 0)
    def _():
        for h in range(H_C):
            st_s[h] = s0_ref[h].T

    lb = lb_ref[...]
    one_m_lb = 1.0 - lb
    row = lax.broadcasted_iota(jnp.int32, (LC, LC), 0)
    col = lax.broadcasted_iota(jnp.int32, (LC, LC), 1)
    same_sub = (row // LS) == (col // LS)
    intra = same_sub & (row >= col)
    row8 = lax.broadcasted_iota(jnp.int32, (SUBLANES, W_C), 0)

    fc = f_ref[...]
    qc = q_ref[...]
    logf = jnp.log(lb + one_m_lb * _sigmoid(fc))
    kk = one_m_lb * _sigmoid(-fc)
    qq = qc * _sigmoid(qc)
    vv = i_ref[...]
    b = _dot_hi(intra.astype(F32), logf)
    k_s[...] = kk
    b_s[...] = b
    v_s[...] = vv
    q_s[...] = qq

    safe = jnp.min(b) > -HGRN_SAFE_DECAY

    @pl.when(safe)
    def _():
        qe = qq * jnp.exp(b)
        kinv = kk * jnp.exp(-b)
        for h in range(H_C):
            hs = slice(h * HEAD_DIM, (h + 1) * HEAD_DIM)
            a = jnp.where(intra, _dot_nt(qe[:, hs].astype(BF16), kinv[:, hs].astype(BF16)), 0.0)
            o_s[:, hs] = _dot(a.astype(BF16), vv[:, hs].astype(BF16))

    @pl.when(jnp.logical_not(safe))
    def _():
        def exact_sub(sc, carry):
            r = pl.multiple_of(sc * LS, LS)
            for rb in range(LS // SUBLANES):
                t0 = rb * SUBLANES
                qb_ = q_s[pl.ds(r + t0, SUBLANES), :]
                bb_ = b_s[pl.ds(r + t0, SUBLANES), :]
                o_h = [jnp.zeros((SUBLANES, HEAD_DIM), F32) for _ in range(H_C)]
                for s in range(t0 + SUBLANES):
                    d = bb_ - b_s[pl.ds(r + s, 1), :]
                    if s >= t0:
                        d = jnp.where(row8 >= (s - t0), d, -jnp.inf)
                    tmp = qb_ * k_s[pl.ds(r + s, 1), :] * jnp.exp(d)
                    vs = v_s[pl.ds(r + s, 1), :]
                    for h in range(H_C):
                        hs = slice(h * HEAD_DIM, (h + 1) * HEAD_DIM)
                        o_h[h] = o_h[h] + jnp.sum(tmp[:, hs], axis=1, keepdims=True) * vs[:, hs]
                for h in range(H_C):
                    o_s[pl.ds(r + t0, SUBLANES), h * HEAD_DIM:(h + 1) * HEAD_DIM] = o_h[h]
            return carry

        lax.fori_loop(0, LC // LS, exact_sub, 0)

    def sub(sc, carry):
        r = pl.multiple_of(sc * LS, LS)
        bs = b_s[pl.ds(r, LS), :]
        qe = q_s[pl.ds(r, LS), :] * jnp.exp(bs)
        bl = bs[LS - 1:LS, :]
        ke = k_s[pl.ds(r, LS), :] * jnp.exp(bl - bs)
        dec = jnp.exp(bl)
        vs = v_s[pl.ds(r, LS), :]
        gg = g_ref[pl.ds(r, LS), :]
        for h in range(H_C):
            hs = slice(h * HEAD_DIM, (h + 1) * HEAD_DIM)
            st = st_s[h]
            o = _dot_nt(qe[:, hs].astype(BF16), st.astype(BF16)) + o_s[pl.ds(r, LS), hs]
            st_s[h] = dec[:, hs] * st + _dot_tn(vs[:, hs].astype(BF16), ke[:, hs].astype(BF16))
            yn = o * lax.rsqrt(jnp.mean(o * o, axis=1, keepdims=True) + EPS) * gn_ref[:, hs]
            gh = gg[:, hs]
            y_ref[pl.ds(r, LS), hs] = (yn * (gh * _sigmoid(gh))).astype(y_ref.dtype)
        return carry

    lax.fori_loop(0, LC // LS, sub, 0, unroll=True)

    @pl.when(c == last)
    def _():
        for h in range(H_C):
            sout_ref[h] = st_s[h].T


def _hgrn(proj, lb, gn_c, s0, B, T, LC, LS, out_dtype):
    nc = T // LC
    base = (3 * W_A + 4 * W_B) // W_C
    blk = lambda k: pl.BlockSpec((LC, W_C), lambda b, c: (b * nc + c, base + k))
    full2 = lambda shape: pl.BlockSpec(shape, lambda b, c: (0, 0))
    st_spec = pl.BlockSpec((None, H_C, HEAD_DIM, HEAD_DIM), lambda b, c: (b, 0, 0, 0))
    return pl.pallas_call(
        functools.partial(_hgrn_kernel, LC=LC, LS=LS),
        grid=(B, nc),
        in_specs=[blk(0), blk(1), blk(2), blk(3), full2((1, W_C)), full2((1, W_C)), st_spec],
        out_specs=[pl.BlockSpec((LC, W_C), lambda b, c: (b * nc + c, 0)), st_spec],
        out_shape=[jax.ShapeDtypeStruct((B * T, W_C), out_dtype),
                   jax.ShapeDtypeStruct((B, H_C, HEAD_DIM, HEAD_DIM), F32)],
        scratch_shapes=[pltpu.VMEM((H_C, HEAD_DIM, HEAD_DIM), F32)] + [pltpu.VMEM((LC, W_C), F32)] * 5,
        compiler_params=_cparams(("arbitrary", "arbitrary")),
        name="hgrn2",
    )(proj, proj, proj, proj, lb, gn_c, s0)


def _outproj_kernel(ya_ref, yb_ref, yc_ref, x_ref, w_ref, g_ref, b_ref, hf_ref, hb_ref, *, alpha):
    half = x_ref.shape[0] // 2
    for r in (slice(0, half), slice(half, 2 * half)):
        cat = jnp.concatenate([ya_ref[r, :].astype(BF16), yb_ref[r, :].astype(BF16), yc_ref[r, :].astype(BF16)],
                              axis=1)
        h = _layer_norm(alpha * x_ref[r, :] + _dot(cat, w_ref[...]), g_ref[...], b_ref[...])
        hf_ref[r, :] = h
        hb_ref[r, :] = h.astype(BF16)


def _outproj(ya, yb, yc, x, w, layer, g, b, alpha, tm):
    M, D = x.shape
    rows = lambda n: pl.BlockSpec((tm, n), lambda i: (i, 0))
    full = lambda shape: pl.BlockSpec(shape, lambda i: (0, 0))
    w_spec = pl.BlockSpec((None,) + w.shape[1:], lambda i: (layer, 0, 0))
    return pl.pallas_call(
        functools.partial(_outproj_kernel, alpha=alpha),
        grid=(M // tm,),
        in_specs=[rows(W_A), rows(W_B), rows(W_C), rows(D), w_spec, full((1, D)), full((1, D))],
        out_specs=[rows(D), rows(D)],
        out_shape=[jax.ShapeDtypeStruct((M, D), F32), jax.ShapeDtypeStruct((M, D), BF16)],
        compiler_params=_cparams(("arbitrary",)),
        name="out_proj_ln1",
    )(ya, yb, yc, x, w, g, b)


def _mlp_body(hb_ref, hf_ref, wu_ref, wd_ref, g_ref, b_ref, of_ref, ob_ref, acc_s, alpha, side_work=None):
    f = pl.program_id(1)

    @pl.when(f == 0)
    def _():
        acc_s[...] = jnp.zeros(acc_s.shape, F32)

    if side_work is not None:
        side_work()
    hb = hf_ref[...].astype(BF16) if hb_ref is None else hb_ref[...]
    u = jnp.maximum(_dot(hb, wu_ref[...]), 0.0)
    acc_s[...] += _dot((u * u).astype(BF16), wd_ref[...])

    @pl.when(f == pl.num_programs(1) - 1)
    def _():
        o = _layer_norm(alpha * hf_ref[...] + acc_s[...], g_ref[...], b_ref[...])
        of_ref[...] = o
        ob_ref[...] = o.astype(BF16)


def _mlp_kernel(hb_ref, hf_ref, wu_ref, wd_ref, g_ref, b_ref, of_ref, ob_ref, acc_s, *, alpha):
    _mlp_body(hb_ref, hf_ref, wu_ref, wd_ref, g_ref, b_ref, of_ref, ob_ref, acc_s, alpha)


def _mlp_kmean_kernel(pt_ref, hf_ref, wu_ref, wd_ref, g_ref, b_ref, ck_ref, of_ref, ob_ref, km_ref,
                      acc_s, pbuf, sem, *, alpha, cache_layer, pages):
    step = pl.program_id(0) * pl.num_programs(1) + pl.program_id(1)
    n_steps = pl.num_programs(0) * pl.num_programs(1)
    n_pages = pt_ref.shape[1]
    cur = step % 2

    def copies(st, slot):
        first = st * pages
        return [pltpu.make_async_copy(ck_ref.at[cache_layer, pt_ref[first // n_pages, first % n_pages + u]],
                                      pbuf.at[slot, u], sem.at[slot]) for u in range(pages)]

    @pl.when(step == 0)
    def _():
        for c in copies(0, 0):
            c.start()

    @pl.when(step + 1 < n_steps)
    def _():
        for c in copies(step + 1, 1 - cur):
            c.start()

    def page_sums():
        for c in copies(step, cur):
            c.wait()
        per_blk = MOBA_BLOCK // PAGE_SIZE
        for u in range(pages // per_blk):
            tot = pbuf[cur, per_blk * u].sum(axis=0)
            for e in range(1, per_blk):
                tot = tot + pbuf[cur, per_blk * u + e].sum(axis=0)
            km_ref[u] = tot * (1.0 / MOBA_BLOCK)

    _mlp_body(None, hf_ref, wu_ref, wd_ref, g_ref, b_ref, of_ref, ob_ref, acc_s, alpha, page_sums)


def _mlp(hb, hf, wu, wd, layer, g, b, alpha, tm, tf, kmean_job=None):
    M, D = hf.shape
    FF = wu.shape[2]
    grid = (M // tm, FF // tf)
    rows = lambda: pl.BlockSpec((tm, D), lambda i, f, *_: (i, 0))
    vec = lambda: pl.BlockSpec((1, D), lambda i, f, *_: (0, 0))
    in_specs = [rows(), rows(), pl.BlockSpec((None, D, tf), lambda i, f, *_: (layer, 0, f)),
                pl.BlockSpec((None, tf, D), lambda i, f, *_: (layer, f, 0)), vec(), vec()]
    out_shape = [jax.ShapeDtypeStruct((M, D), F32), jax.ShapeDtypeStruct((M, D), BF16)]
    acc = pltpu.VMEM((tm, D), F32)
    if kmean_job is None:
        return pl.pallas_call(
            functools.partial(_mlp_kernel, alpha=alpha),
            grid=grid, in_specs=in_specs, out_specs=[rows(), rows()], out_shape=out_shape,
            scratch_shapes=[acc],
            compiler_params=_cparams(("arbitrary", "arbitrary")),
            name="mlp_ln2",
        )(hb, hf, wu, wd, g, b)
    cache_k, page_table, cache_layer = kmean_job
    B, n_pages = page_table.shape
    n_steps = grid[0] * grid[1]
    per_blk = MOBA_BLOCK // PAGE_SIZE
    pages = B * n_pages // n_steps
    assert pages * n_steps == B * n_pages and pages % per_blk == 0 and n_pages % pages == 0
    nf = grid[1]
    steps_per_seq = n_pages // pages
    km_spec = pl.BlockSpec((None, pages // per_blk, H_A, HEAD_DIM),
                           lambda i, f, *_: ((i * nf + f) // steps_per_seq, (i * nf + f) % steps_per_seq, 0, 0))
    return pl.pallas_call(
        functools.partial(_mlp_kmean_kernel, alpha=alpha, cache_layer=cache_layer, pages=pages),
        grid_spec=pltpu.PrefetchScalarGridSpec(
            num_scalar_prefetch=1, grid=grid,
            in_specs=in_specs[1:] + [pl.BlockSpec(memory_space=pl.ANY)],
            out_specs=[rows(), rows(), km_spec],
            scratch_shapes=[acc, pltpu.VMEM((2, pages, PAGE_SIZE, H_A, HEAD_DIM), F32),
                            pltpu.SemaphoreType.DMA((2,))]),
        out_shape=out_shape + [jax.ShapeDtypeStruct((B, n_pages // per_blk, H_A, HEAD_DIM), F32)],
        compiler_params=_cparams(("arbitrary", "arbitrary")),
        name="mlp_ln2_kmean",
    )(page_table, hf, wu, wd, g, b, cache_k)


def _sample_select_kernel(q_ref, km_ref, o_ref, *, nb):
    out = jnp.zeros(o_ref.shape, jnp.int32)
    lane_o = lax.broadcasted_iota(jnp.int32, o_ref.shape, 1)
    for h in range(H_A):
        hs = slice(h * HEAD_DIM, (h + 1) * HEAD_DIM)
        gate = _dot_nt_hi(q_ref[:, hs], km_ref[:, hs])
        rank, lane = _topk_select(gate, nb, nb)
        for slot in range(MOBA_TOPK):
            pick = (lane < nb) & (rank == slot)
            idx = jnp.sum(jnp.where(pick, lane, 0), axis=1, keepdims=True)
            out = jnp.where(lane_o == h * 4 + slot, idx, out)
    o_ref[...] = out


def _sample_select(proj_s, kmean_pad, B, T, nb):
    return pl.pallas_call(
        functools.partial(_sample_select_kernel, nb=nb),
        grid=(B,),
        in_specs=[pl.BlockSpec((T, W_A), lambda b: (b, 0)),
                  pl.BlockSpec((None, LANES, W_A), lambda b: (b, 0, 0))],
        out_specs=pl.BlockSpec((T, LANES), lambda b: (b, 0)),
        out_shape=jax.ShapeDtypeStruct((B * T, LANES), jnp.int32),
        compiler_params=_cparams(("arbitrary",)),
        name="moba_sample_select",
    )(proj_s, kmean_pad)


def _moba_sample_kernel(sel_ref, pt_ref, rb_ref, q_ref, kn_ref, vn_ref, ck_ref, cv_ref, o_ref,
                        kbuf, vbuf, sem, *, T, past, layer):
    b = pl.program_id(0)
    h = pl.program_id(1)
    nh = pl.num_programs(1)
    step = b * nh + h
    n_steps = pl.num_programs(0) * nh
    per_blk = MOBA_BLOCK // PAGE_SIZE

    def copies(bb, hh, buf, qi, slot, e):
        blk = sel_ref[bb * T + qi, hh * 4 + slot]
        page = pt_ref[bb, blk * per_blk + e]
        idx = qi * MOBA_TOPK + slot
        dst = pl.ds(e * PAGE_SIZE, PAGE_SIZE)
        return (pltpu.make_async_copy(ck_ref.at[layer, page, :, hh, :], kbuf.at[buf, idx, dst, :], sem.at[buf, 0]),
                pltpu.make_async_copy(cv_ref.at[layer, page, :, hh, :], vbuf.at[buf, idx, dst, :], sem.at[buf, 1]))

    def for_all_copies(bb, hh, buf, fn):
        for qi in range(T):
            for slot in range(MOBA_TOPK):
                for e in range(per_blk):
                    for c in copies(bb, hh, buf, qi, slot, e):
                        fn(c)

    cur = step % 2

    @pl.when(step == 0)
    def _():
        for_all_copies(b, h, 0, lambda c: c.start())

    @pl.when(step + 1 < n_steps)
    def _():
        nxt = step + 1
        for_all_copies(nxt // nh, nxt % nh, 1 - cur, lambda c: c.start())

    q = q_ref[...]
    qb = q.astype(BF16)
    scale = HEAD_DIM ** -0.5
    rowT = lax.broadcasted_iota(jnp.int32, (T, T), 0)
    colT = lax.broadcasted_iota(jnp.int32, (T, T), 1)
    s_own = _dot_nt(qb, kn_ref[...].astype(BF16)) * scale + _t5_bias_from_dist(rowT - colT, rb_ref, h)
    s_own = jnp.where(rowT >= colT, s_own, NEG_BIG)

    for_all_copies(b, h, cur, lambda c: c.wait())

    rowB = lax.broadcasted_iota(jnp.int32, (T, MOBA_BLOCK), 0)
    colB = lax.broadcasted_iota(jnp.int32, (T, MOBA_BLOCK), 1)
    row1 = lax.broadcasted_iota(jnp.int32, (T, 1), 0)
    s_slot = []
    for slot in range(MOBA_TOPK):
        s = jnp.zeros((T, MOBA_BLOCK), F32)
        blk_col = jnp.zeros((T, 1), jnp.int32)
        for qi in range(T):
            sq = _dot_nt(qb, kbuf[cur, qi * MOBA_TOPK + slot].astype(BF16))
            s = jnp.where(rowB == qi, sq, s)
            blk_col = jnp.where(row1 == qi, sel_ref[b * T + qi, h * 4 + slot], blk_col)
        dist = past + rowB - (blk_col * MOBA_BLOCK + colB)
        s_slot.append(s * scale + _t5_bias_from_dist(dist, rb_ref, h))

    m = jnp.max(s_own, axis=1, keepdims=True)
    for s in s_slot:
        m = jnp.maximum(m, jnp.max(s, axis=1, keepdims=True))
    p_own = jnp.exp(s_own - m)
    l = jnp.sum(p_own, axis=1, keepdims=True)
    acc = _dot(p_own.astype(BF16), vn_ref[...].astype(BF16))
    for slot in range(MOBA_TOPK):
        p = jnp.exp(s_slot[slot] - m)
        l = l + jnp.sum(p, axis=1, keepdims=True)
        for qi in range(T):
            pq = jnp.where(rowB == qi, p, 0.0).astype(BF16)
            acc = acc + _dot(pq, vbuf[cur, qi * MOBA_TOPK + slot].astype(BF16))
    o_ref[...] = acc / l


def _moba_sample(sel, page_table, rel_bias, proj_s, cache_k, cache_v, B, T, past, layer):
    n_slots = T * MOBA_TOPK
    return pl.pallas_call(
        functools.partial(_moba_sample_kernel, T=T, past=past, layer=layer),
        grid_spec=pltpu.PrefetchScalarGridSpec(
            num_scalar_prefetch=3,
            grid=(B, H_A),
            in_specs=[pl.BlockSpec((T, HEAD_DIM), lambda b, h, *_: (b, h)),
                      pl.BlockSpec((T, HEAD_DIM), lambda b, h, *_: (b, H_A + h)),
                      pl.BlockSpec((T, HEAD_DIM), lambda b, h, *_: (b, 2 * H_A + h)),
                      pl.BlockSpec(memory_space=pl.ANY),
                      pl.BlockSpec(memory_space=pl.ANY)],
            out_specs=pl.BlockSpec((T, HEAD_DIM), lambda b, h, *_: (b, h)),
            scratch_shapes=[pltpu.VMEM((2, n_slots, MOBA_BLOCK, HEAD_DIM), F32),
                            pltpu.VMEM((2, n_slots, MOBA_BLOCK, HEAD_DIM), F32),
                            pltpu.SemaphoreType.DMA((2, 2))],
        ),
        out_shape=jax.ShapeDtypeStruct((B * T, W_A), F32),
        compiler_params=_cparams(("arbitrary", "arbitrary")),
        name="moba_sample",
    )(sel, page_table, rel_bias, proj_s, proj_s, proj_s, cache_k, cache_v)


def _pad_rows(a, rows):
    return jnp.pad(a, ((0, 0), (0, rows - a.shape[1])) + ((0, 0),) * (a.ndim - 2))


def _mixer_states_in(c0, n0, m0, conv0):
    B = c0.shape[0]
    n0p = _pad_rows(n0, SUBLANES)
    m0p = _pad_rows(jnp.broadcast_to(m0[:, :, None], (B, H_B, LANES)), SUBLANES)
    conv0p = jnp.pad(conv0, ((0, 0), (SUBLANES - (CONV_W - 1), 0), (0, 0)))
    return c0, n0p, m0p, conv0p


def _tile(m, pref):
    return pref if m % pref == 0 else m


def _layer(x_f32, x_bf16, B, T, layer, wts, states, attn_fn, mlstm_chunk, hgrn_chunk, kmean_job=None):
    (w_main, w_gate, bgate, conv_w, conv_b, gn_b, gn_c, lb, w_out, ln1_g, ln1_b, w_up, w_down,
     ln2_g, ln2_b, alpha) = wts
    c0, n0, m0, conv0, s0 = states
    M = B * T
    tm = _tile(M, 2048)
    tn = 512
    proj = _matmul(x_bf16, w_main, layer, tm, tn, 0, N_MAIN // tn)
    ya, k_new, v_new = attn_fn(proj)
    c0, n0p, m0p, conv0p = _mixer_states_in(c0, n0, m0, conv0)
    y_dtype = BF16 if T % 16 == 0 else F32
    x_gate = x_bf16 if T % 16 == 0 else x_f32
    yb, c_new, n_new, m_new = _mlstm(proj, x_gate, w_gate, layer, conv0p, conv_w, conv_b, bgate, gn_b, c0, n0p, m0p,
                                     B, T, mlstm_chunk, y_dtype)
    yc, s_new = _hgrn(proj, lb, gn_c, s0, B, T, hgrn_chunk, min(HGRN_SUB, hgrn_chunk), y_dtype)
    tm2 = _tile(M, 512)
    hf, hb = _outproj(ya, yb, yc, x_f32, w_out, layer, ln1_g, ln1_b, alpha, tm2)
    of, ob, *km = _mlp(hb, hf, w_up, w_down, layer, ln2_g, ln2_b, alpha, tm2, 1024, kmean_job)
    conv_new = proj.reshape(B, T, N_MAIN)[:, T - (CONV_W - 1):, 3 * W_A:3 * W_A + 2 * W_B]
    return of, ob, (k_new, v_new, c_new, n_new[:, :H_B, :], m_new[:, :H_B, 0], conv_new, s_new), km


def kernel(x_prompt, x_sample, cache_k, cache_v, page_table, state_b_C, state_b_n, state_b_m, state_b_conv,
           state_c_S, w_in, b_gate, conv_w, conv_b, gn_b, gn_c, lower_bounds, rel_bias, w_out, ln1_g, ln1_b,
           w_up, w_down, ln2_g, ln2_b):
    depth = w_in.shape[0]
    Bp, Tp, D = x_prompt.shape
    Bs, Ts, _ = x_sample.shape
    n_pages = page_table.shape[1]
    past = n_pages * PAGE_SIZE
    alpha = (2 * depth) ** 0.25

    sm = jax.nn.softmax(lower_bounds.astype(F32), axis=0)
    lb_all = jnp.cumsum(sm, axis=0) - sm[0]

    w_main = jnp.concatenate([w_in[:, :, :GATE_COL0], w_in[:, :, GATE_COL0 + 2 * H_B:]], axis=-1).astype(BF16)
    w_gate = jnp.pad(w_in[:, :, GATE_COL0:GATE_COL0 + 2 * H_B], ((0, 0), (0, 0), (0, LANES - 2 * H_B))).astype(BF16)
    bgate = jnp.pad(b_gate, ((0, 0), (0, LANES - 2 * H_B)))[:, None, :]
    w_out_b = w_out.astype(BF16)
    w_up_b = w_up.astype(BF16)
    w_down_b = w_down.astype(BF16)

    bias_tab = _bias_table(rel_bias, Tp // MOBA_BLOCK)
    nb_past = past // MOBA_BLOCK

    zeros_p = (jnp.zeros((Bp, H_B, HEAD_DIM, HEAD_DIM), F32), jnp.zeros((Bp, H_B, HEAD_DIM), F32),
               jnp.zeros((Bp, H_B), F32), jnp.zeros((Bp, CONV_W - 1, 2 * W_B), F32),
               jnp.zeros((Bp, H_C, HEAD_DIM, HEAD_DIM), F32))

    xp_f = x_prompt.reshape(Bp * Tp, D)
    xs_f = x_sample.reshape(Bs * Ts, D)
    xp_b = xp_f.astype(BF16)
    xs_b = xs_f.astype(BF16)
    outs = [[] for _ in range(14)]
    mlstm_chunk_p = math.gcd(Tp, 256)
    hgrn_chunk_p = math.gcd(Tp, 256)
    for l in range(depth):
        wts = (w_main, w_gate, bgate[l], conv_w[l], conv_b[l][None, :], gn_b[l][None, :], gn_c[l][None, :],
               lb_all[l][None, :], w_out_b, ln1_g[l][None, :], ln1_b[l][None, :], w_up_b, w_down_b,
               ln2_g[l][None, :], ln2_b[l][None, :], alpha)

        attn_p = lambda proj: _moba_prompt(proj, bias_tab, Bp, Tp)
        xp_f, xp_b, (kp, vp, Cp, nP, mP, cP, SP), (kmean,) = _layer(
            xp_f, xp_b, Bp, Tp, l, wts, zeros_p, attn_p, mlstm_chunk_p, hgrn_chunk_p, (cache_k, page_table, l))
        kmean_pad = jnp.pad(kmean.reshape(Bs, nb_past, W_A), ((0, 0), (0, LANES - nb_past), (0, 0)))

        def attn_s(proj, l=l, kmean_pad=kmean_pad):
            sel = _sample_select(proj, kmean_pad, Bs, Ts, nb_past)
            ya = _moba_sample(sel, page_table, rel_bias, proj, cache_k, cache_v, Bs, Ts, past, l)
            kv = proj[:, W_A:3 * W_A].reshape(Bs, Ts, 2, H_A, HEAD_DIM)
            return ya, kv[:, :, 0], kv[:, :, 1]

        st_s = (state_b_C[l], state_b_n[l], state_b_m[l], state_b_conv[l], state_c_S[l])
        xs_f, xs_b, (ks, vs, Cs, nS, mS, cS, SS), _ = _layer(xs_f, xs_b, Bs, Ts, l, wts, st_s, attn_s, Ts, Ts)

        for lst, val in zip(outs, (kp, vp, ks, vs, Cp, nP, mP, cP, Cs, nS, mS, cS, SP, SS)):
            lst.append(val)

    return (xp_f.reshape(Bp, Tp, D), xs_f.reshape(Bs, Ts, D)) + tuple(jnp.stack(o) for o in outs)
```

```python
import functools
import math

import numpy as np
import jax
import jax.numpy as jnp
from jax import lax
from jax.experimental import pallas as pl
from jax.experimental.pallas import tpu as pltpu

F32 = jnp.float32
BF16 = jnp.bfloat16

HEAD_DIM = 128
H_A, H_B, H_C = 8, 4, 4
W_A, W_B, W_C = H_A * HEAD_DIM, H_B * HEAD_DIM, H_C * HEAD_DIM
MOBA_BLOCK = 256
MOBA_TOPK = 3
NUM_BUCKETS = 32
MAX_DISTANCE = 2048
CONV_W = 4
EPS = 1e-5
GATE_MASK = -1e30
NEG_BIG = -1e30
LOG2E = math.log2(math.e)
PAGE_SIZE = 128
LANES = 128
SUBLANES = 8
HGRN_SUB = 32
HGRN_SAFE_DECAY = 60.0
MOBA_GROUP = 4
QB = 4
VT_EXTRA = 16
VMEM_LIMIT = 56 * 1024 * 1024

N_MAIN = 3 * W_A + 4 * W_B + 4 * W_C
GATE_COL0 = 3 * W_A + 3 * W_B


def _t5_thresholds():
    max_exact = NUM_BUCKETS // 2
    n = np.arange(1, 4 * MAX_DISTANCE, dtype=np.float32)
    large = max_exact + (np.log(n / np.float32(max_exact)) / np.float32(math.log(MAX_DISTANCE / max_exact))
                         * np.float32(NUM_BUCKETS - max_exact)).astype(np.int32)
    large = np.minimum(large, NUM_BUCKETS - 1)
    thr = []
    for b in range(max_exact + 1, NUM_BUCKETS):
        thr.append(int(np.argmax(large >= b)) + 1)
    return tuple(thr)


T5_THRESHOLDS = _t5_thresholds()


def _cparams(sem):
    return pltpu.CompilerParams(dimension_semantics=sem, vmem_limit_bytes=VMEM_LIMIT)


def _dot(a, b):
    return jnp.dot(a, b, preferred_element_type=F32)


def _dot_nt(a, b):
    return lax.dot_general(a, b, (((1,), (1,)), ((), ())), preferred_element_type=F32)


def _dot_tn(a, b):
    return lax.dot_general(a, b, (((0,), (0,)), ((), ())), preferred_element_type=F32)


def _dot_hi(a, b):
    return jnp.dot(a, b, precision=lax.Precision.HIGHEST, preferred_element_type=F32)


def _dot_nt_hi(a, b):
    return lax.dot_general(a, b, (((1,), (1,)), ((), ())), precision=lax.Precision.HIGHEST,
                           preferred_element_type=F32)


def _sigmoid(x):
    return 1.0 / (1.0 + jnp.exp(-x))


def _layer_norm(z, g, b):
    mu = jnp.mean(z, axis=-1, keepdims=True)
    zc = z - mu
    var = jnp.mean(zc * zc, axis=-1, keepdims=True)
    return zc * lax.rsqrt(var + EPS) * g + b


def _matmul_kernel(x_ref, w_ref, o_ref):
    o_ref[...] = _dot(x_ref[...], w_ref[...])


def _matmul(x, w, layer, tm, tn, col_blk0, n_blk):
    M, K = x.shape
    N = n_blk * tn
    return pl.pallas_call(
        _matmul_kernel,
        grid=(M // tm, n_blk),
        in_specs=[pl.BlockSpec((tm, K), lambda i, j: (i, 0)),
                  pl.BlockSpec((None, K, tn), lambda i, j: (layer, 0, col_blk0 + j))],
        out_specs=pl.BlockSpec((tm, tn), lambda i, j: (i, j)),
        out_shape=jax.ShapeDtypeStruct((M, N), F32),
        compiler_params=_cparams(("arbitrary", "arbitrary")),
        name="in_proj",
    )(x, w)


def _t5_bias_from_dist(dist, rb_ref, h):
    n = jnp.maximum(dist, 0)
    large = jnp.full(n.shape, NUM_BUCKETS // 2, jnp.int32)
    for thr in T5_THRESHOLDS:
        large = large + (n >= thr).astype(jnp.int32)
    bucket = jnp.where(n < NUM_BUCKETS // 2, n, large)
    val = jnp.zeros(n.shape, F32)
    for b in range(NUM_BUCKETS):
        val = jnp.where(bucket == b, rb_ref[b, h], val)
    return val


def _bias_table_kernel(rb_ref, o_ref, *, nb):
    h = pl.program_id(0)
    blk = MOBA_BLOCK
    d_const = -(-(T5_THRESHOLDS[-1] - 1) // blk) + 1
    row = lax.broadcasted_iota(jnp.int32, (blk, blk), 0)
    col = lax.broadcasted_iota(jnp.int32, (blk, blk), 1)
    for e in range(2 * nb - 1):
        d = nb - 1 - e
        if d >= d_const:
            tile = jnp.full((blk, blk), rb_ref[NUM_BUCKETS - 1, h], F32)
        elif d < 0:
            tile = jnp.full((blk, blk), rb_ref[0, h], F32)
        else:
            tile = _t5_bias_from_dist(d * blk + col - row, rb_ref, h)
        o_ref[e * blk:(e + 1) * blk, :] = tile * LOG2E


def _bias_table(rel_bias, nb):
    ne = 2 * nb - 1
    return pl.pallas_call(
        functools.partial(_bias_table_kernel, nb=nb),
        grid=(H_A,),
        in_specs=[pl.BlockSpec(memory_space=pltpu.SMEM)],
        out_specs=pl.BlockSpec((None, ne * MOBA_BLOCK, MOBA_BLOCK), lambda h: (h, 0, 0)),
        out_shape=jax.ShapeDtypeStruct((H_A, ne * MOBA_BLOCK, MOBA_BLOCK), F32),
        compiler_params=_cparams(("arbitrary",)),
        name="t5_bias_table",
    )(rel_bias)


def _topk_select(gate, n_valid, n_cand):
    lane = lax.broadcasted_iota(jnp.int32, gate.shape, 1)
    gm = jnp.where(lane < n_valid, gate, GATE_MASK)
    rank = jnp.zeros(gate.shape, jnp.int32)
    for c in range(n_cand):
        gc = gm[:, c:c + 1]
        ahead = (gc > gm) | ((gc == gm) & (c < lane))
        rank = rank + ahead.astype(jnp.int32)
    return rank, lane


def _moba_prompt_kernel(q_ref, k_ref, v_ref, bias_ref, o_ref, kout_ref, vout_ref,
                        ka_s, vt_s, vtd_s, qa_s, kmean_s, sem, *, nb):
    b = pl.program_id(0)
    h = pl.program_id(1)
    i = pl.program_id(2)
    blk = MOBA_BLOCK
    G = MOBA_GROUP
    nbp = -(-nb // SUBLANES) * SUBLANES

    kv_copies = (pltpu.make_async_copy(k_ref, kout_ref.at[b, :, h, :], sem.at[0]),
                 pltpu.make_async_copy(v_ref, vout_ref.at[b, :, h, :], sem.at[1]))

    @pl.when(i == 0)
    def _():
        for c in kv_copies:
            c.start()
        kmean_s[...] = jnp.zeros(kmean_s.shape, F32)
        lane = lax.broadcasted_iota(jnp.int32, (blk, HEAD_DIM), 1)
        for n in range(nb):
            rows = slice((n % G) * blk, (n % G + 1) * blk)
            kf = k_ref[n * blk:(n + 1) * blk, :]
            ka_s[n // G, rows, 0:HEAD_DIM] = kf.astype(BF16)
            ka_s[n // G, rows, HEAD_DIM:2 * HEAD_DIM] = jnp.where(lane == n, NEG_BIG, 0.0).astype(BF16)
            ones_row = jnp.where(lax.broadcasted_iota(jnp.int32, (VT_EXTRA, blk), 0) == 0, 1.0, 0.0)
            vt = jnp.concatenate([v_ref[n * blk:(n + 1) * blk, :].T, ones_row], axis=0).astype(BF16)
            vt_s[n // G, :, rows] = vt
            vtd_s[n] = vt
            kmean_s[n:n + 1, :] = jnp.mean(kf, axis=0, keepdims=True)
        km = kmean_s[...]
        sub = lax.broadcasted_iota(jnp.int32, (nbp, blk), 0)
        pad = jnp.zeros((HEAD_DIM - nbp, blk), F32)
        for t in range(nb):
            q = q_ref[t * blk:(t + 1) * blk, :]
            if t > MOBA_TOPK:
                gm = jnp.where(sub < t, _dot_nt_hi(km, q), GATE_MASK)
                rank = jnp.zeros((nbp, blk), jnp.int32)
                for c in range(t):
                    gc = gm[c:c + 1, :]
                    rank = rank + ((gc > gm) | ((gc == gm) & (c < sub))).astype(jnp.int32)
                notsel = jnp.where((sub < t) & (rank < MOBA_TOPK), 0.0, 1.0)
            else:
                notsel = jnp.where(sub < t, 0.0, 1.0)
            qt = (q * (HEAD_DIM ** -0.5 * LOG2E)).T
            qa_s[t // QB, :, (t % QB) * blk:(t % QB + 1) * blk] = (
                jnp.concatenate([qt, notsel, pad], axis=0).astype(BF16))

    qaug = qa_s[i]
    rowk = lax.broadcasted_iota(jnp.int32, (blk, blk), 0)
    colq = lax.broadcasted_iota(jnp.int32, (blk, blk), 1)

    def tile(n_grp):
        m_parts, acc_parts = [], []
        for u in range(QB):
            iu = i * QB + u
            kd = ka_s[iu // G, pl.ds(pl.multiple_of((iu % G) * blk, blk), blk), 0:HEAD_DIM]
            sd = _dot(kd, qaug[0:HEAD_DIM, u * blk:(u + 1) * blk]) + bias_ref[(nb - 1) * blk:nb * blk, :]
            sd = jnp.where(colq >= rowk, sd, NEG_BIG)
            mu = jnp.max(sd, axis=0, keepdims=True)
            m_parts.append(mu)
            acc_parts.append(_dot(vtd_s[iu], jnp.exp2(sd - mu).astype(BF16)))
        m = jnp.concatenate(m_parts, axis=1)
        acc = jnp.concatenate(acc_parts, axis=1)
        for g in range(n_grp - 1):
            bias = jnp.concatenate(
                [bias_ref[pl.ds(pl.multiple_of((nb - 1 - (i * QB + u) + G * g) * blk, blk), G * blk), :]
                 for u in range(QB)], axis=1)
            s = _dot(ka_s[g], qaug) + bias
            m_new = jnp.maximum(m, jnp.max(s, axis=0, keepdims=True))
            acc = jnp.exp2(m - m_new) * acc + _dot(vt_s[g], jnp.exp2(s - m_new).astype(BF16))
            m = m_new
        for a in range(G - 1):
            c0 = (a + 1) * blk
            bias = jnp.concatenate([bias_ref[(nb - 1 - (u - a)) * blk:(nb - (u - a)) * blk, :]
                                    for u in range(a + 1, QB)], axis=1)
            s = _dot(ka_s[i, a * blk:(a + 1) * blk, :], qaug[:, c0:]) + bias
            m_t = m[:, c0:]
            m_new = jnp.maximum(m_t, jnp.max(s, axis=0, keepdims=True))
            acc_t = jnp.exp2(m_t - m_new) * acc[:, c0:] + _dot(vtd_s[i * G + a], jnp.exp2(s - m_new).astype(BF16))
            m = jnp.concatenate([m[:, :c0], m_new], axis=1)
            acc = jnp.concatenate([acc[:, :c0], acc_t], axis=1)
        o_ref[...] = (acc[0:HEAD_DIM] / acc[HEAD_DIM:HEAD_DIM + 1]).T.astype(o_ref.dtype)

    for grp in range(nb // G):
        pl.when(i == grp)(functools.partial(tile, grp + 1))

    @pl.when(i == nb // QB - 1)
    def _():
        for c in kv_copies:
            c.wait()


def _moba_prompt(proj, bias_tab, B, T):
    nb = T // MOBA_BLOCK
    G = MOBA_GROUP
    assert nb % G == 0 and G == QB and nb >= 2 * G, "a step's query blocks are exactly one key group"
    nq = nb // QB
    kv_shape = jax.ShapeDtypeStruct((B, T, H_A, HEAD_DIM), F32)
    return pl.pallas_call(
        functools.partial(_moba_prompt_kernel, nb=nb),
        grid=(B, H_A, nq),
        in_specs=[pl.BlockSpec((T, HEAD_DIM), lambda b, h, i: (b, h)),
                  pl.BlockSpec((T, HEAD_DIM), lambda b, h, i: (b, H_A + h)),
                  pl.BlockSpec((T, HEAD_DIM), lambda b, h, i: (b, 2 * H_A + h)),
                  pl.BlockSpec((None, (2 * nb - 1) * MOBA_BLOCK, MOBA_BLOCK), lambda b, h, i: (h, 0, 0))],
        out_specs=[pl.BlockSpec((QB * MOBA_BLOCK, HEAD_DIM), lambda b, h, i: (b * nq + i, h)),
                   pl.BlockSpec(memory_space=pl.ANY), pl.BlockSpec(memory_space=pl.ANY)],
        out_shape=[jax.ShapeDtypeStruct((B * T, W_A), BF16), kv_shape, kv_shape],
        scratch_shapes=[pltpu.VMEM((nb // G, G * MOBA_BLOCK, 2 * HEAD_DIM), BF16),
                        pltpu.VMEM((nb // G, HEAD_DIM + VT_EXTRA, G * MOBA_BLOCK), BF16),
                        pltpu.VMEM((nb, HEAD_DIM + VT_EXTRA, MOBA_BLOCK), BF16),
                        pltpu.VMEM((nq, 2 * HEAD_DIM, QB * MOBA_BLOCK), BF16),
                        pltpu.VMEM((-(-nb // SUBLANES) * SUBLANES, HEAD_DIM), F32),
                        pltpu.SemaphoreType.DMA((2,))],
        compiler_params=_cparams(("arbitrary", "arbitrary", "arbitrary")),
        name="moba_prompt",
    )(proj, proj, proj, bias_tab)


def _mlstm_kernel(q_ref, k_ref, qp_ref, kp_ref, v_ref, og_ref, x_ref, wg_ref, conv0_ref, cw_ref, cb_ref, bg_ref,
                  gn_ref, c0_ref, n0_ref, m0_ref,
                  y_ref, cout_ref, nout_ref, mout_ref,
                  c_s, n_s, m_s, ext_s, *, L):
    c = pl.program_id(1)
    last = pl.num_programs(1) - 1

    @pl.when(c == 0)
    def _():
        c_s[...] = c0_ref[...]
        n_s[...] = n0_ref[...]
        m_s[...] = m0_ref[...]

    def conv_silu(u_ref, up_ref, col0):
        u = u_ref[...]
        tail = jnp.where(c == 0, conv0_ref[:, col0:col0 + W_B], up_ref[L - SUBLANES:L, :])
        ext_s[0:SUBLANES, :] = tail
        ext_s[SUBLANES:SUBLANES + L, :] = u
        acc = u * cw_ref[CONV_W - 1:CONV_W, col0:col0 + W_B] + cb_ref[:, col0:col0 + W_B]
        for j in range(1, CONV_W):
            xj = ext_s[SUBLANES - j:SUBLANES - j + L, :]
            acc = acc + xj * cw_ref[CONV_W - 1 - j:CONV_W - j, col0:col0 + W_B]
        return acc * _sigmoid(acc)

    qc = conv_silu(q_ref, qp_ref, 0)
    kc = conv_silu(k_ref, kp_ref, W_B) * (HEAD_DIM ** -0.5)
    v = v_ref[...]
    og = og_ref[...]

    g = _dot(x_ref[...].astype(BF16), wg_ref[...]) + bg_ref[...]
    lf = jnp.minimum(g, 0.0) - jnp.log(1.0 + jnp.exp(-jnp.abs(g)))
    row = lax.broadcasted_iota(jnp.int32, (L, L), 0)
    col = lax.broadcasted_iota(jnp.int32, (L, L), 1)
    causal = row >= col
    fcum = _dot_hi(causal.astype(F32), lf)

    for h in range(H_B):
        hs = slice(h * HEAD_DIM, (h + 1) * HEAD_DIM)
        fcol = fcum[:, H_B + h:H_B + h + 1]
        rcol = g[:, h:h + 1] - fcol
        rrow = jnp.sum(jnp.where(row == col, rcol, 0.0), axis=0, keepdims=True)
        dm = jnp.where(causal, fcol + rrow, -jnp.inf)
        mprev = m_s[h:h + 1, 0:1]
        gcol = fcol + mprev
        mt = jnp.maximum(gcol, jnp.max(dm, axis=1, keepdims=True))
        w = jnp.exp(dm - mt)
        wg = jnp.exp(gcol - mt)
        qh = qc[:, hs]
        kh = kc[:, hs]
        vh = v[:, hs]
        qhb = qh.astype(BF16)
        s = _dot_nt(qhb, kh.astype(BF16)) * w
        num = _dot(s.astype(BF16), vh.astype(BF16)) + wg * _dot(qhb, c_s[h].astype(BF16))
        den = jnp.sum(s, axis=1, keepdims=True) + wg * jnp.sum(qh * n_s[h:h + 1, :], axis=1, keepdims=True)
        hh = num / jnp.maximum(jnp.abs(den), jnp.exp(-mt))
        ml = mt[L - 1:L, :]
        wl = jnp.exp(fcol[L - 1:L, :] + rcol - ml)
        gl = jnp.exp(gcol[L - 1:L, :] - ml)
        kw = kh * wl
        c_s[h] = gl * c_s[h] + _dot_tn(kw.astype(BF16), vh.astype(BF16))
        n_s[h:h + 1, :] = gl * n_s[h:h + 1, :] + jnp.sum(kw, axis=0, keepdims=True)
        m_s[h:h + 1, :] = jnp.broadcast_to(ml, (1, LANES))
        hc = hh - jnp.mean(hh, axis=1, keepdims=True)
        yn = hc * lax.rsqrt(jnp.mean(hc * hc, axis=1, keepdims=True) + EPS) * gn_ref[:, hs]
        y_ref[:, hs] = (_sigmoid(og[:, hs]) * yn).astype(y_ref.dtype)

    @pl.when(c == last)
    def _():
        cout_ref[...] = c_s[...]
        nout_ref[...] = n_s[...]
        mout_ref[...] = m_s[...]


def _mlstm(proj, x, w_gate, layer, conv0, conv_w, conv_b, bgate, gn_b, c0, n0, m0, B, T, L, out_dtype):
    nc = T // L
    q_blk = 3 * W_A // W_B
    k_blk, v_blk, o_blk = q_blk + 1, q_blk + 2, q_blk + 3

    def cur(colblk):
        return pl.BlockSpec((L, W_B), lambda b, c: (b * nc + c, colblk))

    def prev(colblk):
        return pl.BlockSpec((L, W_B), lambda b, c: (b * nc + jnp.maximum(c - 1, 0), colblk))

    full2 = lambda shape: pl.BlockSpec(shape, lambda b, c: (0, 0))
    per_b3 = lambda shape: pl.BlockSpec((None,) + shape, lambda b, c: (b, 0, 0))
    return pl.pallas_call(
        functools.partial(_mlstm_kernel, L=L),
        grid=(B, nc),
        in_specs=[cur(q_blk), cur(k_blk), prev(q_blk), prev(k_blk), cur(v_blk), cur(o_blk),
                  pl.BlockSpec((L, x.shape[1]), lambda b, c: (b * nc + c, 0)),
                  pl.BlockSpec((None,) + w_gate.shape[1:], lambda b, c: (layer, 0, 0)),
                  per_b3((SUBLANES, 2 * W_B)),
                  full2((CONV_W, 2 * W_B)), full2((1, 2 * W_B)), full2((1, LANES)), full2((1, W_B)),
                  pl.BlockSpec((None, H_B, HEAD_DIM, HEAD_DIM), lambda b, c: (b, 0, 0, 0)),
                  per_b3((SUBLANES, HEAD_DIM)), per_b3((SUBLANES, LANES))],
        out_specs=[pl.BlockSpec((L, W_B), lambda b, c: (b * nc + c, 0)),
                   pl.BlockSpec((None, H_B, HEAD_DIM, HEAD_DIM), lambda b, c: (b, 0, 0, 0)),
                   per_b3((SUBLANES, HEAD_DIM)), per_b3((SUBLANES, LANES))],
        out_shape=[jax.ShapeDtypeStruct((B * T, W_B), out_dtype),
                   jax.ShapeDtypeStruct((B, H_B, HEAD_DIM, HEAD_DIM), F32),
                   jax.ShapeDtypeStruct((B, SUBLANES, HEAD_DIM), F32),
                   jax.ShapeDtypeStruct((B, SUBLANES, LANES), F32)],
        scratch_shapes=[pltpu.VMEM((H_B, HEAD_DIM, HEAD_DIM), F32), pltpu.VMEM((SUBLANES, HEAD_DIM), F32),
                        pltpu.VMEM((SUBLANES, LANES), F32), pltpu.VMEM((L + SUBLANES, W_B), F32)],
        compiler_params=_cparams(("arbitrary", "arbitrary")),
        name="mlstm",
    )(proj, proj, proj, proj, proj, proj, x, w_gate, conv0, conv_w, conv_b, bgate, gn_b, c0, n0, m0)


def _hgrn_kernel(q_ref, f_ref, i_ref, g_ref, lb_ref, gn_ref, s0_ref, y_ref, sout_ref,
                 st_s, k_s, b_s, v_s, q_s, o_s, *, LC, LS):
    c = pl.program_id(1)
    last = pl.num_programs(1) - 1

    @pl.when(c == 0)
    def _():
        for h in range(H_C):
            st_s[h] = s0_ref[h].T

    lb = lb_ref[...]
    one_m_lb = 1.0 - lb
    row = lax.broadcasted_iota(jnp.int32, (LC, LC), 0)
    col = lax.broadcasted_iota(jnp.int32, (LC, LC), 1)
    same_sub = (row // LS) == (col // LS)
    intra = same_sub & (row >= col)
    row8 = lax.broadcasted_iota(jnp.int32, (SUBLANES, W_C), 0)

    fc = f_ref[...]
    qc = q_ref[...]
    logf = jnp.log(lb + one_m_lb * _sigmoid(fc))
    kk = one_m_lb * _sigmoid(-fc)
    qq = qc * _sigmoid(qc)
    vv = i_ref[...]
    b = _dot_hi(intra.astype(F32), logf)
    k_s[...] = kk
    b_s[...] = b
    v_s[...] = vv
    q_s[...] = qq

    safe = jnp.min(b) > -HGRN_SAFE_DECAY

    @pl.when(safe)
    def _():
        qe = qq * jnp.exp(b)
        kinv = kk * jnp.exp(-b)
        for h in range(H_C):
            hs = slice(h * HEAD_DIM, (h + 1) * HEAD_DIM)
            a = jnp.where(intra, _dot_nt(qe[:, hs].astype(BF16), kinv[:, hs].astype(BF16)), 0.0)
            o_s[:, hs] = _dot(a.astype(BF16), vv[:, hs].astype(BF16))

    @pl.when(jnp.logical_not(safe))
    def _():
        def exact_sub(sc, carry):
            r = pl.multiple_of(sc * LS, LS)
            for rb in range(LS // SUBLANES):
                t0 = rb * SUBLANES
                qb_ = q_s[pl.ds(r + t0, SUBLANES), :]
                bb_ = b_s[pl.ds(r + t0, SUBLANES), :]
                o_h = [jnp.zeros((SUBLANES, HEAD_DIM), F32) for _ in range(H_C)]
                for s in range(t0 + SUBLANES):
                    d = bb_ - b_s[pl.ds(r + s, 1), :]
                    if s >= t0:
                        d = jnp.where(row8 >= (s - t0), d, -jnp.inf)
                    tmp = qb_ * k_s[pl.ds(r + s, 1), :] * jnp.exp(d)
                    vs = v_s[pl.ds(r + s, 1), :]
                    for h in range(H_C):
                        hs = slice(h * HEAD_DIM, (h + 1) * HEAD_DIM)
                        o_h[h] = o_h[h] + jnp.sum(tmp[:, hs], axis=1, keepdims=True) * vs[:, hs]
                for h in range(H_C):
                    o_s[pl.ds(r + t0, SUBLANES), h * HEAD_DIM:(h + 1) * HEAD_DIM] = o_h[h]
            return carry

        lax.fori_loop(0, LC // LS, exact_sub, 0)

    def sub(sc, carry):
        r = pl.multiple_of(sc * LS, LS)
        bs = b_s[pl.ds(r, LS), :]
        qe = q_s[pl.ds(r, LS), :] * jnp.exp(bs)
        bl = bs[LS - 1:LS, :]
        ke = k_s[pl.ds(r, LS), :] * jnp.exp(bl - bs)
        dec = jnp.exp(bl)
        vs = v_s[pl.ds(r, LS), :]
        gg = g_ref[pl.ds(r, LS), :]
        for h in range(H_C):
            hs = slice(h * HEAD_DIM, (h + 1) * HEAD_DIM)
            st = st_s[h]
            o = _dot_nt(qe[:, hs].astype(BF16), st.astype(BF16)) + o_s[pl.ds(r, LS), hs]
            st_s[h] = dec[:, hs] * st + _dot_tn(vs[:, hs].astype(BF16), ke[:, hs].astype(BF16))
            yn = o * lax.rsqrt(jnp.mean(o * o, axis=1, keepdims=True) + EPS) * gn_ref[:, hs]
            gh = gg[:, hs]
            y_ref[pl.ds(r, LS), hs] = (yn * (gh * _sigmoid(gh))).astype(y_ref.dtype)
        return carry

    lax.fori_loop(0, LC // LS, sub, 0, unroll=True)

    @pl.when(c == last)
    def _():
        for h in range(H_C):
            sout_ref[h] = st_s[h].T


def _hgrn(proj, lb, gn_c, s0, B, T, LC, LS, out_dtype):
    nc = T // LC
    base = (3 * W_A + 4 * W_B) // W_C
    blk = lambda k: pl.BlockSpec((LC, W_C), lambda b, c: (b * nc + c, base + k))
    full2 = lambda shape: pl.BlockSpec(shape, lambda b, c: (0, 0))
    st_spec = pl.BlockSpec((None, H_C, HEAD_DIM, HEAD_DIM), lambda b, c: (b, 0, 0, 0))
    return pl.pallas_call(
        functools.partial(_hgrn_kernel, LC=LC, LS=LS),
        grid=(B, nc),
        in_specs=[blk(0), blk(1), blk(2), blk(3), full2((1, W_C)), full2((1, W_C)), st_spec],
        out_specs=[pl.BlockSpec((LC, W_C), lambda b, c: (b * nc + c, 0)), st_spec],
        out_shape=[jax.ShapeDtypeStruct((B * T, W_C), out_dtype),
                   jax.ShapeDtypeStruct((B, H_C, HEAD_DIM, HEAD_DIM), F32)],
        scratch_shapes=[pltpu.VMEM((H_C, HEAD_DIM, HEAD_DIM), F32)] + [pltpu.VMEM((LC, W_C), F32)] * 5,
        compiler_params=_cparams(("arbitrary", "arbitrary")),
        name="hgrn2",
    )(proj, proj, proj, proj, lb, gn_c, s0)


def _outproj_kernel(ya_ref, yb_ref, yc_ref, x_ref, w_ref, g_ref, b_ref, hf_ref, hb_ref, *, alpha):
    half = x_ref.shape[0] // 2
    for r in (slice(0, half), slice(half, 2 * half)):
        cat = jnp.concatenate([ya_ref[r, :].astype(BF16), yb_ref[r, :].astype(BF16), yc_ref[r, :].astype(BF16)],
                              axis=1)
        h = _layer_norm(alpha * x_ref[r, :] + _dot(cat, w_ref[...]), g_ref[...], b_ref[...])
        hf_ref[r, :] = h
        hb_ref[r, :] = h.astype(BF16)


def _outproj(ya, yb, yc, x, w, layer, g, b, alpha, tm):
    M, D = x.shape
    rows = lambda n: pl.BlockSpec((tm, n), lambda i: (i, 0))
    full = lambda shape: pl.BlockSpec(shape, lambda i: (0, 0))
    w_spec = pl.BlockSpec((None,) + w.shape[1:], lambda i: (layer, 0, 0))
    return pl.pallas_call(
        functools.partial(_outproj_kernel, alpha=alpha),
        grid=(M // tm,),
        in_specs=[rows(W_A), rows(W_B), rows(W_C), rows(D), w_spec, full((1, D)), full((1, D))],
        out_specs=[rows(D), rows(D)],
        out_shape=[jax.ShapeDtypeStruct((M, D), F32), jax.ShapeDtypeStruct((M, D), BF16)],
        compiler_params=_cparams(("arbitrary",)),
        name="out_proj_ln1",
    )(ya, yb, yc, x, w, g, b)


def _mlp_body(hb_ref, hf_ref, wu_ref, wd_ref, g_ref, b_ref, of_ref, ob_ref, acc_s, alpha, side_work=None):
    f = pl.program_id(1)

    @pl.when(f == 0)
    def _():
        acc_s[...] = jnp.zeros(acc_s.shape, F32)

    if side_work is not None:
        side_work()
    hb = hf_ref[...].astype(BF16) if hb_ref is None else hb_ref[...]
    u = jnp.maximum(_dot(hb, wu_ref[...]), 0.0)
    acc_s[...] += _dot((u * u).astype(BF16), wd_ref[...])

    @pl.when(f == pl.num_programs(1) - 1)
    def _():
        o = _layer_norm(alpha * hf_ref[...] + acc_s[...], g_ref[...], b_ref[...])
        of_ref[...] = o
        ob_ref[...] = o.astype(BF16)


def _mlp_kernel(hb_ref, hf_ref, wu_ref, wd_ref, g_ref, b_ref, of_ref, ob_ref, acc_s, *, alpha):
    _mlp_body(hb_ref, hf_ref, wu_ref, wd_ref, g_ref, b_ref, of_ref, ob_ref, acc_s, alpha)


def _mlp_kmean_kernel(pt_ref, hf_ref, wu_ref, wd_ref, g_ref, b_ref, ck_ref, of_ref, ob_ref, km_ref,
                      acc_s, pbuf, sem, *, alpha, cache_layer, pages):
    step = pl.program_id(0) * pl.num_programs(1) + pl.program_id(1)
    n_steps = pl.num_programs(0) * pl.num_programs(1)
    n_pages = pt_ref.shape[1]
    cur = step % 2

    def copies(st, slot):
        first = st * pages
        return [pltpu.make_async_copy(ck_ref.at[cache_layer, pt_ref[first // n_pages, first % n_pages + u]],
                                      pbuf.at[slot, u], sem.at[slot]) for u in range(pages)]

    @pl.when(step == 0)
    def _():
        for c in copies(0, 0):
            c.start()

    @pl.when(step + 1 < n_steps)
    def _():
        for c in copies(step + 1, 1 - cur):
            c.start()

    def page_sums():
        for c in copies(step, cur):
            c.wait()
        per_blk = MOBA_BLOCK // PAGE_SIZE
        for u in range(pages // per_blk):
            tot = pbuf[cur, per_blk * u].sum(axis=0)
            for e in range(1, per_blk):
                tot = tot + pbuf[cur, per_blk * u + e].sum(axis=0)
            km_ref[u] = tot * (1.0 / MOBA_BLOCK)

    _mlp_body(None, hf_ref, wu_ref, wd_ref, g_ref, b_ref, of_ref, ob_ref, acc_s, alpha, page_sums)


def _mlp(hb, hf, wu, wd, layer, g, b, alpha, tm, tf, kmean_job=None):
    M, D = hf.shape
    FF = wu.shape[2]
    grid = (M // tm, FF // tf)
    rows = lambda: pl.BlockSpec((tm, D), lambda i, f, *_: (i, 0))
    vec = lambda: pl.BlockSpec((1, D), lambda i, f, *_: (0, 0))
    in_specs = [rows(), rows(), pl.BlockSpec((None, D, tf), lambda i, f, *_: (layer, 0, f)),
                pl.BlockSpec((None, tf, D), lambda i, f, *_: (layer, f, 0)), vec(), vec()]
    out_shape = [jax.ShapeDtypeStruct((M, D), F32), jax.ShapeDtypeStruct((M, D), BF16)]
    acc = pltpu.VMEM((tm, D), F32)
    if kmean_job is None:
        return pl.pallas_call(
            functools.partial(_mlp_kernel, alpha=alpha),
            grid=grid, in_specs=in_specs, out_specs=[rows(), rows()], out_shape=out_shape,
            scratch_shapes=[acc],
            compiler_params=_cparams(("arbitrary", "arbitrary")),
            name="mlp_ln2",
        )(hb, hf, wu, wd, g, b)
    cache_k, page_table, cache_layer = kmean_job
    B, n_pages = page_table.shape
    n_steps = grid[0] * grid[1]
    per_blk = MOBA_BLOCK // PAGE_SIZE
    pages = B * n_pages // n_steps
    assert pages * n_steps == B * n_pages and pages % per_blk == 0 and n_pages % pages == 0
    nf = grid[1]
    steps_per_seq = n_pages // pages
    km_spec = pl.BlockSpec((None, pages // per_blk, H_A, HEAD_DIM),
                           lambda i, f, *_: ((i * nf + f) // steps_per_seq, (i * nf + f) % steps_per_seq, 0, 0))
    return pl.pallas_call(
        functools.partial(_mlp_kmean_kernel, alpha=alpha, cache_layer=cache_layer, pages=pages),
        grid_spec=pltpu.PrefetchScalarGridSpec(
            num_scalar_prefetch=1, grid=grid,
            in_specs=in_specs[1:] + [pl.BlockSpec(memory_space=pl.ANY)],
            out_specs=[rows(), rows(), km_spec],
            scratch_shapes=[acc, pltpu.VMEM((2, pages, PAGE_SIZE, H_A, HEAD_DIM), F32),
                            pltpu.SemaphoreType.DMA((2,))]),
        out_shape=out_shape + [jax.ShapeDtypeStruct((B, n_pages // per_blk, H_A, HEAD_DIM), F32)],
        compiler_params=_cparams(("arbitrary", "arbitrary")),
        name="mlp_ln2_kmean",
    )(page_table, hf, wu, wd, g, b, cache_k)


def _sample_select_kernel(q_ref, km_ref, o_ref, *, nb):
    out = jnp.zeros(o_ref.shape, jnp.int32)
    lane_o = lax.broadcasted_iota(jnp.int32, o_ref.shape, 1)
    for h in range(H_A):
        hs = slice(h * HEAD_DIM, (h + 1) * HEAD_DIM)
        gate = _dot_nt_hi(q_ref[:, hs], km_ref[:, hs])
        rank, lane = _topk_select(gate, nb, nb)
        for slot in range(MOBA_TOPK):
            pick = (lane < nb) & (rank == slot)
            idx = jnp.sum(jnp.where(pick, lane, 0), axis=1, keepdims=True)
            out = jnp.where(lane_o == h * 4 + slot, idx, out)
    o_ref[...] = out


def _sample_select(proj_s, kmean_pad, B, T, nb):
    return pl.pallas_call(
        functools.partial(_sample_select_kernel, nb=nb),
        grid=(B,),
        in_specs=[pl.BlockSpec((T, W_A), lambda b: (b, 0)),
                  pl.BlockSpec((None, LANES, W_A), lambda b: (b, 0, 0))],
        out_specs=pl.BlockSpec((T, LANES), lambda b: (b, 0)),
        out_shape=jax.ShapeDtypeStruct((B * T, LANES), jnp.int32),
        compiler_params=_cparams(("arbitrary",)),
        name="moba_sample_select",
    )(proj_s, kmean_pad)


def _moba_sample_kernel(sel_ref, pt_ref, rb_ref, q_ref, kn_ref, vn_ref, ck_ref, cv_ref, o_ref,
                        kbuf, vbuf, sem, *, T, past, layer):
    b = pl.program_id(0)
    h = pl.program_id(1)
    nh = pl.num_programs(1)
    step = b * nh + h
    n_steps = pl.num_programs(0) * nh
    per_blk = MOBA_BLOCK // PAGE_SIZE

    def copies(bb, hh, buf, qi, slot, e):
        blk = sel_ref[bb * T + qi, hh * 4 + slot]
        page = pt_ref[bb, blk * per_blk + e]
        idx = qi * MOBA_TOPK + slot
        dst = pl.ds(e * PAGE_SIZE, PAGE_SIZE)
        return (pltpu.make_async_copy(ck_ref.at[layer, page, :, hh, :], kbuf.at[buf, idx, dst, :], sem.at[buf, 0]),
                pltpu.make_async_copy(cv_ref.at[layer, page, :, hh, :], vbuf.at[buf, idx, dst, :], sem.at[buf, 1]))

    def for_all_copies(bb, hh, buf, fn):
        for qi in range(T):
            for slot in range(MOBA_TOPK):
                for e in range(per_blk):
                    for c in copies(bb, hh, buf, qi, slot, e):
                        fn(c)

    cur = step % 2

    @pl.when(step == 0)
    def _():
        for_all_copies(b, h, 0, lambda c: c.start())

    @pl.when(step + 1 < n_steps)
    def _():
        nxt = step + 1
        for_all_copies(nxt // nh, nxt % nh, 1 - cur, lambda c: c.start())

    q = q_ref[...]
    qb = q.astype(BF16)
    scale = HEAD_DIM ** -0.5
    rowT = lax.broadcasted_iota(jnp.int32, (T, T), 0)
    colT = lax.broadcasted_iota(jnp.int32, (T, T), 1)
    s_own = _dot_nt(qb, kn_ref[...].astype(BF16)) * scale + _t5_bias_from_dist(rowT - colT, rb_ref, h)
    s_own = jnp.where(rowT >= colT, s_own, NEG_BIG)

    for_all_copies(b, h, cur, lambda c: c.wait())

    rowB = lax.broadcasted_iota(jnp.int32, (T, MOBA_BLOCK), 0)
    colB = lax.broadcasted_iota(jnp.int32, (T, MOBA_BLOCK), 1)
    row1 = lax.broadcasted_iota(jnp.int32, (T, 1), 0)
    s_slot = []
    for slot in range(MOBA_TOPK):
        s = jnp.zeros((T, MOBA_BLOCK), F32)
        blk_col = jnp.zeros((T, 1), jnp.int32)
        for qi in range(T):
            sq = _dot_nt(qb, kbuf[cur, qi * MOBA_TOPK + slot].astype(BF16))
            s = jnp.where(rowB == qi, sq, s)
            blk_col = jnp.where(row1 == qi, sel_ref[b * T + qi, h * 4 + slot], blk_col)
        dist = past + rowB - (blk_col * MOBA_BLOCK + colB)
        s_slot.append(s * scale + _t5_bias_from_dist(dist, rb_ref, h))

    m = jnp.max(s_own, axis=1, keepdims=True)
    for s in s_slot:
        m = jnp.maximum(m, jnp.max(s, axis=1, keepdims=True))
    p_own = jnp.exp(s_own - m)
    l = jnp.sum(p_own, axis=1, keepdims=True)
    acc = _dot(p_own.astype(BF16), vn_ref[...].astype(BF16))
    for slot in range(MOBA_TOPK):
        p = jnp.exp(s_slot[slot] - m)
        l = l + jnp.sum(p, axis=1, keepdims=True)
        for qi in range(T):
            pq = jnp.where(rowB == qi, p, 0.0).astype(BF16)
            acc = acc + _dot(pq, vbuf[cur, qi * MOBA_TOPK + slot].astype(BF16))
    o_ref[...] = acc / l


def _moba_sample(sel, page_table, rel_bias, proj_s, cache_k, cache_v, B, T, past, layer):
    n_slots = T * MOBA_TOPK
    return pl.pallas_call(
        functools.partial(_moba_sample_kernel, T=T, past=past, layer=layer),
        grid_spec=pltpu.PrefetchScalarGridSpec(
            num_scalar_prefetch=3,
            grid=(B, H_A),
            in_specs=[pl.BlockSpec((T, HEAD_DIM), lambda b, h, *_: (b, h)),
                      pl.BlockSpec((T, HEAD_DIM), lambda b, h, *_: (b, H_A + h)),
                      pl.BlockSpec((T, HEAD_DIM), lambda b, h, *_: (b, 2 * H_A + h)),
                      pl.BlockSpec(memory_space=pl.ANY),
                      pl.BlockSpec(memory_space=pl.ANY)],
            out_specs=pl.BlockSpec((T, HEAD_DIM), lambda b, h, *_: (b, h)),
            scratch_shapes=[pltpu.VMEM((2, n_slots, MOBA_BLOCK, HEAD_DIM), F32),
                            pltpu.VMEM((2, n_slots, MOBA_BLOCK, HEAD_DIM), F32),
                            pltpu.SemaphoreType.DMA((2, 2))],
        ),
        out_shape=jax.ShapeDtypeStruct((B * T, W_A), F32),
        compiler_params=_cparams(("arbitrary", "arbitrary")),
        name="moba_sample",
    )(sel, page_table, rel_bias, proj_s, proj_s, proj_s, cache_k, cache_v)


def _pad_rows(a, rows):
    return jnp.pad(a, ((0, 0), (0, rows - a.shape[1])) + ((0, 0),) * (a.ndim - 2))


def _mixer_states_in(c0, n0, m0, conv0):
    B = c0.shape[0]
    n0p = _pad_rows(n0, SUBLANES)
    m0p = _pad_rows(jnp.broadcast_to(m0[:, :, None], (B, H_B, LANES)), SUBLANES)
    conv0p = jnp.pad(conv0, ((0, 0), (SUBLANES - (CONV_W - 1), 0), (0, 0)))
    return c0, n0p, m0p, conv0p


def _tile(m, pref):
    return pref if m % pref == 0 else m


def _layer(x_f32, x_bf16, B, T, layer, wts, states, attn_fn, mlstm_chunk, hgrn_chunk, kmean_job=None):
    (w_main, w_gate, bgate, conv_w, conv_b, gn_b, gn_c, lb, w_out, ln1_g, ln1_b, w_up, w_down,
     ln2_g, ln2_b, alpha) = wts
    c0, n0, m0, conv0, s0 = states
    M = B * T
    tm = _tile(M, 2048)
    tn = 1024
    proj = _matmul(x_bf16, w_main, layer, tm, tn, 0, N_MAIN // tn)
    ya, k_new, v_new = attn_fn(proj)
    c0, n0p, m0p, conv0p = _mixer_states_in(c0, n0, m0, conv0)
    y_dtype = BF16 if T % 16 == 0 else F32
    x_gate = x_bf16 if T % 16 == 0 else x_f32
    yb, c_new, n_new, m_new = _mlstm(proj, x_gate, w_gate, layer, conv0p, conv_w, conv_b, bgate, gn_b, c0, n0p, m0p,
                                     B, T, mlstm_chunk, y_dtype)
    yc, s_new = _hgrn(proj, lb, gn_c, s0, B, T, hgrn_chunk, min(HGRN_SUB, hgrn_chunk), y_dtype)
    tm2 = _tile(M, 512)
    hf, hb = _outproj(ya, yb, yc, x_f32, w_out, layer, ln1_g, ln1_b, alpha, tm2)
    of, ob, *km = _mlp(hb, hf, w_up, w_down, layer, ln2_g, ln2_b, alpha, tm2, 1024, kmean_job)
    conv_new = proj.reshape(B, T, N_MAIN)[:, T - (CONV_W - 1):, 3 * W_A:3 * W_A + 2 * W_B]
    return of, ob, (k_new, v_new, c_new, n_new[:, :H_B, :], m_new[:, :H_B, 0], conv_new, s_new), km


def kernel(x_prompt, x_sample, cache_k, cache_v, page_table, state_b_C, state_b_n, state_b_m, state_b_conv,
           state_c_S, w_in, b_gate, conv_w, conv_b, gn_b, gn_c, lower_bounds, rel_bias, w_out, ln1_g, ln1_b,
           w_up, w_down, ln2_g, ln2_b):
    depth = w_in.shape[0]
    Bp, Tp, D = x_prompt.shape
    Bs, Ts, _ = x_sample.shape
    n_pages = page_table.shape[1]
    past = n_pages * PAGE_SIZE
    alpha = (2 * depth) ** 0.25

    sm = jax.nn.softmax(lower_bounds.astype(F32), axis=0)
    lb_all = jnp.cumsum(sm, axis=0) - sm[0]

    w_main = jnp.concatenate([w_in[:, :, :GATE_COL0], w_in[:, :, GATE_COL0 + 2 * H_B:]], axis=-1).astype(BF16)
    w_gate = jnp.pad(w_in[:, :, GATE_COL0:GATE_COL0 + 2 * H_B], ((0, 0), (0, 0), (0, LANES - 2 * H_B))).astype(BF16)
    bgate = jnp.pad(b_gate, ((0, 0), (0, LANES - 2 * H_B)))[:, None, :]
    w_out_b = w_out.astype(BF16)
    w_up_b = w_up.astype(BF16)
    w_down_b = w_down.astype(BF16)

    bias_tab = _bias_table(rel_bias, Tp // MOBA_BLOCK)
    nb_past = past // MOBA_BLOCK

    zeros_p = (jnp.zeros((Bp, H_B, HEAD_DIM, HEAD_DIM), F32), jnp.zeros((Bp, H_B, HEAD_DIM), F32),
               jnp.zeros((Bp, H_B), F32), jnp.zeros((Bp, CONV_W - 1, 2 * W_B), F32),
               jnp.zeros((Bp, H_C, HEAD_DIM, HEAD_DIM), F32))

    xp_f = x_prompt.reshape(Bp * Tp, D)
    xs_f = x_sample.reshape(Bs * Ts, D)
    xp_b = xp_f.astype(BF16)
    xs_b = xs_f.astype(BF16)
    outs = [[] for _ in range(14)]
    mlstm_chunk_p = math.gcd(Tp, 256)
    hgrn_chunk_p = math.gcd(Tp, 256)
    for l in range(depth):
        wts = (w_main, w_gate, bgate[l], conv_w[l], conv_b[l][None, :], gn_b[l][None, :], gn_c[l][None, :],
               lb_all[l][None, :], w_out_b, ln1_g[l][None, :], ln1_b[l][None, :], w_up_b, w_down_b,
               ln2_g[l][None, :], ln2_b[l][None, :], alpha)

        attn_p = lambda proj: _moba_prompt(proj, bias_tab, Bp, Tp)
        xp_f, xp_b, (kp, vp, Cp, nP, mP, cP, SP), (kmean,) = _layer(
            xp_f, xp_b, Bp, Tp, l, wts, zeros_p, attn_p, mlstm_chunk_p, hgrn_chunk_p, (cache_k, page_table, l))
        kmean_pad = jnp.pad(kmean.reshape(Bs, nb_past, W_A), ((0, 0), (0, LANES - nb_past), (0, 0)))

        def attn_s(proj, l=l, kmean_pad=kmean_pad):
            sel = _sample_select(proj, kmean_pad, Bs, Ts, nb_past)
            ya = _moba_sample(sel, page_table, rel_bias, proj, cache_k, cache_v, Bs, Ts, past, l)
            kv = proj[:, W_A:3 * W_A].reshape(Bs, Ts, 2, H_A, HEAD_DIM)
            return ya, kv[:, :, 0], kv[:, :, 1]

        st_s = (state_b_C[l], state_b_n[l], state_b_m[l], state_b_conv[l], state_c_S[l])
        xs_f, xs_b, (ks, vs, Cs, nS, mS, cS, SS), _ = _layer(xs_f, xs_b, Bs, Ts, l, wts, st_s, attn_s, Ts, Ts)

        for lst, val in zip(outs, (kp, vp, ks, vs, Cp, nP, mP, cP, Cs, nS, mS, cS, SP, SS)):
            lst.append(val)

    return (xp_f.reshape(Bp, Tp, D), xs_f.reshape(Bs, Ts, D)) + tuple(jnp.stack(o) for o in outs)
```
